```python
import math
import jax, jax.numpy as jnp
from jax import lax
import numpy as np

D_MODEL = 1024
BATCH = 8
SEQ = 4096
DEPTH = 2
DEC_BATCH = 16
DEC_SEQ = 64
PAST_LEN = 4096

CHUNK = 64
N_HEADS = D_MODEL // 128
HEAD_DIM = 64
ATT_DIM = N_HEADS * HEAD_DIM
SSM_DIM = D_MODEL // 4
SSM_GROUP = 16
SSM_GROUPS = SSM_DIM // SSM_GROUP
SSM_STATE = 64
CONV_DIM = D_MODEL // 4
CONV_WIDTH = 3
MIX_DIM = ATT_DIM + SSM_DIM + CONV_DIM
D_FF = 4 * D_MODEL
Q_BLOCK = 128
RMS_EPS = 1e-6
NEG_INF = -1e30
DT_MIN = 0.001
DT_MAX = 0.1
SPLITS = (ATT_DIM, 2 * ATT_DIM, 3 * ATT_DIM, 3 * ATT_DIM + N_HEADS,
          3 * ATT_DIM + N_HEADS + SSM_DIM, 3 * ATT_DIM + N_HEADS + SSM_DIM + CONV_DIM,
          3 * ATT_DIM + N_HEADS + SSM_DIM + 2 * CONV_DIM)
PROJ_DIM = 3 * ATT_DIM + N_HEADS + SSM_DIM + 3 * CONV_DIM

kernel_name = 'hybrid_fox_s5_shortconv_stream_step'


def _rms(x, w):
    xf = x.astype(jnp.float32)
    y = xf * lax.rsqrt(jnp.mean(xf * xf, axis=-1, keepdims=True) + RMS_EPS)
    return (y * w.astype(jnp.float32)).astype(x.dtype)


def _fox_block(q, k, v, c_q, c_k, q_pos, k_pos):
    s = jnp.einsum('bqhd,bkhd->bhqk', q, k).astype(jnp.float32) * (HEAD_DIM ** -0.5)
    s = s + jnp.swapaxes(c_q, 1, 2)[..., :, None] - jnp.swapaxes(c_k, 1, 2)[..., None, :]
    s = jnp.where(k_pos[None, :] <= q_pos[:, None], s, NEG_INF)
    p = jax.nn.softmax(s, axis=-1).astype(v.dtype)
    return jnp.einsum('bhqk,bkhd->bqhd', p, v)


def _fox_prompt(q, k, v, logf):
    bt, L = q.shape[0], q.shape[1]
    c = jnp.cumsum(logf, axis=1)
    pos = jnp.arange(L)
    nb = L // Q_BLOCK
    qb = jnp.swapaxes(q.reshape(bt, nb, Q_BLOCK, N_HEADS, HEAD_DIM), 0, 1)
    cb = jnp.swapaxes(c.reshape(bt, nb, Q_BLOCK, N_HEADS), 0, 1)
    pb = pos.reshape(nb, Q_BLOCK)
    out = lax.map(lambda a: _fox_block(a[0], k, v, a[1], c, a[2], pos), (qb, cb, pb))
    return jnp.swapaxes(out, 0, 1).reshape(bt, L, ATT_DIM)


def _fox_sample(q, k, v, logf, cache_k, cache_v, cache_logf):
    bt, L = q.shape[0], q.shape[1]
    past = cache_k.shape[1]
    k_all = jnp.concatenate([cache_k.astype(k.dtype), k], axis=1)
    v_all = jnp.concatenate([cache_v.astype(v.dtype), v], axis=1)
    c_all = jnp.cumsum(jnp.concatenate([cache_logf.astype(jnp.float32), logf], axis=1), axis=1)
    k_pos = jnp.arange(past + L)
    q_pos = past + jnp.arange(L)
    out = _fox_block(q, k_all, v_all, c_all[:, past:], c_all, q_pos, k_pos)
    return out.reshape(bt, L, ATT_DIM)


def _s5(u, lam_re, lam_im, log_dt, b_re, b_im, c_re, c_im, d_skip, h0):
    f32 = jnp.float32
    bt, L = u.shape[0], u.shape[1]
    ug = u.astype(f32).reshape(bt, L, SSM_GROUPS, SSM_GROUP)
    lr = jnp.minimum(lam_re.astype(f32), -1e-4)
    li = lam_im.astype(f32)
    dt = jnp.exp(log_dt.astype(f32))[:, None]
    ldr, ldi = lr * dt, li * dt
    mag = jnp.exp(ldr)
    ar, ai = mag * jnp.cos(ldi), mag * jnp.sin(ldi)
    den = lr * lr + li * li
    nr = ar - 1.0
    qr = (nr * lr + ai * li) / den
    qi = (ai * lr - nr * li) / den
    br, bi = b_re.astype(f32), b_im.astype(f32)
    bbar_re = qr[..., None] * br - qi[..., None] * bi
    bbar_im = qr[..., None] * bi + qi[..., None] * br
    bu_re = jnp.einsum('blgc,gpc->blgp', ug, bbar_re)
    bu_im = jnp.einsum('blgc,gpc->blgp', ug, bbar_im)
    a_re = jnp.broadcast_to(ar, bu_re.shape)
    a_im = jnp.broadcast_to(ai, bu_im.shape)

    def combine(e1, e2):
        a1r, a1i, b1r, b1i = e1
        a2r, a2i, b2r, b2i = e2
        return (a1r * a2r - a1i * a2i, a1r * a2i + a1i * a2r,
                a2r * b1r - a2i * b1i + b2r, a2r * b1i + a2i * b1r + b2i)

    _, _, hr, hi = lax.associative_scan(combine, (a_re, a_im, bu_re, bu_im), axis=1)
    if h0 is not None:
        h0r, h0i = h0[0].astype(f32)[:, None], h0[1].astype(f32)[:, None]
        t = jnp.arange(1, L + 1, dtype=f32)[:, None, None]
        pm = jnp.exp(ldr * t)
        pr, pi = pm * jnp.cos(ldi * t), pm * jnp.sin(ldi * t)
        hr = hr + pr * h0r - pi * h0i
        hi = hi + pr * h0i + pi * h0r
    y = jnp.einsum('blgp,gcp->blgc', hr, c_re.astype(f32)) - jnp.einsum('blgp,gcp->blgc', hi, c_im.astype(f32))
    y = y.reshape(bt, L, SSM_DIM) + d_skip.astype(f32) * u.astype(f32)
    return y.astype(u.dtype), hr[:, -1], hi[:, -1]


def _short_conv(h, gate_b, gate_c, w, prev):
    z = gate_c * h
    bt, L = z.shape[0], z.shape[1]
    if prev is None:
        pad = jnp.zeros((bt, CONV_WIDTH - 1, CONV_DIM), z.dtype)
    else:
        pad = prev.astype(z.dtype)
    zp = jnp.concatenate([pad, z], axis=1)
    y = sum(w[j] * zp[:, j:j + L] for j in range(CONV_WIDTH))
    return gate_b * y, zp[:, -(CONV_WIDTH - 1):]


def _layer(x, lp, past):
    bt, L = x.shape[0], x.shape[1]
    h = _rms(x, lp['ln1'])
    proj = h @ lp['w_in']
    q, k, v, fg, u, hc, gb, gc = jnp.split(proj, SPLITS, axis=-1)
    q = _rms(q.reshape(bt, L, N_HEADS, HEAD_DIM), lp['qn'])
    k = _rms(k.reshape(bt, L, N_HEADS, HEAD_DIM), lp['kn'])
    v = v.reshape(bt, L, N_HEADS, HEAD_DIM)
    logf = jax.nn.log_sigmoid((fg + lp['bf']).astype(jnp.float32))
    if past is None:
        att = _fox_prompt(q, k, v, logf)
        h0, conv_prev = None, None
    else:
        ck, cv, clf, s_re, s_im, s_conv = past
        att = _fox_sample(q, k, v, logf, ck, cv, clf)
        h0, conv_prev = (s_re, s_im), s_conv
    y_ssm, hr, hi = _s5(u, lp['lam_re'], lp['lam_im'], lp['log_dt'], lp['b_re'], lp['b_im'],
                        lp['c_re'], lp['c_im'], lp['d'], h0)
    g = jax.nn.gelu(y_ssm)
    ssm = g * jax.nn.sigmoid(g @ lp['w_glu'] + lp['b_glu'])
    conv, conv_state = _short_conv(hc, gb, gc, lp['conv_w'], conv_prev)
    bn = lp['bn']
    mix = jnp.concatenate([_rms(att, bn[:ATT_DIM]),
                           _rms(ssm, bn[ATT_DIM:ATT_DIM + SSM_DIM]),
                           _rms(conv, bn[ATT_DIM + SSM_DIM:])], axis=-1)
    x = x + mix @ lp['w_out']
    f = jax.nn.relu(_rms(x, lp['ln2']) @ lp['w_up'])
    x = x + (f * f) @ lp['w_down']
    return x, (k, v, logf, hr, hi, conv_state)


def setup_inputs(seed: int = 0) -> dict:
    key = jax.random.key(seed)
    ks = jax.random.split(key, 32)
    f32 = jnp.float32

    def nrm(k, shape, scale):
        return jax.random.normal(k, shape, f32) * scale

    out = {}
    out['x_prompt'] = nrm(ks[0], (BATCH, SEQ, D_MODEL), 1.0)
    out['x_sample'] = nrm(ks[1], (DEC_BATCH, DEC_SEQ, D_MODEL), 1.0)
    out['cache_k'] = nrm(ks[2], (DEPTH, DEC_BATCH, PAST_LEN, N_HEADS, HEAD_DIM), 1.0)
    out['cache_v'] = nrm(ks[3], (DEPTH, DEC_BATCH, PAST_LEN, N_HEADS, HEAD_DIM), 1.0)
    out['cache_logf'] = jax.nn.log_sigmoid(nrm(ks[4], (DEPTH, DEC_BATCH, PAST_LEN, N_HEADS), 1.0) + 3.0)
    out['state_ssm_re'] = nrm(ks[5], (DEPTH, DEC_BATCH, SSM_GROUPS, SSM_STATE), 0.1)
    out['state_ssm_im'] = nrm(ks[6], (DEPTH, DEC_BATCH, SSM_GROUPS, SSM_STATE), 0.1)
    out['state_conv'] = nrm(ks[7], (DEPTH, DEC_BATCH, CONV_WIDTH - 1, CONV_DIM), 1.0)
    out['ln1_w'] = 1.0 + nrm(ks[8], (DEPTH, D_MODEL), 0.02)
    out['w_in'] = nrm(ks[9], (DEPTH, D_MODEL, PROJ_DIM), D_MODEL ** -0.5)
    out['b_forget'] = jax.random.uniform(ks[10], (DEPTH, N_HEADS), f32, 1.0, 5.0)
    out['q_norm_w'] = 1.0 + nrm(ks[11], (DEPTH, HEAD_DIM), 0.02)
    out['k_norm_w'] = 1.0 + nrm(ks[12], (DEPTH, HEAD_DIM), 0.02)
    out['conv_w'] = nrm(ks[13], (DEPTH, CONV_WIDTH, CONV_DIM), CONV_WIDTH ** -0.5)
    out['ssm_lam_re'] = -0.5 + nrm(ks[14], (DEPTH, SSM_GROUPS, SSM_STATE), 0.01)
    out['ssm_lam_im'] = math.pi * jnp.arange(SSM_STATE, dtype=f32)[None, None, :] + nrm(ks[15], (DEPTH, SSM_GROUPS, SSM_STATE), 0.01)
    out['ssm_log_dt'] = jax.random.uniform(ks[16], (DEPTH, SSM_GROUPS), f32, math.log(DT_MIN), math.log(DT_MAX))
    out['ssm_b_re'] = nrm(ks[17], (DEPTH, SSM_GROUPS, SSM_STATE, SSM_GROUP), (2 * SSM_GROUP) ** -0.5)
    out['ssm_b_im'] = nrm(ks[18], (DEPTH, SSM_GROUPS, SSM_STATE, SSM_GROUP), (2 * SSM_GROUP) ** -0.5)
    out['ssm_c_re'] = nrm(ks[19], (DEPTH, SSM_GROUPS, SSM_GROUP, SSM_STATE), SSM_STATE ** -0.5)
    out['ssm_c_im'] = nrm(ks[20], (DEPTH, SSM_GROUPS, SSM_GROUP, SSM_STATE), SSM_STATE ** -0.5)
    out['ssm_d'] = nrm(ks[21], (DEPTH, SSM_DIM), 1.0)
    out['w_glu'] = nrm(ks[22], (DEPTH, SSM_DIM, SSM_DIM), SSM_DIM ** -0.5)
    out['b_glu'] = nrm(ks[23], (DEPTH, SSM_DIM), 0.02)
    out['branch_norm_w'] = 1.0 + nrm(ks[24], (DEPTH, MIX_DIM), 0.02)
    out['w_out'] = nrm(ks[25], (DEPTH, MIX_DIM, D_MODEL), MIX_DIM ** -0.5)
    out['ln2_w'] = 1.0 + nrm(ks[26], (DEPTH, D_MODEL), 0.02)
    out['w_up'] = nrm(ks[27], (DEPTH, D_MODEL, D_FF), D_MODEL ** -0.5)
    out['w_down'] = nrm(ks[28], (DEPTH, D_FF, D_MODEL), 0.5 * D_FF ** -0.5)
    return out


def reference(x_prompt, x_sample, cache_k, cache_v, cache_logf, state_ssm_re, state_ssm_im, state_conv,
              ln1_w, w_in, b_forget, q_norm_w, k_norm_w, conv_w, ssm_lam_re, ssm_lam_im, ssm_log_dt,
              ssm_b_re, ssm_b_im, ssm_c_re, ssm_c_im, ssm_d, w_glu, b_glu, branch_norm_w, w_out,
              ln2_w, w_up, w_down):
    xp, xs = x_prompt, x_sample
    st_p, st_s = [], []
    for l in range(DEPTH):
        lp = dict(ln1=ln1_w[l], w_in=w_in[l], bf=b_forget[l], qn=q_norm_w[l], kn=k_norm_w[l],
                  conv_w=conv_w[l], lam_re=ssm_lam_re[l], lam_im=ssm_lam_im[l], log_dt=ssm_log_dt[l],
                  b_re=ssm_b_re[l], b_im=ssm_b_im[l], c_re=ssm_c_re[l], c_im=ssm_c_im[l], d=ssm_d[l],
                  w_glu=w_glu[l], b_glu=b_glu[l], bn=branch_norm_w[l], w_out=w_out[l],
                  ln2=ln2_w[l], w_up=w_up[l], w_down=w_down[l])
        xp, sp = _layer(xp, lp, None)
        xs, ss = _layer(xs, lp, (cache_k[l], cache_v[l], cache_logf[l],
                                 state_ssm_re[l], state_ssm_im[l], state_conv[l]))
        st_p.append(sp)
        st_s.append(ss)

    def stk(lst, i):
        return jnp.stack([s[i] for s in lst])

    return (xp, xs,
            stk(st_p, 0), stk(st_p, 1), stk(st_p, 2), stk(st_p, 3), stk(st_p, 4), stk(st_p, 5),
            stk(st_s, 0), stk(st_s, 1), stk(st_s, 2), stk(st_s, 3), stk(st_s, 4), stk(st_s, 5))
```

```python
import functools

import jax
import jax.numpy as jnp
from jax import lax
from jax.experimental import pallas as pl
from jax.experimental.pallas import tpu as pltpu

F32 = jnp.float32
BF16 = jnp.bfloat16
HIGHEST = lax.Precision.HIGHEST

D_MODEL = 1024
N_HEADS = 8
HEAD_DIM = 64
ATT_DIM = N_HEADS * HEAD_DIM
SSM_DIM = 256
SSM_GROUP = 16
SSM_GROUPS = SSM_DIM // SSM_GROUP
SSM_STATE = 64
STATE_LANES = SSM_GROUPS * SSM_STATE
CONV_DIM = 256
CONV_WIDTH = 3
D_FF = 4 * D_MODEL
RMS_EPS = 1e-6
NEG_INF = -1e30

LANES = 128
SUBLANES = 8
HEAD_PAIRS = ATT_DIM // LANES
PROJ_PAD = 3 * ATT_DIM + SSM_DIM + 3 * CONV_DIM + LANES
VMEM_LIMIT = 56 * 1024 * 1024


def _params(*sem):
    return pltpu.CompilerParams(dimension_semantics=sem, vmem_limit_bytes=VMEM_LIMIT)


def _const_spec(shape):
    zeros = (0,) * len(shape)
    return pl.BlockSpec(shape, lambda *_: zeros)


def _inproj_body(x_ref, ln1_ref, w_ref, qnw_ref, knw_ref, bf_ref, mbd_ref, tri_ref, sel_ref, c0_ref,
                 q_ref, kb_ref, vb_ref, kf_ref, vf_ref, lf_ref, cq_ref, ct_ref, u_ref, z_ref, gb_ref,
                 carry_ref, *, cum_block):
    i = pl.program_id(1)
    x = x_ref[0]
    ms = jnp.mean(x * x, axis=-1, keepdims=True)
    h = (x * lax.rsqrt(ms + RMS_EPS) * ln1_ref[...]).astype(BF16)
    proj = jnp.dot(h, w_ref[...], preferred_element_type=F32)

    def head_norm(t, w):
        ss = jnp.dot((t * t).astype(BF16), mbd_ref[...], preferred_element_type=F32)
        return t * lax.rsqrt(ss * (1.0 / HEAD_DIM) + RMS_EPS) * w

    q = head_norm(proj[:, 0:ATT_DIM], qnw_ref[...]) * (HEAD_DIM ** -0.5)
    k = head_norm(proj[:, ATT_DIM:2 * ATT_DIM], knw_ref[...])
    v = proj[:, 2 * ATT_DIM:3 * ATT_DIM]
    q_ref[0] = q.astype(BF16)
    kb_ref[0] = k.astype(BF16)
    vb_ref[0] = v.astype(BF16)
    kf_ref[0] = k
    vf_ref[0] = v
    o = 3 * ATT_DIM
    u_ref[0] = proj[:, o:o + SSM_DIM]
    hc = proj[:, o + SSM_DIM:o + SSM_DIM + CONV_DIM]
    gb_ref[0] = proj[:, o + SSM_DIM + CONV_DIM:o + SSM_DIM + 2 * CONV_DIM]
    gc = proj[:, o + SSM_DIM + 2 * CONV_DIM:o + SSM_DIM + 3 * CONV_DIM]
    z_ref[0] = gc * hc

    lf = jax.nn.log_sigmoid(proj[:, PROJ_PAD - LANES:PROJ_PAD] + bf_ref[...])

    @pl.when(i == 0)
    def _():
        carry_ref[...] = c0_ref[0]

    carry = carry_ref[...]
    tb = lf.shape[0]
    pieces = []
    for s in range(tb // cum_block):
        blk = lf[s * cum_block:(s + 1) * cum_block]
        c = jnp.dot(tri_ref[...], blk, precision=HIGHEST, preferred_element_type=F32) + carry
        carry = c[cum_block - 1:cum_block]
        pieces.append(c)
    c_all = pieces[0] if len(pieces) == 1 else jnp.concatenate(pieces, axis=0)
    carry_ref[...] = carry
    lf_ref[0] = lf[:, 0:N_HEADS]
    cq_ref[0] = c_all[:, 0:N_HEADS]
    ct_ref[0] = lax.dot_general(sel_ref[...], c_all, (((1,), (1,)), ((), ())),
                                precision=HIGHEST, preferred_element_type=F32)


def _inproj(x, c0, lw, tb):
    bsz, seq, _ = x.shape
    cum_block = min(tb, LANES)
    tri = (lax.broadcasted_iota(jnp.int32, (cum_block, cum_block), 1)
           <= lax.broadcasted_iota(jnp.int32, (cum_block, cum_block), 0)).astype(F32)
    sel = (lax.broadcasted_iota(jnp.int32, (N_HEADS, LANES), 0)
           == lax.broadcasted_iota(jnp.int32, (N_HEADS, LANES), 1)).astype(F32)
    tok = lambda w: pl.BlockSpec((1, tb, w), lambda b, i: (b, i, 0))
    out_shapes = (
        jax.ShapeDtypeStruct((bsz, seq, ATT_DIM), BF16),
        jax.ShapeDtypeStruct((bsz, seq, ATT_DIM), BF16),
        jax.ShapeDtypeStruct((bsz, seq, ATT_DIM), BF16),
        jax.ShapeDtypeStruct((bsz, seq, ATT_DIM), F32),
        jax.ShapeDtypeStruct((bsz, seq, ATT_DIM), F32),
        jax.ShapeDtypeStruct((bsz, seq, N_HEADS), F32),
        jax.ShapeDtypeStruct((bsz, seq, N_HEADS), F32),
        jax.ShapeDtypeStruct((bsz, N_HEADS, seq), F32),
        jax.ShapeDtypeStruct((bsz, seq, SSM_DIM), F32),
        jax.ShapeDtypeStruct((bsz, seq, CONV_DIM), F32),
        jax.ShapeDtypeStruct((bsz, seq, CONV_DIM), F32),
    )
    out_specs = (tok(ATT_DIM), tok(ATT_DIM), tok(ATT_DIM), tok(ATT_DIM), tok(ATT_DIM),
                 tok(N_HEADS), tok(N_HEADS),
                 pl.BlockSpec((1, N_HEADS, tb), lambda b, i: (b, 0, i)),
                 tok(SSM_DIM), tok(CONV_DIM), tok(CONV_DIM))
    in_specs = [tok(D_MODEL), _const_spec((1, D_MODEL)), _const_spec((D_MODEL, PROJ_PAD)),
                _const_spec((1, ATT_DIM)), _const_spec((1, ATT_DIM)), _const_spec((1, LANES)),
                _const_spec((ATT_DIM, ATT_DIM)), _const_spec((cum_block, cum_block)),
                _const_spec((N_HEADS, LANES)),
                pl.BlockSpec((1, 1, LANES), lambda b, i: (b, 0, 0))]
    return pl.pallas_call(
        functools.partial(_inproj_body, cum_block=cum_block),
        grid=(bsz, seq // tb),
        in_specs=in_specs, out_specs=out_specs, out_shape=out_shapes,
        scratch_shapes=[pltpu.VMEM((1, LANES), F32)],
        compiler_params=_params("arbitrary", "arbitrary"),
        name="inproj",
    )(x, lw["ln1"], lw["w_in"], lw["qn"], lw["kn"], lw["bf"], lw["mbd"], tri, sel, c0)


def _softmax_step(carry, s, v, mask):
    m, l, acc = carry
    if mask is not None:
        s = jnp.where(mask, s, NEG_INF)
    m_new = jnp.maximum(m, jnp.max(s, axis=-1, keepdims=True))
    alpha = jnp.exp(m - m_new)
    p = jnp.exp(s - m_new)
    l = alpha * l + jnp.sum(p, axis=-1, keepdims=True)
    acc = alpha * acc + jnp.dot(p.astype(BF16), v, preferred_element_type=F32)
    return m_new, l, acc


def _head_select(q, cq8, head, a):
    tq = q.shape[0]
    lane = lax.broadcasted_iota(jnp.int32, (tq, LANES), 1)
    qa = jnp.where((lane >= HEAD_DIM * a) & (lane < HEAD_DIM * (a + 1)), q, jnp.zeros_like(q))
    lane8 = lax.broadcasted_iota(jnp.int32, (tq, N_HEADS), 1)
    cqa = jnp.sum(jnp.where(lane8 == head, cq8, 0.0), axis=-1, keepdims=True)
    return qa, cqa


def _attn_body(q_ref, k_ref, v_ref, cq_ref, ct_ref, o_ref, *, tq):
    hp = pl.program_id(1)
    qi = pl.program_id(2)
    q = q_ref[0]
    cq8 = cq_ref[0]
    row = lax.broadcasted_iota(jnp.int32, (tq, tq), 0)
    col = lax.broadcasted_iota(jnp.int32, (tq, tq), 1)
    lane = lax.broadcasted_iota(jnp.int32, (tq, LANES), 1)
    outs = []
    for a in range(2):
        head = 2 * hp + a
        qa, cqa = _head_select(q, cq8, head, a)

        def scores(j):
            start = pl.multiple_of(j * tq, tq)
            k = k_ref[0, pl.ds(start, tq), :]
            v = v_ref[0, pl.ds(start, tq), :]
            ck = ct_ref[0, pl.ds(head, 1), pl.ds(start, tq)]
            s = lax.dot_general(qa, k, (((1,), (1,)), ((), ())), preferred_element_type=F32)
            return s + cqa - ck, v

        def body(j, carry):
            s, v = scores(j)
            return _softmax_step(carry, s, v, None)

        init = (jnp.full((tq, 1), NEG_INF, F32), jnp.zeros((tq, 1), F32), jnp.zeros((tq, LANES), F32))
        carry = lax.fori_loop(0, qi, body, init)
        s, v = scores(qi)
        _, l, acc = _softmax_step(carry, s, v, col <= row)
        outs.append(acc / l)
    o_ref[0] = jnp.where(lane < HEAD_DIM, outs[0], outs[1])


def _attn_prompt(q, kb, vb, cq, ct, tq):
    bsz, seq, _ = q.shape
    return pl.pallas_call(
        functools.partial(_attn_body, tq=tq),
        grid=(bsz, HEAD_PAIRS, seq // tq),
        in_specs=[pl.BlockSpec((1, tq, LANES), lambda b, h, i: (b, i, h)),
                  pl.BlockSpec((1, seq, LANES), lambda b, h, i: (b, 0, h)),
                  pl.BlockSpec((1, seq, LANES), lambda b, h, i: (b, 0, h)),
                  pl.BlockSpec((1, tq, N_HEADS), lambda b, h, i: (b, i, 0)),
                  pl.BlockSpec((1, N_HEADS, seq), lambda b, h, i: (b, 0, 0))],
        out_specs=pl.BlockSpec((1, tq, LANES), lambda b, h, i: (b, i, h)),
        out_shape=jax.ShapeDtypeStruct((bsz, seq, ATT_DIM), F32),
        compiler_params=_params("arbitrary", "arbitrary", "arbitrary"),
        name="attn_prompt",
    )(q, kb, vb, cq, ct)


def _attn_sample_body(q_ref, ck_ref, cv_ref, kn_ref, vn_ref, cq_ref, ctc_ref, ctn_ref, o_ref,
                      m_s, l_s, acc_s, *, nkv):
    hp = pl.program_id(1)
    j = pl.program_id(2)
    tq = q_ref.shape[1]
    q = q_ref[0]
    cq8 = cq_ref[0]

    @pl.when(j == 0)
    def _():
        m_s[...] = jnp.full(m_s.shape, NEG_INF, F32)
        l_s[...] = jnp.zeros(l_s.shape, F32)
        acc_s[...] = jnp.zeros(acc_s.shape, F32)

    def update(k, v, ct_blk, mask):
        for a in range(2):
            head = 2 * hp + a
            qa, cqa = _head_select(q, cq8, head, a)
            ck = ct_blk[0, pl.ds(head, 1), :]
            s = lax.dot_general(qa, k, (((1,), (1,)), ((), ())), preferred_element_type=F32)
            s = s + cqa - ck
            carry = (m_s[a][:, 0:1], l_s[a][:, 0:1], acc_s[a])
            m, l, acc = _softmax_step(carry, s, v, mask)
            m_s[a] = jnp.broadcast_to(m, (tq, LANES))
            l_s[a] = jnp.broadcast_to(l, (tq, LANES))
            acc_s[a] = acc

    @pl.when(j < nkv)
    def _():
        update(ck_ref[0].astype(BF16), cv_ref[0].astype(BF16), ctc_ref, None)

    @pl.when(j == nkv)
    def _():
        row = lax.broadcasted_iota(jnp.int32, (tq, tq), 0)
        col = lax.broadcasted_iota(jnp.int32, (tq, tq), 1)
        update(kn_ref[0], vn_ref[0], ctn_ref, col <= row)
        lane = lax.broadcasted_iota(jnp.int32, (tq, LANES), 1)
        o_ref[0] = jnp.where(lane < HEAD_DIM, acc_s[0] / l_s[0], acc_s[1] / l_s[1])


def _attn_sample(q, kb, vb, cq, ct_new, cache_k, cache_v, ct_cache, tk):
    bsz, seq, _ = q.shape
    past = cache_k.shape[1]
    nkv = past // tk
    cache_blk = pl.BlockSpec((1, tk, LANES), lambda b, h, j: (b, jnp.minimum(j, nkv - 1), h))
    new_blk = pl.BlockSpec((1, seq, LANES), lambda b, h, j: (b, 0, h))
    return pl.pallas_call(
        functools.partial(_attn_sample_body, nkv=nkv),
        grid=(bsz, HEAD_PAIRS, nkv + 1),
        in_specs=[new_blk, cache_blk, cache_blk, new_blk, new_blk,
                  pl.BlockSpec((1, seq, N_HEADS), lambda b, h, j: (b, 0, 0)),
                  pl.BlockSpec((1, N_HEADS, tk), lambda b, h, j: (b, 0, jnp.minimum(j, nkv - 1))),
                  pl.BlockSpec((1, N_HEADS, seq), lambda b, h, j: (b, 0, 0))],
        out_specs=new_blk,
        out_shape=jax.ShapeDtypeStruct((bsz, seq, ATT_DIM), F32),
        scratch_shapes=[pltpu.VMEM((2, seq, LANES), F32)] * 3,
        compiler_params=_params("arbitrary", "arbitrary", "arbitrary"),
        name="attn_sample",
    )(q, cache_k, cache_v, kb, vb, cq, ct_cache, ct_new)


def _cache_cumsum_body(x_ref, o_ref):
    rows, past = x_ref.shape
    tri = (lax.broadcasted_iota(jnp.int32, (LANES, LANES), 0)
           <= lax.broadcasted_iota(jnp.int32, (LANES, LANES), 1)).astype(F32)
    carry = jnp.zeros((rows, 1), F32)
    for b in range(past // LANES):
        blk = x_ref[:, b * LANES:(b + 1) * LANES]
        c = jnp.dot(blk, tri, precision=HIGHEST, preferred_element_type=F32) + carry
        o_ref[:, b * LANES:(b + 1) * LANES] = c
        carry = c[:, LANES - 1:LANES]


def _cache_cumsum(x):
    return pl.pallas_call(
        _cache_cumsum_body,
        out_shape=jax.ShapeDtypeStruct(x.shape, F32),
        compiler_params=pltpu.CompilerParams(vmem_limit_bytes=VMEM_LIMIT),
        name="cache_cumsum",
    )(x)


def _s5_prep_body(lr_ref, li_ref, ldt_ref, br_ref, bi_ref, ar_ref, ai_ref, bbr_ref, bbi_ref):
    lr = jnp.minimum(lr_ref[...], -1e-4)
    li = li_ref[...]
    dt = jnp.exp(ldt_ref[...])
    ldr, ldi = lr * dt, li * dt
    mag = jnp.exp(ldr)
    ar, ai = mag * jnp.cos(ldi), mag * jnp.sin(ldi)
    den = lr * lr + li * li
    nr = ar - 1.0
    qr = (nr * lr + ai * li) / den
    qi = (ai * lr - nr * li) / den
    ar_ref[...] = ar
    ai_ref[...] = ai
    br, bi = br_ref[...], bi_ref[...]
    bbr_ref[...] = qr[:, None, :] * br - qi[:, None, :] * bi
    bbi_ref[...] = qr[:, None, :] * bi + qi[:, None, :] * br


def _s5_prep(lam_re, lam_im, log_dt, b_re, b_im):
    g, p = lam_re.shape
    gp = jax.ShapeDtypeStruct((g, p), F32)
    gcp = jax.ShapeDtypeStruct((g, SSM_GROUP, p), F32)
    return pl.pallas_call(
        _s5_prep_body, out_shape=(gp, gp, gcp, gcp), name="s5_prep",
    )(lam_re, lam_im, log_dt.reshape(g, 1), jnp.swapaxes(b_re, 1, 2), jnp.swapaxes(b_im, 1, 2))


def _block_diag(t):
    g, c, p = t.shape
    eye = jnp.eye(g, dtype=t.dtype)
    return (t[:, :, None, :] * eye[:, None, :, None]).reshape(g * c, g * p)


def _ssm_body(u_ref, z_ref, gb_ref, h0r_ref, h0i_ref, cprev_ref,
              bre_ref, bim_ref, cre_ref, cim_ref, ar_ref, ai_ref, d_ref, wglu_ref, bglu_ref, cw_ref,
              ssm_ref, conv_ref, hr_out, hi_out, cst_out,
              sre, sim, hr_s, hi_s, zprev_s, *, lb):
    j = pl.program_id(1)
    nb = SUBLANES

    @pl.when(j == 0)
    def _():
        hr_s[...] = h0r_ref[...]
        hi_s[...] = h0i_ref[...]
        zprev_s[...] = cprev_ref[...]

    u2 = u_ref[...].reshape(nb * lb, SSM_DIM)
    ub = u2.astype(BF16)
    bu_re = jnp.dot(ub, bre_ref[...], preferred_element_type=F32)
    bu_im = jnp.dot(ub, bim_ref[...], preferred_element_type=F32)
    tiles = STATE_LANES // LANES
    for c in range(tiles):
        sre[c] = bu_re[:, c * LANES:(c + 1) * LANES]
        sim[c] = bu_im[:, c * LANES:(c + 1) * LANES]
    ar = [jnp.broadcast_to(ar_ref[:, c * LANES:(c + 1) * LANES], (nb, LANES)) for c in range(tiles)]
    ai = [jnp.broadcast_to(ai_ref[:, c * LANES:(c + 1) * LANES], (nb, LANES)) for c in range(tiles)]

    def step(t, carry):
        hr, hi = carry
        rows = pl.ds(t, nb, stride=lb)
        nr, ni = [], []
        for c in range(tiles):
            r = ar[c] * hr[c] - ai[c] * hi[c] + sre[c, rows, :]
            i = ar[c] * hi[c] + ai[c] * hr[c] + sim[c, rows, :]
            sre[c, rows, :] = r
            sim[c, rows, :] = i
            nr.append(r)
            ni.append(i)
        return tuple(nr), tuple(ni)

    init = (tuple(hr_s[:, c * LANES:(c + 1) * LANES] for c in range(tiles)),
            tuple(hi_s[:, c * LANES:(c + 1) * LANES] for c in range(tiles)))
    hr, hi = lax.fori_loop(0, lb, step, init)
    for c in range(tiles):
        hr_s[:, c * LANES:(c + 1) * LANES] = hr[c]
        hi_s[:, c * LANES:(c + 1) * LANES] = hi[c]
        hr_out[:, c * LANES:(c + 1) * LANES] = hr[c]
        hi_out[:, c * LANES:(c + 1) * LANES] = hi[c]

    h_re = jnp.concatenate([sre[c] for c in range(tiles)], axis=-1).astype(BF16)
    h_im = jnp.concatenate([sim[c] for c in range(tiles)], axis=-1).astype(BF16)
    y = (jnp.dot(h_re, cre_ref[...], preferred_element_type=F32)
         - jnp.dot(h_im, cim_ref[...], preferred_element_type=F32))
    y = y + d_ref[...] * u2
    g = jax.nn.gelu(y)
    gate = jax.nn.sigmoid(jnp.dot(g.astype(BF16), wglu_ref[...], preferred_element_type=F32) + bglu_ref[...])
    ssm_ref[...] = (g * gate).reshape(nb, lb, SSM_DIM)

    tt = lax.broadcasted_iota(jnp.int32, (lb, CONV_DIM), 0)
    w0, w1, w2 = cw_ref[0:1, :], cw_ref[1:2, :], cw_ref[2:3, :]
    for b in range(nb):
        zb = z_ref[b]
        prev = zprev_s[b]
        z1 = jnp.where(tt == 0, prev[1:2], pltpu.roll(zb, 1, 0))
        z2 = jnp.where(tt == 0, prev[0:1], jnp.where(tt == 1, prev[1:2], pltpu.roll(zb, 2, 0)))
        conv_ref[b] = gb_ref[b] * (w0 * z2 + w1 * z1 + w2 * zb)
        last = zb[lb - (CONV_WIDTH - 1):lb]
        zprev_s[b] = last
        cst_out[b] = last


def _ssm_conv(u, z, gb, h0r, h0i, cprev, lw, lb):
    bsz, seq, _ = u.shape
    nb = SUBLANES
    tok = pl.BlockSpec((nb, lb, SSM_DIM), lambda g, j: (g, j, 0))
    st = pl.BlockSpec((nb, STATE_LANES), lambda g, j: (g, 0))
    cs = pl.BlockSpec((nb, CONV_WIDTH - 1, CONV_DIM), lambda g, j: (g, 0, 0))
    return pl.pallas_call(
        functools.partial(_ssm_body, lb=lb),
        grid=(bsz // nb, seq // lb),
        in_specs=[tok, tok, tok, st, st, cs,
                  _const_spec((SSM_DIM, STATE_LANES)), _const_spec((SSM_DIM, STATE_LANES)),
                  _const_spec((STATE_LANES, SSM_DIM)), _const_spec((STATE_LANES, SSM_DIM)),
                  _const_spec((1, STATE_LANES)), _const_spec((1, STATE_LANES)),
                  _const_spec((1, SSM_DIM)), _const_spec((SSM_DIM, SSM_DIM)), _const_spec((1, SSM_DIM)),
                  _const_spec((CONV_WIDTH, CONV_DIM))],
        out_specs=(tok, tok, st, st, cs),
        out_shape=(jax.ShapeDtypeStruct((bsz, seq, SSM_DIM), F32),
                   jax.ShapeDtypeStruct((bsz, seq, CONV_DIM), F32),
                   jax.ShapeDtypeStruct((bsz, STATE_LANES), F32),
                   jax.ShapeDtypeStruct((bsz, STATE_LANES), F32),
                   jax.ShapeDtypeStruct((bsz, CONV_WIDTH - 1, CONV_DIM), F32)),
        scratch_shapes=[pltpu.VMEM((STATE_LANES // LANES, nb * lb, LANES), F32),
                        pltpu.VMEM((STATE_LANES // LANES, nb * lb, LANES), F32),
                        pltpu.VMEM((nb, STATE_LANES), F32), pltpu.VMEM((nb, STATE_LANES), F32),
                        pltpu.VMEM((nb, CONV_WIDTH - 1, CONV_DIM), F32)],
        compiler_params=_params("arbitrary", "arbitrary"),
        name="ssm_conv",
    )(u, z, gb, h0r, h0i, cprev, lw["b_re_blk"], lw["b_im_blk"], lw["c_re_blk"], lw["c_im_blk"],
      lw["a_re"], lw["a_im"], lw["d"], lw["w_glu"], lw["b_glu"], lw["conv_w"])


def _rms_rows(t, w):
    return t * lax.rsqrt(jnp.mean(t * t, axis=-1, keepdims=True) + RMS_EPS) * w


def _mix_ffn_body(x_ref, att_ref, ssm_ref, conv_ref, bn_ref, wout_ref, ln2_ref, wup_ref, wdown_ref, o_ref,
                  *, ff_chunk):
    bn = bn_ref[...]
    mix = jnp.concatenate(
        [_rms_rows(att_ref[...], bn[:, 0:ATT_DIM]),
         _rms_rows(ssm_ref[...], bn[:, ATT_DIM:ATT_DIM + SSM_DIM]),
         _rms_rows(conv_ref[...], bn[:, ATT_DIM + SSM_DIM:])], axis=-1).astype(BF16)
    x1 = x_ref[...] + jnp.dot(mix, wout_ref[...], preferred_element_type=F32)
    h2 = _rms_rows(x1, ln2_ref[...]).astype(BF16)
    acc = x1
    for c in range(D_FF // ff_chunk):
        f = jnp.maximum(jnp.dot(h2, wup_ref[:, c * ff_chunk:(c + 1) * ff_chunk], preferred_element_type=F32), 0.0)
        acc = acc + jnp.dot((f * f).astype(BF16), wdown_ref[c * ff_chunk:(c + 1) * ff_chunk, :],
                            preferred_element_type=F32)
    o_ref[...] = acc


def _mix_ffn(x, att, ssm, conv, lw, tb):
    bsz, seq, _ = x.shape
    rows = bsz * seq
    tok = lambda w: pl.BlockSpec((tb, w), lambda i: (i, 0))
    once = lambda shape: pl.BlockSpec(shape, lambda i: (0, 0), pipeline_mode=pl.Buffered(1))
    out = pl.pallas_call(
        functools.partial(_mix_ffn_body, ff_chunk=1024),
        grid=(rows // tb,),
        in_specs=[tok(D_MODEL), tok(ATT_DIM), tok(SSM_DIM), tok(CONV_DIM),
                  once((1, D_MODEL)), once((D_MODEL, D_MODEL)), once((1, D_MODEL)),
                  once((D_MODEL, D_FF)), once((D_FF, D_MODEL))],
        out_specs=tok(D_MODEL),
        out_shape=jax.ShapeDtypeStruct((rows, D_MODEL), F32),
        compiler_params=_params("arbitrary"),
        name="mix_ffn",
    )(x.reshape(rows, D_MODEL), att.reshape(rows, ATT_DIM), ssm.reshape(rows, SSM_DIM),
      conv.reshape(rows, CONV_DIM), lw["bn"], lw["w_out"], lw["ln2"], lw["w_up"], lw["w_down"])
    return out.reshape(bsz, seq, D_MODEL)


def _layer_weights(l, ln1_w, w_in, b_forget, q_norm_w, k_norm_w, conv_w, ssm_lam_re, ssm_lam_im, ssm_log_dt,
                   ssm_b_re, ssm_b_im, ssm_c_re, ssm_c_im, ssm_d, w_glu, b_glu, branch_norm_w, w_out,
                   ln2_w, w_up, w_down):
    w = w_in[l]
    fg0 = 3 * ATT_DIM
    w_re = jnp.concatenate([w[:, :fg0], w[:, fg0 + N_HEADS:], w[:, fg0:fg0 + N_HEADS],
                            jnp.zeros((D_MODEL, LANES - N_HEADS), F32)], axis=1).astype(BF16)
    head_id = jnp.arange(ATT_DIM) // HEAD_DIM
    a_re, a_im, bb_re, bb_im = _s5_prep(ssm_lam_re[l], ssm_lam_im[l], ssm_log_dt[l], ssm_b_re[l], ssm_b_im[l])
    return dict(
        ln1=ln1_w[l].reshape(1, D_MODEL), w_in=w_re,
        qn=jnp.tile(q_norm_w[l], N_HEADS).reshape(1, ATT_DIM), kn=jnp.tile(k_norm_w[l], N_HEADS).reshape(1, ATT_DIM),
        bf=jnp.pad(b_forget[l], (0, LANES - N_HEADS)).reshape(1, LANES),
        mbd=(head_id[:, None] == head_id[None, :]).astype(BF16),
        a_re=a_re.reshape(1, STATE_LANES), a_im=a_im.reshape(1, STATE_LANES),
        b_re_blk=_block_diag(bb_re).astype(BF16), b_im_blk=_block_diag(bb_im).astype(BF16),
        c_re_blk=_block_diag(ssm_c_re[l]).T.astype(BF16), c_im_blk=_block_diag(ssm_c_im[l]).T.astype(BF16),
        d=ssm_d[l].reshape(1, SSM_DIM), w_glu=w_glu[l].astype(BF16), b_glu=b_glu[l].reshape(1, SSM_DIM),
        conv_w=conv_w[l], bn=branch_norm_w[l].reshape(1, D_MODEL), w_out=w_out[l].astype(BF16),
        ln2=ln2_w[l].reshape(1, D_MODEL), w_up=w_up[l].astype(BF16), w_down=w_down[l].astype(BF16))


def _run_layer(x, lw, past, tiles):
    bsz, seq, _ = x.shape
    if past is None:
        c0 = jnp.zeros((bsz, 1, LANES), F32)
        h0r = jnp.zeros((bsz, STATE_LANES), F32)
        h0i = h0r
        cprev = jnp.zeros((bsz, CONV_WIDTH - 1, CONV_DIM), F32)
    else:
        ck, cv, clf, s_re, s_im, s_conv = past
        plen = ck.shape[1]
        ct_cache = _cache_cumsum(jnp.swapaxes(clf, 1, 2).reshape(bsz * N_HEADS, plen)).reshape(bsz, N_HEADS, plen)
        c0 = jnp.pad(ct_cache[:, :, plen - 1], ((0, 0), (0, LANES - N_HEADS))).reshape(bsz, 1, LANES)
        h0r = s_re.reshape(bsz, STATE_LANES)
        h0i = s_im.reshape(bsz, STATE_LANES)
        cprev = s_conv
    q, kb, vb, kf, vf, lf, cq, ct, u, z, gb = _inproj(x, c0, lw, tiles["tok"])
    if past is None:
        att = _attn_prompt(q, kb, vb, cq, ct, tiles["attn"])
    else:
        att = _attn_sample(q, kb, vb, cq, ct, ck.reshape(bsz, plen, ATT_DIM), cv.reshape(bsz, plen, ATT_DIM),
                           ct_cache, tiles["cache"])
    ssm, conv, hr, hi, cst = _ssm_conv(u, z, gb, h0r, h0i, cprev, lw, tiles["scan"])
    y = _mix_ffn(x, att, ssm, conv, lw, tiles["ffn"])
    state = (kf.reshape(bsz, seq, N_HEADS, HEAD_DIM), vf.reshape(bsz, seq, N_HEADS, HEAD_DIM), lf,
             hr.reshape(bsz, SSM_GROUPS, SSM_STATE), hi.reshape(bsz, SSM_GROUPS, SSM_STATE), cst)
    return y, state


def _tiles(seq, past):
    tok = min(seq, 512)
    t = dict(tok=tok, attn=min(seq, 512), scan=min(seq, 128), ffn=tok)
    if past is not None:
        t["cache"] = min(past, 2048)
    return t


def _forward(x_prompt, x_sample, cache_k, cache_v, cache_logf, state_ssm_re, state_ssm_im, state_conv, *weights):
    depth = cache_k.shape[0]
    xp, xs = x_prompt, x_sample
    tiles_p = _tiles(xp.shape[1], None)
    tiles_s = _tiles(xs.shape[1], cache_k.shape[2])
    st_p, st_s = [], []
    for l in range(depth):
        lw = _layer_weights(l, *weights)
        xp, sp = _run_layer(xp, lw, None, tiles_p)
        xs, ss = _run_layer(xs, lw, (cache_k[l], cache_v[l], cache_logf[l],
                                     state_ssm_re[l], state_ssm_im[l], state_conv[l]), tiles_s)
        st_p.append(sp)
        st_s.append(ss)
    stk = lambda lst, i: jnp.stack([s[i] for s in lst])
    return (xp, xs) + tuple(stk(st_p, i) for i in range(6)) + tuple(stk(st_s, i) for i in range(6))


def kernel(x_prompt, x_sample, cache_k, cache_v, cache_logf, state_ssm_re, state_ssm_im, state_conv, ln1_w, w_in, b_forget, q_norm_w, k_norm_w, conv_w, ssm_lam_re, ssm_lam_im, ssm_log_dt, ssm_b_re, ssm_b_im, ssm_c_re, ssm_c_im, ssm_d, w_glu, b_glu, branch_norm_w, w_out, ln2_w, w_up, w_down):
    return _forward(x_prompt, x_sample, cache_k, cache_v, cache_logf, state_ssm_re, state_ssm_im, state_conv,
                    ln1_w, w_in, b_forget, q_norm_w, k_norm_w, conv_w, ssm_lam_re, ssm_lam_im, ssm_log_dt,
                    ssm_b_re, ssm_b_im, ssm_c_re, ssm_c_im, ssm_d, w_glu, b_glu, branch_norm_w, w_out,
                    ln2_w, w_up, w_down)
```

```python
import functools
import math

import jax
import jax.numpy as jnp
from jax import lax
from jax.experimental import pallas as pl
from jax.experimental.pallas import tpu as pltpu

F32 = jnp.float32
BF16 = jnp.bfloat16
HIGHEST = lax.Precision.HIGHEST

D_MODEL = 1024
N_HEADS = 8
HEAD_DIM = 64
ATT_DIM = N_HEADS * HEAD_DIM
SSM_DIM = 256
SSM_GROUP = 16
SSM_GROUPS = SSM_DIM // SSM_GROUP
SSM_STATE = 64
STATE_LANES = SSM_GROUPS * SSM_STATE
CONV_DIM = 256
CONV_WIDTH = 3
D_FF = 4 * D_MODEL
RMS_EPS = 1e-6
NEG_INF = -1e30
LOG2E = math.log2(math.e)

LANES = 128
SUBLANES = 8
HEAD_PAIRS = ATT_DIM // LANES
PAIR_EXT = 2 * LANES
AUG_STRIDE = 8
BIAS_PIECES = 3
PROJ_PAD = 3 * ATT_DIM + SSM_DIM + 3 * CONV_DIM + LANES
VMEM_LIMIT = 56 * 1024 * 1024


def _params(*sem):
    return pltpu.CompilerParams(dimension_semantics=sem, vmem_limit_bytes=VMEM_LIMIT)


def _const_spec(shape):
    zeros = (0,) * len(shape)
    return pl.BlockSpec(shape, lambda *_: zeros)


def _nt_dot(a, b, **kw):
    return lax.dot_general(a, b, (((1,), (1,)), ((), ())), preferred_element_type=F32, **kw)


def _bias_selectors():
    h = jnp.arange(N_HEADS)
    base = LANES * (h // 2) + AUG_STRIDE * (h % 2)
    sq = jnp.zeros((BIAS_PIECES * LANES, ATT_DIM), F32)
    sk = jnp.zeros((ATT_DIM, BIAS_PIECES * LANES), F32)
    for r in range(BIAS_PIECES):
        sq = sq.at[r * LANES + h, base + r].set(1.0)
        sk = sk.at[base + BIAS_PIECES + r, r * LANES + h].set(-1.0)
    return sq.astype(BF16), sk.astype(BF16)


def _inproj_body(*refs, cum_block, prompt, n_prev):
    (x_ref, ln1_ref, w_ref, qnw_ref, knw_ref, bf_ref, mbd_ref, tri_ref, sel_ref, c0_ref, sq_ref, sk_ref) = refs[:12]
    outs = refs[12 + n_prev:]
    carry_ref = outs[-1]
    i = pl.program_id(1)
    x = x_ref[...]
    ms = jnp.mean(x * x, axis=-1, keepdims=True)
    h = (x * lax.rsqrt(ms + RMS_EPS) * ln1_ref[...]).astype(BF16)
    proj = jnp.dot(h, w_ref[...], preferred_element_type=F32)
    tb = x.shape[0]

    def head_norm(t, w):
        ss = jnp.dot((t * t).astype(BF16), mbd_ref[...], preferred_element_type=F32)
        return t * lax.rsqrt(ss * (1.0 / HEAD_DIM) + RMS_EPS) * w

    q = head_norm(proj[:, 0:ATT_DIM], qnw_ref[...]) * (LOG2E * HEAD_DIM ** -0.5)
    k = head_norm(proj[:, ATT_DIM:2 * ATT_DIM], knw_ref[...])
    v = proj[:, 2 * ATT_DIM:3 * ATT_DIM]
    o = 3 * ATT_DIM
    u = proj[:, o:o + SSM_DIM]
    hc = proj[:, o + SSM_DIM:o + SSM_DIM + CONV_DIM]
    gb = proj[:, o + SSM_DIM + CONV_DIM:o + SSM_DIM + 2 * CONV_DIM]
    gc = proj[:, o + SSM_DIM + 2 * CONV_DIM:o + SSM_DIM + 3 * CONV_DIM]
    lf = jax.nn.log_sigmoid(proj[:, PROJ_PAD - LANES:PROJ_PAD] + bf_ref[...])

    @pl.when(i == 0)
    def _():
        carry_ref[...] = c0_ref[...]

    carry = carry_ref[...]
    pieces = []
    for s in range(tb // cum_block):
        blk = lf[s * cum_block:(s + 1) * cum_block]
        c = jnp.dot(tri_ref[...], blk, precision=HIGHEST, preferred_element_type=F32) + carry
        carry = c[cum_block - 1:cum_block]
        pieces.append(c)
    c_all = pieces[0] if len(pieces) == 1 else jnp.concatenate(pieces, axis=0)
    carry_ref[...] = carry
    c2 = c_all * LOG2E
    lf_t = _nt_dot(sel_ref[...], lf, precision=HIGHEST)

    if prompt:
        qx_ref, kx_ref, vb_ref, kt_ref, vt_ref, lft_ref, u_ref, z_ref, gb_ref = outs[:-1]
        p1 = c2.astype(BF16)
        r1 = c2 - p1.astype(F32)
        p2 = r1.astype(BF16)
        p3 = (r1 - p2.astype(F32)).astype(BF16)
        pcs = jnp.concatenate([p1, p2, p3], axis=-1)
        lane = lax.broadcasted_iota(jnp.int32, (tb, ATT_DIM), 1) % LANES
        q_one = (lane < 2 * AUG_STRIDE) & (lane % AUG_STRIDE >= BIAS_PIECES) & (lane % AUG_STRIDE < 2 * BIAS_PIECES)
        qa = jnp.dot(pcs, sq_ref[...], preferred_element_type=F32) + q_one.astype(F32)
        row = lax.broadcasted_iota(jnp.int32, (ATT_DIM, tb), 0) % LANES
        k_one = (row < 2 * AUG_STRIDE) & (row % AUG_STRIDE < BIAS_PIECES)
        ka_t = _nt_dot(sk_ref[...], pcs) + k_one.astype(F32)
        k_t = k.T
        qb, qab, ktb, kab = q.astype(BF16), qa.astype(BF16), k_t.astype(BF16), ka_t.astype(BF16)
        qx, kx = [], []
        for p in range(HEAD_PAIRS):
            sl = slice(p * LANES, (p + 1) * LANES)
            qx += [qb[:, sl], qab[:, sl]]
            kx += [ktb[sl], kab[sl]]
        qx_ref[...] = jnp.concatenate(qx, axis=-1)
        kx_ref[...] = jnp.concatenate(kx, axis=0)
        vb_ref[...] = v.astype(BF16)
        kt_ref[...] = k_t
        vt_ref[...] = v.T
    else:
        q_ref, kb_ref, vb_ref, kf_ref, vf_ref, lft_ref, cq_ref, ct_ref, u_ref, z_ref, gb_ref = outs[:-1]
        q_ref[...] = q.astype(BF16)
        kb_ref[...] = k.astype(BF16)
        vb_ref[...] = v.astype(BF16)
        kf_ref[...] = k.reshape(tb, N_HEADS, HEAD_DIM)
        vf_ref[...] = v.reshape(tb, N_HEADS, HEAD_DIM)
        cq_ref[...] = c2[:, 0:N_HEADS]
        ct_ref[...] = _nt_dot(sel_ref[...], c2, precision=HIGHEST)
    lft_ref[...] = lf_t
    u_ref[...] = u
    z_ref[...] = gc * hc
    gb_ref[...] = gb


def _inproj(x, c0, lw, tb, layer, depth, prev, prompt):
    bsz, seq, _ = x.shape
    cum_block = min(tb, LANES)
    tri = (lax.broadcasted_iota(jnp.int32, (cum_block, cum_block), 1)
           <= lax.broadcasted_iota(jnp.int32, (cum_block, cum_block), 0)).astype(F32)
    sel = (lax.broadcasted_iota(jnp.int32, (N_HEADS, LANES), 0)
           == lax.broadcasted_iota(jnp.int32, (N_HEADS, LANES), 1)).astype(F32)
    sq, sk = _bias_selectors()
    tok = lambda w: pl.BlockSpec((None, tb, w), lambda b, i: (b, i, 0))
    head_major = pl.BlockSpec((None, N_HEADS, tb), lambda b, i: (b, 0, i))
    tok_shape = lambda w, dt: jax.ShapeDtypeStruct((bsz, seq, w), dt)
    heads_shape = jax.ShapeDtypeStruct((bsz, N_HEADS, seq), F32)
    if prompt:
        state_shape = jax.ShapeDtypeStruct((depth, bsz, ATT_DIM, seq), F32)
        state_spec = pl.BlockSpec((None, None, ATT_DIM, tb), lambda b, i: (layer, b, 0, i))
        out_shapes = (tok_shape(HEAD_PAIRS * PAIR_EXT, BF16),
                      jax.ShapeDtypeStruct((bsz, HEAD_PAIRS * PAIR_EXT, seq), BF16),
                      tok_shape(ATT_DIM, BF16), state_shape, state_shape, heads_shape,
                      tok_shape(SSM_DIM, F32), tok_shape(CONV_DIM, F32), tok_shape(CONV_DIM, F32))
        out_specs = (tok(HEAD_PAIRS * PAIR_EXT),
                     pl.BlockSpec((None, HEAD_PAIRS * PAIR_EXT, tb), lambda b, i: (b, 0, i)),
                     tok(ATT_DIM), state_spec, state_spec, head_major,
                     tok(SSM_DIM), tok(CONV_DIM), tok(CONV_DIM))
        state_out = (3, 4)
    else:
        state_shape = jax.ShapeDtypeStruct((depth, bsz, seq, N_HEADS, HEAD_DIM), F32)
        state_spec = pl.BlockSpec((None, None, tb, N_HEADS, HEAD_DIM), lambda b, i: (layer, b, i, 0, 0))
        out_shapes = (tok_shape(ATT_DIM, BF16), tok_shape(ATT_DIM, BF16), tok_shape(ATT_DIM, BF16),
                      state_shape, state_shape, heads_shape,
                      tok_shape(N_HEADS, F32), heads_shape,
                      tok_shape(SSM_DIM, F32), tok_shape(CONV_DIM, F32), tok_shape(CONV_DIM, F32))
        out_specs = (tok(ATT_DIM), tok(ATT_DIM), tok(ATT_DIM), state_spec, state_spec, head_major,
                     tok(N_HEADS), head_major, tok(SSM_DIM), tok(CONV_DIM), tok(CONV_DIM))
        state_out = (3, 4)
    in_specs = [tok(D_MODEL), _const_spec((1, D_MODEL)), _const_spec((D_MODEL, PROJ_PAD)),
                _const_spec((1, ATT_DIM)), _const_spec((1, ATT_DIM)), _const_spec((1, LANES)),
                _const_spec((ATT_DIM, ATT_DIM)), _const_spec((cum_block, cum_block)),
                _const_spec((N_HEADS, LANES)),
                pl.BlockSpec((None, 1, LANES), lambda b, i: (b, 0, 0)),
                _const_spec(sq.shape), _const_spec(sk.shape)]
    args = [x, lw["ln1"], lw["w_in"], lw["qn"], lw["kn"], lw["bf"], lw["mbd"], tri, sel, c0, sq, sk]
    aliases = {}
    n_prev = 0
    if prev is not None:
        n_prev = len(prev)
        for n, buf in enumerate(prev):
            aliases[len(args)] = state_out[n]
            args.append(buf)
            in_specs.append(pl.BlockSpec(memory_space=pl.ANY))
    return pl.pallas_call(
        functools.partial(_inproj_body, cum_block=cum_block, prompt=prompt, n_prev=n_prev),
        grid=(bsz, seq // tb),
        in_specs=in_specs, out_specs=out_specs, out_shape=out_shapes,
        scratch_shapes=[pltpu.VMEM((1, LANES), F32)],
        input_output_aliases=aliases,
        compiler_params=_params("arbitrary", "arbitrary"),
        name="inproj_prompt" if prompt else "inproj_sample",
    )(*args)


def _attn_body(q_ref, kx_ref, v_ref, o_ref, m_s, acc_s, *, tq, pairs):
    qi = pl.program_id(2)
    lane_x = lax.broadcasted_iota(jnp.int32, (tq, PAIR_EXT), 1)
    lane = lax.broadcasted_iota(jnp.int32, (tq, LANES), 1)
    head_lanes, q_heads = [], []
    for a in range(2):
        head_lanes.append((lane >= HEAD_DIM * a) & (lane < HEAD_DIM * (a + 1)))
    for p in range(pairs):
        qx = q_ref[:, p * PAIR_EXT:(p + 1) * PAIR_EXT]
        for a in range(2):
            mine = (((lane_x >= HEAD_DIM * a) & (lane_x < HEAD_DIM * (a + 1)))
                    | ((lane_x >= LANES + AUG_STRIDE * a) & (lane_x < LANES + AUG_STRIDE * (a + 1))))
            q_heads.append(jnp.where(mine, qx, jnp.zeros_like(qx)))
    row = lax.broadcasted_iota(jnp.int32, (tq, tq), 0)
    col = lax.broadcasted_iota(jnp.int32, (tq, tq), 1)

    def block(j, first):
        start = pl.multiple_of(j * tq, tq)
        for p in range(pairs):
            kx = kx_ref[p * PAIR_EXT:(p + 1) * PAIR_EXT, pl.ds(start, tq)]
            v = v_ref[pl.ds(start, tq), p * LANES:(p + 1) * LANES]
            for a in range(2):
                h = 2 * p + a
                s = jnp.dot(q_heads[h], kx, preferred_element_type=F32)
                va = jnp.where(head_lanes[a], v, jnp.ones_like(v))
                if first:
                    s = jnp.where(col <= row, s, NEG_INF)
                m_new = jnp.broadcast_to(jnp.max(s, axis=-1, keepdims=True), (tq, LANES))
                if not first:
                    m_old = m_s[h]
                    m_new = jnp.maximum(m_old, m_new)
                prob = jnp.concatenate([jnp.exp2(s[:, c * LANES:(c + 1) * LANES] - m_new)
                                        for c in range(tq // LANES)], axis=-1).astype(BF16)
                pv = jnp.dot(prob, va, preferred_element_type=F32)
                acc_s[h] = pv if first else jnp.exp2(m_old - m_new) * acc_s[h] + pv
                m_s[h] = m_new

    block(qi, True)

    def body(j, carry):
        block(j, False)
        return carry

    lax.fori_loop(0, qi, body, 0)
    for p in range(pairs):
        outs = [acc_s[2 * p + a] / pltpu.roll(acc_s[2 * p + a], HEAD_DIM, 1) for a in range(2)]
        o_ref[:, p * LANES:(p + 1) * LANES] = jnp.where(head_lanes[0], outs[0], outs[1])


def _attn_prompt(qx, kx, vb, tq, pairs):
    bsz, seq, _ = qx.shape
    return pl.pallas_call(
        functools.partial(_attn_body, tq=tq, pairs=pairs),
        grid=(bsz, HEAD_PAIRS // pairs, seq // tq),
        in_specs=[pl.BlockSpec((None, tq, pairs * PAIR_EXT), lambda b, h, i: (b, i, h)),
                  pl.BlockSpec((None, pairs * PAIR_EXT, seq), lambda b, h, i: (b, h, 0)),
                  pl.BlockSpec((None, seq, pairs * LANES), lambda b, h, i: (b, 0, h))],
        out_specs=pl.BlockSpec((None, tq, pairs * LANES), lambda b, h, i: (b, i, h)),
        out_shape=jax.ShapeDtypeStruct((bsz, seq, ATT_DIM), F32),
        scratch_shapes=[pltpu.VMEM((2 * pairs, tq, LANES), F32), pltpu.VMEM((2 * pairs, tq, LANES), F32)],
        compiler_params=_params("arbitrary", "arbitrary", "arbitrary"),
        name="attn_prompt",
    )(qx, kx, vb)


def _attn_sample_body(q_ref, ckt_ref, cvt_ref, kn_ref, vn_ref, cq_ref, ctc_ref, ctn_ref, o_ref,
                      m_s, l_s, acc_s, *, nkv):
    hp = pl.program_id(1)
    j = pl.program_id(2)
    tq = q_ref.shape[0]
    q = q_ref[...]
    cq8 = cq_ref[...]
    lane = lax.broadcasted_iota(jnp.int32, (tq, LANES), 1)
    lane8 = lax.broadcasted_iota(jnp.int32, (tq, N_HEADS), 1)

    @pl.when(j == 0)
    def _():
        m_s[...] = jnp.full(m_s.shape, NEG_INF, F32)
        l_s[...] = jnp.zeros(l_s.shape, F32)
        acc_s[...] = jnp.zeros(acc_s.shape, F32)

    def update(scores_fn, pv_fn, ck_blk, scale, mask):
        for a in range(2):
            head = 2 * hp + a
            qa = jnp.where((lane >= HEAD_DIM * a) & (lane < HEAD_DIM * (a + 1)), q, jnp.zeros_like(q))
            cqa = jnp.sum(jnp.where(lane8 == head, cq8, 0.0), axis=-1, keepdims=True)
            ck = ck_blk[pl.ds(head, 1), :]
            if scale is not None:
                ck = ck * scale
            s = scores_fn(qa) + cqa - ck
            if mask is not None:
                s = jnp.where(mask, s, NEG_INF)
            m, l = m_s[a][:, 0:1], l_s[a][:, 0:1]
            m_new = jnp.maximum(m, jnp.max(s, axis=-1, keepdims=True))
            alpha = jnp.exp2(m - m_new)
            p = jnp.exp2(s - m_new)
            l = alpha * l + jnp.sum(p, axis=-1, keepdims=True)
            acc_s[a] = alpha * acc_s[a] + pv_fn(p.astype(BF16))
            m_s[a] = jnp.broadcast_to(m_new, (tq, LANES))
            l_s[a] = jnp.broadcast_to(l, (tq, LANES))

    @pl.when(j < nkv)
    def _():
        tk = ckt_ref.shape[-1]
        kt = ckt_ref[...].reshape(LANES, tk).astype(BF16)
        vt = cvt_ref[...].reshape(LANES, tk).astype(BF16)
        update(lambda qa: jnp.dot(qa, kt, preferred_element_type=F32),
               lambda p: _nt_dot(p, vt), ctc_ref, LOG2E, None)

    @pl.when(j == nkv)
    def _():
        row = lax.broadcasted_iota(jnp.int32, (tq, tq), 0)
        col = lax.broadcasted_iota(jnp.int32, (tq, tq), 1)
        kn, vn = kn_ref[...], vn_ref[...]
        update(lambda qa: _nt_dot(qa, kn),
               lambda p: jnp.dot(p, vn, preferred_element_type=F32), ctn_ref, None, col <= row)
        o_ref[...] = jnp.where(lane < HEAD_DIM, acc_s[0] / l_s[0], acc_s[1] / l_s[1])


def _attn_sample(q, kb, vb, cq, ct_new, cache_kt, cache_vt, ct_cache, layer, tk):
    bsz, seq, _ = q.shape
    past = cache_kt.shape[-1]
    nkv = past // tk
    cache_blk = pl.BlockSpec((None, None, 2, HEAD_DIM, tk),
                             lambda b, h, j: (layer, b, h, 0, jnp.minimum(j, nkv - 1)))
    new_blk = pl.BlockSpec((None, seq, LANES), lambda b, h, j: (b, 0, h))
    return pl.pallas_call(
        functools.partial(_attn_sample_body, nkv=nkv),
        grid=(bsz, HEAD_PAIRS, nkv + 1),
        in_specs=[new_blk, cache_blk, cache_blk, new_blk, new_blk,
                  pl.BlockSpec((None, seq, N_HEADS), lambda b, h, j: (b, 0, 0)),
                  pl.BlockSpec((None, N_HEADS, tk), lambda b, h, j: (b, 0, jnp.minimum(j, nkv - 1))),
                  pl.BlockSpec((None, N_HEADS, seq), lambda b, h, j: (b, 0, 0))],
        out_specs=new_blk,
        out_shape=jax.ShapeDtypeStruct((bsz, seq, ATT_DIM), F32),
        scratch_shapes=[pltpu.VMEM((2, seq, LANES), F32)] * 3,
        compiler_params=_params("arbitrary", "arbitrary", "arbitrary"),
        name="attn_sample",
    )(q, cache_kt, cache_vt, kb, vb, cq, ct_cache, ct_new)


def _cache_cumsum_body(x_ref, o_ref):
    rows, past = x_ref.shape
    tri = (lax.broadcasted_iota(jnp.int32, (LANES, LANES), 0)
           <= lax.broadcasted_iota(jnp.int32, (LANES, LANES), 1)).astype(F32)
    carry = jnp.zeros((rows, 1), F32)
    for b in range(past // LANES):
        blk = x_ref[:, b * LANES:(b + 1) * LANES]
        c = jnp.dot(blk, tri, precision=HIGHEST, preferred_element_type=F32) + carry
        o_ref[:, b * LANES:(b + 1) * LANES] = c
        carry = c[:, LANES - 1:LANES]


def _cache_cumsum(x):
    return pl.pallas_call(
        _cache_cumsum_body,
        out_shape=jax.ShapeDtypeStruct(x.shape, F32),
        compiler_params=pltpu.CompilerParams(vmem_limit_bytes=VMEM_LIMIT),
        name="cache_cumsum",
    )(x)


def _s5_prep_body(lr_ref, li_ref, ldt_ref, br_ref, bi_ref, ar_ref, ai_ref, bbr_ref, bbi_ref):
    lr = jnp.minimum(lr_ref[...], -1e-4)
    li = li_ref[...]
    dt = jnp.exp(ldt_ref[...])
    ldr, ldi = lr * dt, li * dt
    mag = jnp.exp(ldr)
    ar, ai = mag * jnp.cos(ldi), mag * jnp.sin(ldi)
    den = lr * lr + li * li
    nr = ar - 1.0
    qr = (nr * lr + ai * li) / den
    qi = (ai * lr - nr * li) / den
    ar_ref[...] = ar
    ai_ref[...] = ai
    br, bi = br_ref[...], bi_ref[...]
    bbr_ref[...] = qr[:, None, :] * br - qi[:, None, :] * bi
    bbi_ref[...] = qr[:, None, :] * bi + qi[:, None, :] * br


def _s5_prep(lam_re, lam_im, log_dt, b_re, b_im):
    g, p = lam_re.shape
    gp = jax.ShapeDtypeStruct((g, p), F32)
    gcp = jax.ShapeDtypeStruct((g, SSM_GROUP, p), F32)
    return pl.pallas_call(
        _s5_prep_body, out_shape=(gp, gp, gcp, gcp), name="s5_prep",
    )(lam_re, lam_im, log_dt.reshape(g, 1), jnp.swapaxes(b_re, 1, 2), jnp.swapaxes(b_im, 1, 2))


def _block_diag(t):
    g, c, p = t.shape
    eye = jnp.eye(g, dtype=t.dtype)
    return (t[:, :, None, :] * eye[:, None, :, None]).reshape(g * c, g * p)


def _ssm_body(u_ref, z_ref, gb_ref, h0r_ref, h0i_ref, cprev_ref,
              bre_ref, bim_ref, cre_ref, cim_ref, ar_ref, ai_ref, d_ref, wglu_ref, bglu_ref, cw_ref,
              ssm_ref, conv_ref, hr_out, hi_out, cst_out,
              sre, sim, hr_s, hi_s, zprev_s, *, lb):
    j = pl.program_id(1)
    nb = SUBLANES
    tiles = STATE_LANES // LANES

    @pl.when(j == 0)
    def _():
        hr_s[...] = h0r_ref[...]
        hi_s[...] = h0i_ref[...]
        zprev_s[...] = cprev_ref[...]

    u2 = pltpu.einshape("btc->tbc", u_ref[...]).reshape(lb * nb, SSM_DIM)
    ub = u2.astype(BF16)
    bu_re = jnp.dot(ub, bre_ref[...], preferred_element_type=F32)
    bu_im = jnp.dot(ub, bim_ref[...], preferred_element_type=F32)
    for c in range(tiles):
        sre[c] = bu_re[:, c * LANES:(c + 1) * LANES]
        sim[c] = bu_im[:, c * LANES:(c + 1) * LANES]
    ar = [jnp.broadcast_to(ar_ref[:, c * LANES:(c + 1) * LANES], (nb, LANES)) for c in range(tiles)]
    ai = [jnp.broadcast_to(ai_ref[:, c * LANES:(c + 1) * LANES], (nb, LANES)) for c in range(tiles)]

    def step(t, carry):
        hr, hi = carry
        rows = pl.ds(pl.multiple_of(t * nb, nb), nb)
        nr, ni = [], []
        for c in range(tiles):
            r = ar[c] * hr[c] - ai[c] * hi[c] + sre[c, rows, :]
            i = ar[c] * hi[c] + ai[c] * hr[c] + sim[c, rows, :]
            sre[c, rows, :] = r
            sim[c, rows, :] = i
            nr.append(r)
            ni.append(i)
        return tuple(nr), tuple(ni)

    init = (tuple(hr_s[:, c * LANES:(c + 1) * LANES] for c in range(tiles)),
            tuple(hi_s[:, c * LANES:(c + 1) * LANES] for c in range(tiles)))
    hr, hi = lax.fori_loop(0, lb, step, init)
    for c in range(tiles):
        hr_s[:, c * LANES:(c + 1) * LANES] = hr[c]
        hi_s[:, c * LANES:(c + 1) * LANES] = hi[c]
        hr_out[:, c * LANES:(c + 1) * LANES] = hr[c]
        hi_out[:, c * LANES:(c + 1) * LANES] = hi[c]

    h_re = jnp.concatenate([sre[c] for c in range(tiles)], axis=-1).astype(BF16)
    h_im = jnp.concatenate([sim[c] for c in range(tiles)], axis=-1).astype(BF16)
    y = (jnp.dot(h_re, cre_ref[...], preferred_element_type=F32)
         - jnp.dot(h_im, cim_ref[...], preferred_element_type=F32))
    y = y + d_ref[...] * u2
    g = jax.nn.gelu(y)
    gate = jax.nn.sigmoid(jnp.dot(g.astype(BF16), wglu_ref[...], preferred_element_type=F32) + bglu_ref[...])
    ssm_ref[...] = pltpu.einshape("tbc->btc", (g * gate).reshape(lb, nb, SSM_DIM))

    tt = lax.broadcasted_iota(jnp.int32, (lb, CONV_DIM), 0)
    w0, w1, w2 = cw_ref[0:1, :], cw_ref[1:2, :], cw_ref[2:3, :]
    for b in range(nb):
        zb = z_ref[b]
        prev = zprev_s[b]
        z1 = jnp.where(tt == 0, prev[1:2], pltpu.roll(zb, 1, 0))
        z2 = jnp.where(tt == 0, prev[0:1], jnp.where(tt == 1, prev[1:2], pltpu.roll(zb, 2, 0)))
        conv_ref[b] = gb_ref[b] * (w0 * z2 + w1 * z1 + w2 * zb)
        last = zb[lb - (CONV_WIDTH - 1):lb]
        zprev_s[b] = last
        cst_out[b] = last


def _ssm_conv(u, z, gb, h0r, h0i, cprev, lw, lb):
    bsz, seq, _ = u.shape
    nb = SUBLANES
    tok = pl.BlockSpec((nb, lb, SSM_DIM), lambda g, j: (g, j, 0))
    st = pl.BlockSpec((nb, STATE_LANES), lambda g, j: (g, 0))
    cs = pl.BlockSpec((nb, CONV_WIDTH - 1, CONV_DIM), lambda g, j: (g, 0, 0))
    return pl.pallas_call(
        functools.partial(_ssm_body, lb=lb),
        grid=(bsz // nb, seq // lb),
        in_specs=[tok, tok, tok, st, st, cs,
                  _const_spec((SSM_DIM, STATE_LANES)), _const_spec((SSM_DIM, STATE_LANES)),
                  _const_spec((STATE_LANES, SSM_DIM)), _const_spec((STATE_LANES, SSM_DIM)),
                  _const_spec((1, STATE_LANES)), _const_spec((1, STATE_LANES)),
                  _const_spec((1, SSM_DIM)), _const_spec((SSM_DIM, SSM_DIM)), _const_spec((1, SSM_DIM)),
                  _const_spec((CONV_WIDTH, CONV_DIM))],
        out_specs=(tok, tok, st, st, cs),
        out_shape=(jax.ShapeDtypeStruct((bsz, seq, SSM_DIM), F32),
                   jax.ShapeDtypeStruct((bsz, seq, CONV_DIM), F32),
                   jax.ShapeDtypeStruct((bsz, STATE_LANES), F32),
                   jax.ShapeDtypeStruct((bsz, STATE_LANES), F32),
                   jax.ShapeDtypeStruct((bsz, CONV_WIDTH - 1, CONV_DIM), F32)),
        scratch_shapes=[pltpu.VMEM((STATE_LANES // LANES, nb * lb, LANES), F32),
                        pltpu.VMEM((STATE_LANES // LANES, nb * lb, LANES), F32),
                        pltpu.VMEM((nb, STATE_LANES), F32), pltpu.VMEM((nb, STATE_LANES), F32),
                        pltpu.VMEM((nb, CONV_WIDTH - 1, CONV_DIM), F32)],
        compiler_params=_params("arbitrary", "arbitrary"),
        name="ssm_conv",
    )(u, z, gb, h0r, h0i, cprev, lw["b_re_blk"], lw["b_im_blk"], lw["c_re_blk"], lw["c_im_blk"],
      lw["a_re"], lw["a_im"], lw["d"], lw["w_glu"], lw["b_glu"], lw["conv_w"])


def _rms_rows(t, w):
    return t * lax.rsqrt(jnp.mean(t * t, axis=-1, keepdims=True) + RMS_EPS) * w


def _mix_ffn_body(x_ref, att_ref, ssm_ref, conv_ref, bn_ref, wout_ref, ln2_ref, wup_ref, wdown_ref, o_ref,
                  *, ff_chunk):
    bn = bn_ref[...]
    mix = jnp.concatenate(
        [_rms_rows(att_ref[...], bn[:, 0:ATT_DIM]),
         _rms_rows(ssm_ref[...], bn[:, ATT_DIM:ATT_DIM + SSM_DIM]),
         _rms_rows(conv_ref[...], bn[:, ATT_DIM + SSM_DIM:])], axis=-1).astype(BF16)
    x1 = x_ref[...] + jnp.dot(mix, wout_ref[...], preferred_element_type=F32)
    h2 = _rms_rows(x1, ln2_ref[...]).astype(BF16)
    acc = x1
    for c in range(D_FF // ff_chunk):
        f = jnp.maximum(jnp.dot(h2, wup_ref[:, c * ff_chunk:(c + 1) * ff_chunk], preferred_element_type=F32), 0.0)
        acc = acc + jnp.dot((f * f).astype(BF16), wdown_ref[c * ff_chunk:(c + 1) * ff_chunk, :],
                            preferred_element_type=F32)
    o_ref[...] = acc


def _mix_ffn(x, att, ssm, conv, lw, tb):
    bsz, seq, _ = x.shape
    rows = bsz * seq
    tok = lambda w: pl.BlockSpec((tb, w), lambda i: (i, 0))
    once = lambda shape: pl.BlockSpec(shape, lambda i: (0, 0), pipeline_mode=pl.Buffered(1))
    out = pl.pallas_call(
        functools.partial(_mix_ffn_body, ff_chunk=1024),
        grid=(rows // tb,),
        in_specs=[tok(D_MODEL), tok(ATT_DIM), tok(SSM_DIM), tok(CONV_DIM),
                  once((1, D_MODEL)), once((D_MODEL, D_MODEL)), once((1, D_MODEL)),
                  once((D_MODEL, D_FF)), once((D_FF, D_MODEL))],
        out_specs=tok(D_MODEL),
        out_shape=jax.ShapeDtypeStruct((rows, D_MODEL), F32),
        compiler_params=_params("arbitrary"),
        name="mix_ffn",
    )(x.reshape(rows, D_MODEL), att.reshape(rows, ATT_DIM), ssm.reshape(rows, SSM_DIM),
      conv.reshape(rows, CONV_DIM), lw["bn"], lw["w_out"], lw["ln2"], lw["w_up"], lw["w_down"])
    return out.reshape(bsz, seq, D_MODEL)


def _layer_weights(l, ln1_w, w_in, b_forget, q_norm_w, k_norm_w, conv_w, ssm_lam_re, ssm_lam_im, ssm_log_dt,
                   ssm_b_re, ssm_b_im, ssm_c_re, ssm_c_im, ssm_d, w_glu, b_glu, branch_norm_w, w_out,
                   ln2_w, w_up, w_down):
    w = w_in[l]
    fg0 = 3 * ATT_DIM
    w_re = jnp.concatenate([w[:, :fg0], w[:, fg0 + N_HEADS:], w[:, fg0:fg0 + N_HEADS],
                            jnp.zeros((D_MODEL, LANES - N_HEADS), F32)], axis=1).astype(BF16)
    head_id = jnp.arange(ATT_DIM) // HEAD_DIM
    a_re, a_im, bb_re, bb_im = _s5_prep(ssm_lam_re[l], ssm_lam_im[l], ssm_log_dt[l], ssm_b_re[l], ssm_b_im[l])
    return dict(
        ln1=ln1_w[l].reshape(1, D_MODEL), w_in=w_re,
        qn=jnp.tile(q_norm_w[l], N_HEADS).reshape(1, ATT_DIM), kn=jnp.tile(k_norm_w[l], N_HEADS).reshape(1, ATT_DIM),
        bf=jnp.pad(b_forget[l], (0, LANES - N_HEADS)).reshape(1, LANES),
        mbd=(head_id[:, None] == head_id[None, :]).astype(BF16),
        a_re=a_re.reshape(1, STATE_LANES), a_im=a_im.reshape(1, STATE_LANES),
        b_re_blk=_block_diag(bb_re).astype(BF16), b_im_blk=_block_diag(bb_im).astype(BF16),
        c_re_blk=_block_diag(ssm_c_re[l]).T.astype(BF16), c_im_blk=_block_diag(ssm_c_im[l]).T.astype(BF16),
        d=ssm_d[l].reshape(1, SSM_DIM), w_glu=w_glu[l].astype(BF16), b_glu=b_glu[l].reshape(1, SSM_DIM),
        conv_w=conv_w[l], bn=branch_norm_w[l].reshape(1, D_MODEL), w_out=w_out[l].astype(BF16),
        ln2=ln2_w[l].reshape(1, D_MODEL), w_up=w_up[l].astype(BF16), w_down=w_down[l].astype(BF16))


def _run_layer(x, lw, layer, depth, prev, past, tiles):
    bsz, seq, _ = x.shape
    if past is None:
        c0 = jnp.zeros((bsz, 1, LANES), F32)
        h0r = jnp.zeros((bsz, STATE_LANES), F32)
        h0i = h0r
        cprev = jnp.zeros((bsz, CONV_WIDTH - 1, CONV_DIM), F32)
        qx, kx, vb, k_all, v_all, lft, u, z, gb = _inproj(x, c0, lw, tiles["tok"], layer, depth, prev, True)
        att = _attn_prompt(qx, kx, vb, tiles["attn"], tiles["attn_pairs"])
    else:
        cache_kt, cache_vt, clf_t, s_re, s_im, s_conv = past
        plen = cache_kt.shape[-1]
        ct_cache = _cache_cumsum(clf_t.reshape(bsz * N_HEADS, plen)).reshape(bsz, N_HEADS, plen)
        c0 = jnp.pad(ct_cache[:, :, plen - 1], ((0, 0), (0, LANES - N_HEADS))).reshape(bsz, 1, LANES)
        h0r = s_re.reshape(bsz, STATE_LANES)
        h0i = s_im.reshape(bsz, STATE_LANES)
        cprev = s_conv
        q, kb, vb, k_all, v_all, lft, cq, ct, u, z, gb = _inproj(x, c0, lw, tiles["tok"], layer, depth, prev, False)
        att = _attn_sample(q, kb, vb, cq, ct, cache_kt, cache_vt, ct_cache, layer, tiles["cache"])
    ssm, conv, hr, hi, cst = _ssm_conv(u, z, gb, h0r, h0i, cprev, lw, tiles["scan"])
    y = _mix_ffn(x, att, ssm, conv, lw, tiles["ffn"])
    small = (lft, hr.reshape(bsz, SSM_GROUPS, SSM_STATE), hi.reshape(bsz, SSM_GROUPS, SSM_STATE), cst)
    return y, (k_all, v_all), small


def _tiles(seq, past):
    tok = min(seq, 512)
    t = dict(tok=tok, attn=min(seq, 512), attn_pairs=2, scan=min(seq, 128), ffn=tok)
    if past is not None:
        t["cache"] = min(past, 2048)
    return t


def _forward(x_prompt, x_sample, cache_k, cache_v, cache_logf, state_ssm_re, state_ssm_im, state_conv, *weights):
    depth = cache_k.shape[0]
    xp, xs = x_prompt, x_sample
    bp, lp = xp.shape[0], xp.shape[1]
    tiles_p = _tiles(lp, None)
    tiles_s = _tiles(xs.shape[1], cache_k.shape[2])
    cache_kt = jnp.transpose(cache_k, (0, 1, 3, 4, 2))
    cache_vt = jnp.transpose(cache_v, (0, 1, 3, 4, 2))
    cache_lft = jnp.swapaxes(cache_logf, 2, 3)
    kv_p, kv_s, small_p, small_s = None, None, [], []
    for l in range(depth):
        lw = _layer_weights(l, *weights)
        xp, kv_p, sp = _run_layer(xp, lw, l, depth, kv_p, None, tiles_p)
        xs, kv_s, ss = _run_layer(xs, lw, l, depth, kv_s,
                                  (cache_kt, cache_vt, cache_lft[l], state_ssm_re[l], state_ssm_im[l], state_conv[l]),
                                  tiles_s)
        small_p.append(sp)
        small_s.append(ss)
    stk = lambda lst, i: jnp.stack([s[i] for s in lst])
    from_t = lambda t: jnp.transpose(t.reshape(depth, bp, N_HEADS, HEAD_DIM, lp), (0, 1, 4, 2, 3))
    return (xp, xs,
            from_t(kv_p[0]), from_t(kv_p[1]), jnp.swapaxes(stk(small_p, 0), 2, 3),
            stk(small_p, 1), stk(small_p, 2), stk(small_p, 3),
            kv_s[0], kv_s[1], jnp.swapaxes(stk(small_s, 0), 2, 3),
            stk(small_s, 1), stk(small_s, 2), stk(small_s, 3))


def kernel(x_prompt, x_sample, cache_k, cache_v, cache_logf, state_ssm_re, state_ssm_im, state_conv, ln1_w, w_in, b_forget, q_norm_w, k_norm_w, conv_w, ssm_lam_re, ssm_lam_im, ssm_log_dt, ssm_b_re, ssm_b_im, ssm_c_re, ssm_c_im, ssm_d, w_glu, b_glu, branch_norm_w, w_out, ln2_w, w_up, w_down):
    return _forward(x_prompt, x_sample, cache_k, cache_v, cache_logf, state_ssm_re, state_ssm_im, state_conv,
                    ln1_w, w_in, b_forget, q_norm_w, k_norm_w, conv_w, ssm_lam_re, ssm_lam_im, ssm_log_dt,
                    ssm_b_re, ssm_b_im, ssm_c_re, ssm_c_im, ssm_d, w_glu, b_glu, branch_norm_w, w_out,
                    ln2_w, w_up, w_down)
```

```python
import functools
import math

import jax
import jax.numpy as jnp
from jax import lax
from jax.experimental import pallas as pl
from jax.experimental.pallas import tpu as pltpu

F32 = jnp.float32
BF16 = jnp.bfloat16
HIGHEST = lax.Precision.HIGHEST

D_MODEL = 1024
N_HEADS = 8
HEAD_DIM = 64
ATT_DIM = N_HEADS * HEAD_DIM
SSM_DIM = 256
SSM_GROUP = 16
SSM_GROUPS = SSM_DIM // SSM_GROUP
SSM_STATE = 64
STATE_LANES = SSM_GROUPS * SSM_STATE
CONV_DIM = 256
CONV_WIDTH = 3
D_FF = 4 * D_MODEL
RMS_EPS = 1e-6
NEG_INF = -1e30
LOG2E = math.log2(math.e)

LANES = 128
SUBLANES = 8
HEAD_PAIRS = ATT_DIM // LANES
PAIR_EXT = 2 * LANES
AUG_STRIDE = 8
BIAS_PIECES = 3
PROJ_PAD = 3 * ATT_DIM + SSM_DIM + 3 * CONV_DIM + LANES
VMEM_LIMIT = 56 * 1024 * 1024


def _params(*sem):
    return pltpu.CompilerParams(dimension_semantics=sem, vmem_limit_bytes=VMEM_LIMIT)


def _const_spec(shape):
    zeros = (0,) * len(shape)
    return pl.BlockSpec(shape, lambda *_: zeros)


def _nt_dot(a, b, **kw):
    return lax.dot_general(a, b, (((1,), (1,)), ((), ())), preferred_element_type=F32, **kw)


def _bias_selectors():
    h = jnp.arange(N_HEADS)
    base = LANES * (h // 2) + AUG_STRIDE * (h % 2)
    sq = jnp.zeros((LANES, ATT_DIM), F32)
    sk = jnp.zeros((ATT_DIM, LANES), F32)
    for r in range(BIAS_PIECES):
        sq = sq.at[N_HEADS * r + h, base + r].set(1.0)
        sk = sk.at[base + BIAS_PIECES + r, N_HEADS * r + h].set(-1.0)
    return sq.astype(BF16), sk.astype(BF16)


def _split3(t):
    p1 = t.astype(BF16)
    r1 = t - p1.astype(F32)
    p2 = r1.astype(BF16)
    p3 = (r1 - p2.astype(F32)).astype(BF16)
    return p1, p2, p3


def _inproj_body(*refs, cum_block, prompt, n_prev):
    (x_ref, ln1_ref, w_ref, qnw_ref, knw_ref, bf_ref, mbd_ref, tri_ref, sel_ref, c0_ref, sq_ref, sk_ref) = refs[:12]
    outs = refs[12 + n_prev:]
    carry_ref = outs[-1]
    i = pl.program_id(1)
    x = x_ref[...]
    ms = jnp.mean(x * x, axis=-1, keepdims=True)
    h = (x * lax.rsqrt(ms + RMS_EPS) * ln1_ref[...]).astype(BF16)
    proj = jnp.dot(h, w_ref[...], preferred_element_type=F32)
    tb = x.shape[0]

    def head_norm(t, w):
        ss = jnp.dot((t * t).astype(BF16), mbd_ref[...], preferred_element_type=F32)
        return t * lax.rsqrt(ss * (1.0 / HEAD_DIM) + RMS_EPS) * w

    q = head_norm(proj[:, 0:ATT_DIM], qnw_ref[...]) * (LOG2E * HEAD_DIM ** -0.5)
    k = head_norm(proj[:, ATT_DIM:2 * ATT_DIM], knw_ref[...])
    v = proj[:, 2 * ATT_DIM:3 * ATT_DIM]
    o = 3 * ATT_DIM
    u = proj[:, o:o + SSM_DIM]
    hc = proj[:, o + SSM_DIM:o + SSM_DIM + CONV_DIM]
    gb = proj[:, o + SSM_DIM + CONV_DIM:o + SSM_DIM + 2 * CONV_DIM]
    gc = proj[:, o + SSM_DIM + 2 * CONV_DIM:o + SSM_DIM + 3 * CONV_DIM]
    lf = jax.nn.log_sigmoid(proj[:, PROJ_PAD - LANES:PROJ_PAD] + bf_ref[...])

    @pl.when(i == 0)
    def _():
        carry_ref[...] = c0_ref[...]

    carry = carry_ref[...]
    pieces = []
    for s in range(tb // cum_block):
        blk = jnp.concatenate(_split3(lf[s * cum_block:(s + 1) * cum_block]), axis=-1)
        r = jnp.dot(tri_ref[...], blk, preferred_element_type=F32)
        c = r[:, 0:LANES] + r[:, LANES:2 * LANES] + r[:, 2 * LANES:3 * LANES] + carry
        carry = c[cum_block - 1:cum_block]
        pieces.append(c)
    c_all = pieces[0] if len(pieces) == 1 else jnp.concatenate(pieces, axis=0)
    carry_ref[...] = carry
    c2 = c_all * LOG2E
    lf_t = _nt_dot(sel_ref[...], lf, precision=HIGHEST)

    if prompt:
        qx_ref, kx_ref, vb_ref, kt_ref, vt_ref, lft_ref, u_ref, z_ref, gb_ref = outs[:-1]
        head_lane = lax.broadcasted_iota(jnp.int32, (tb, LANES), 1) < N_HEADS
        p1, p2, p3 = (p.astype(F32) for p in _split3(jnp.where(head_lane, c2, 0.0)))
        pcs = (p1 + pltpu.roll(p2, N_HEADS, 1) + pltpu.roll(p3, 2 * N_HEADS, 1)).astype(BF16)
        lane = lax.broadcasted_iota(jnp.int32, (tb, ATT_DIM), 1) % LANES
        q_one = (lane < 2 * AUG_STRIDE) & (lane % AUG_STRIDE >= BIAS_PIECES) & (lane % AUG_STRIDE < 2 * BIAS_PIECES)
        qa = jnp.dot(pcs, sq_ref[...], preferred_element_type=F32) + q_one.astype(F32)
        row = lax.broadcasted_iota(jnp.int32, (ATT_DIM, tb), 0) % LANES
        k_one = (row < 2 * AUG_STRIDE) & (row % AUG_STRIDE < BIAS_PIECES)
        ka_t = _nt_dot(sk_ref[...], pcs) + k_one.astype(F32)
        k_t = k.T
        qb, qab, ktb, kab = q.astype(BF16), qa.astype(BF16), k_t.astype(BF16), ka_t.astype(BF16)
        qx, kx = [], []
        for p in range(HEAD_PAIRS):
            sl = slice(p * LANES, (p + 1) * LANES)
            qx += [qb[:, sl], qab[:, sl]]
            kx += [ktb[sl], kab[sl]]
        qx_ref[...] = jnp.concatenate(qx, axis=-1)
        kx_ref[...] = jnp.concatenate(kx, axis=0)
        vb_ref[...] = v.astype(BF16)
        kt_ref[...] = k_t
        vt_ref[...] = v.T
    else:
        q_ref, kb_ref, vb_ref, kf_ref, vf_ref, lft_ref, cq_ref, ct_ref, u_ref, z_ref, gb_ref = outs[:-1]
        q_ref[...] = q.astype(BF16)
        kb_ref[...] = k.astype(BF16)
        vb_ref[...] = v.astype(BF16)
        kf_ref[...] = k.reshape(tb, N_HEADS, HEAD_DIM)
        vf_ref[...] = v.reshape(tb, N_HEADS, HEAD_DIM)
        cq_ref[...] = c2[:, 0:N_HEADS]
        ct_ref[...] = _nt_dot(sel_ref[...], c2, precision=HIGHEST)
    lft_ref[...] = lf_t
    u_ref[...] = u
    z_ref[...] = gc * hc
    gb_ref[...] = gb


def _inproj(x, c0, lw, tb, layer, depth, prev, prompt):
    bsz, seq, _ = x.shape
    cum_block = min(tb, LANES)
    tri = (lax.broadcasted_iota(jnp.int32, (cum_block, cum_block), 1)
           <= lax.broadcasted_iota(jnp.int32, (cum_block, cum_block), 0)).astype(BF16)
    sel =(lax.broadcasted_iota(jnp.int32, (N_HEADS, LANES), 0)
           == lax.broadcasted_iota(jnp.int32, (N_HEADS, LANES), 1)).astype(F32)
    sq, sk = _bias_selectors()
    tok = lambda w: pl.BlockSpec((None, tb, w), lambda b, i: (b, i, 0))
    head_major = pl.BlockSpec((None, N_HEADS, tb), lambda b, i: (b, 0, i))
    tok_shape = lambda w, dt: jax.ShapeDtypeStruct((bsz, seq, w), dt)
    heads_shape = jax.ShapeDtypeStruct((bsz, N_HEADS, seq), F32)
    if prompt:
        state_shape = jax.ShapeDtypeStruct((depth, bsz, ATT_DIM, seq), F32)
        state_spec = pl.BlockSpec((None, None, ATT_DIM, tb), lambda b, i: (layer, b, 0, i))
        out_shapes = (tok_shape(HEAD_PAIRS * PAIR_EXT, BF16),
                      jax.ShapeDtypeStruct((bsz, HEAD_PAIRS * PAIR_EXT, seq), BF16),
                      tok_shape(ATT_DIM, BF16), state_shape, state_shape, heads_shape,
                      tok_shape(SSM_DIM, F32), tok_shape(CONV_DIM, F32), tok_shape(CONV_DIM, F32))
        out_specs = (tok(HEAD_PAIRS * PAIR_EXT),
                     pl.BlockSpec((None, HEAD_PAIRS * PAIR_EXT, tb), lambda b, i: (b, 0, i)),
                     tok(ATT_DIM), state_spec, state_spec, head_major,
                     tok(SSM_DIM), tok(CONV_DIM), tok(CONV_DIM))
        state_out = (3, 4)
    else:
        state_shape = jax.ShapeDtypeStruct((depth, bsz, seq, N_HEADS, HEAD_DIM), F32)
        state_spec = pl.BlockSpec((None, None, tb, N_HEADS, HEAD_DIM), lambda b, i: (layer, b, i, 0, 0))
        out_shapes = (tok_shape(ATT_DIM, BF16), tok_shape(ATT_DIM, BF16), tok_shape(ATT_DIM, BF16),
                      state_shape, state_shape, heads_shape,
                      tok_shape(N_HEADS, F32), heads_shape,
                      tok_shape(SSM_DIM, F32), tok_shape(CONV_DIM, F32), tok_shape(CONV_DIM, F32))
        out_specs = (tok(ATT_DIM), tok(ATT_DIM), tok(ATT_DIM), state_spec, state_spec, head_major,
                     tok(N_HEADS), head_major, tok(SSM_DIM), tok(CONV_DIM), tok(CONV_DIM))
        state_out = (3, 4)
    in_specs = [tok(D_MODEL), _const_spec((1, D_MODEL)), _const_spec((D_MODEL, PROJ_PAD)),
                _const_spec((1, ATT_DIM)), _const_spec((1, ATT_DIM)), _const_spec((1, LANES)),
                _const_spec((ATT_DIM, ATT_DIM)), _const_spec((cum_block, cum_block)),
                _const_spec((N_HEADS, LANES)),
                pl.BlockSpec((None, 1, LANES), lambda b, i: (b, 0, 0)),
                _const_spec(sq.shape), _const_spec(sk.shape)]
    args = [x, lw["ln1"], lw["w_in"], lw["qn"], lw["kn"], lw["bf"], lw["mbd"], tri, sel, c0, sq, sk]
    aliases = {}
    n_prev = 0
    if prev is not None:
        n_prev = len(prev)
        for n, buf in enumerate(prev):
            aliases[len(args)] = state_out[n]
            args.append(buf)
            in_specs.append(pl.BlockSpec(memory_space=pl.ANY))
    return pl.pallas_call(
        functools.partial(_inproj_body, cum_block=cum_block, prompt=prompt, n_prev=n_prev),
        grid=(bsz, seq // tb),
        in_specs=in_specs, out_specs=out_specs, out_shape=out_shapes,
        scratch_shapes=[pltpu.VMEM((1, LANES), F32)],
        input_output_aliases=aliases,
        compiler_params=_params("arbitrary", "arbitrary"),
        name="inproj_prompt" if prompt else "inproj_sample",
    )(*args)


def _attn_body(q_ref, kx_ref, v_ref, o_ref, m_s, acc_s, *, tq, pairs):
    qi = pl.program_id(2)
    lane_x = lax.broadcasted_iota(jnp.int32, (tq, PAIR_EXT), 1)
    lane = lax.broadcasted_iota(jnp.int32, (tq, LANES), 1)
    head_lanes, q_heads = [], []
    for a in range(2):
        head_lanes.append((lane >= HEAD_DIM * a) & (lane < HEAD_DIM * (a + 1)))
    for p in range(pairs):
        qx = q_ref[:, p * PAIR_EXT:(p + 1) * PAIR_EXT]
        for a in range(2):
            mine = (((lane_x >= HEAD_DIM * a) & (lane_x < HEAD_DIM * (a + 1)))
                    | ((lane_x >= LANES + AUG_STRIDE * a) & (lane_x < LANES + AUG_STRIDE * (a + 1))))
            q_heads.append(jnp.where(mine, qx, jnp.zeros_like(qx)))
    row = lax.broadcasted_iota(jnp.int32, (tq, tq), 0)
    col = lax.broadcasted_iota(jnp.int32, (tq, tq), 1)

    def block(j, first):
        start = pl.multiple_of(j * tq, tq)
        for p in range(pairs):
            kx = kx_ref[p * PAIR_EXT:(p + 1) * PAIR_EXT, pl.ds(start, tq)]
            v = v_ref[pl.ds(start, tq), p * LANES:(p + 1) * LANES]
            for a in range(2):
                h = 2 * p + a
                s = jnp.dot(q_heads[h], kx, preferred_element_type=F32)
                if first:
                    s = jnp.where(col <= row, s, NEG_INF)
                va = jnp.where(head_lanes[a], v, jnp.ones_like(v))
                m_new = jnp.broadcast_to(jnp.max(s, axis=-1, keepdims=True), (tq, LANES))
                if not first:
                    m_old = m_s[h]
                    m_new = jnp.maximum(m_old, m_new)
                prob = jnp.concatenate([jnp.exp2(s[:, c * LANES:(c + 1) * LANES] - m_new)
                                        for c in range(tq // LANES)], axis=-1).astype(BF16)
                out = jnp.dot(prob, va, preferred_element_type=F32)
                acc_s[h] = out if first else jnp.exp2(m_old - m_new) * acc_s[h] + out
                m_s[h] = m_new

    block(qi, True)

    def body(j, carry):
        block(j, False)
        return carry

    lax.fori_loop(0, qi, body, 0)
    for p in range(pairs):
        outs = [acc_s[2 * p + a] / pltpu.roll(acc_s[2 * p + a], HEAD_DIM, 1) for a in range(2)]
        o_ref[:, p * LANES:(p + 1) * LANES] = jnp.where(head_lanes[0], outs[0], outs[1])


def _attn_prompt(qx, kx, vb, tq, pairs):
    bsz, seq, _ = qx.shape
    return pl.pallas_call(
        functools.partial(_attn_body, tq=tq, pairs=pairs),
        grid=(bsz, HEAD_PAIRS // pairs, seq // tq),
        in_specs=[pl.BlockSpec((None, tq, pairs * PAIR_EXT), lambda b, h, i: (b, i, h)),
                  pl.BlockSpec((None, pairs * PAIR_EXT, seq), lambda b, h, i: (b, h, 0)),
                  pl.BlockSpec((None, seq, pairs * LANES), lambda b, h, i: (b, 0, h))],
        out_specs=pl.BlockSpec((None, tq, pairs * LANES), lambda b, h, i: (b, i, h)),
        out_shape=jax.ShapeDtypeStruct((bsz, seq, ATT_DIM), F32),
        scratch_shapes=[pltpu.VMEM((2 * pairs, tq, LANES), F32), pltpu.VMEM((2 * pairs, tq, LANES), F32)],
        compiler_params=_params("arbitrary", "arbitrary", "arbitrary"),
        name="attn_prompt",
    )(qx, kx, vb)


def _attn_sample_body(q_ref, ckt_ref, cvt_ref, kn_ref, vn_ref, cq_ref, ctc_ref, ctn_ref, o_ref):
    hp = pl.program_id(1)
    tq = q_ref.shape[0]
    past = ckt_ref.shape[-1]
    q = q_ref[...]
    cq8 = cq_ref[...]
    lane = lax.broadcasted_iota(jnp.int32, (tq, LANES), 1)
    lane8 = lax.broadcasted_iota(jnp.int32, (tq, N_HEADS), 1)
    row = lax.broadcasted_iota(jnp.int32, (tq, tq), 0)
    col = lax.broadcasted_iota(jnp.int32, (tq, tq), 1)
    q2 = jnp.concatenate(
        [jnp.where((lane >= HEAD_DIM * a) & (lane < HEAD_DIM * (a + 1)), q, jnp.zeros_like(q)) for a in range(2)],
        axis=0)
    kt = ckt_ref[...].reshape(LANES, past).astype(BF16)
    vt = cvt_ref[...].reshape(LANES, past).astype(BF16)
    s_cache = jnp.dot(q2, kt, preferred_element_type=F32)
    s_new = _nt_dot(q2, kn_ref[...])
    sc, sn = [], []
    for a in range(2):
        head = 2 * hp + a
        cqa = jnp.sum(jnp.where(lane8 == head, cq8, 0.0), axis=-1, keepdims=True)
        ck_cache = ctc_ref[pl.ds(head, 1), :] * LOG2E
        ck_new = ctn_ref[pl.ds(head, 1), :]
        sc.append(s_cache[a * tq:(a + 1) * tq] + cqa - ck_cache)
        sn.append(jnp.where(col <= row, s_new[a * tq:(a + 1) * tq] + cqa - ck_new, NEG_INF))
    s_cache = jnp.concatenate(sc, axis=0)
    s_new = jnp.concatenate(sn, axis=0)
    m = jnp.maximum(jnp.max(s_cache, axis=-1, keepdims=True), jnp.max(s_new, axis=-1, keepdims=True))
    p_cache = jnp.exp2(s_cache - m)
    p_new = jnp.exp2(s_new - m)
    l = jnp.sum(p_cache, axis=-1, keepdims=True) + jnp.sum(p_new, axis=-1, keepdims=True)
    o = (_nt_dot(p_cache.astype(BF16), vt)
         + jnp.dot(p_new.astype(BF16), vn_ref[...], preferred_element_type=F32)) / l
    o_ref[...] = jnp.where(lane < HEAD_DIM, o[0:tq], o[tq:2 * tq])


def _attn_sample(q, kb, vb, cq, ct_new, cache_kt, cache_vt, ct_cache, layer):
    bsz, seq, _ = q.shape
    past = cache_kt.shape[-1]
    cache_blk = pl.BlockSpec((None, None, 2, HEAD_DIM, past), lambda b, h: (layer, b, h, 0, 0))
    new_blk = pl.BlockSpec((None, seq, LANES), lambda b, h: (b, 0, h))
    return pl.pallas_call(
        _attn_sample_body,
        grid=(bsz, HEAD_PAIRS),
        in_specs=[new_blk, cache_blk, cache_blk, new_blk, new_blk,
                  pl.BlockSpec((None, seq, N_HEADS), lambda b, h: (b, 0, 0)),
                  pl.BlockSpec((None, N_HEADS, past), lambda b, h: (b, 0, 0)),
                  pl.BlockSpec((None, N_HEADS, seq), lambda b, h: (b, 0, 0))],
        out_specs=new_blk,
        out_shape=jax.ShapeDtypeStruct((bsz, seq, ATT_DIM), F32),
        compiler_params=_params("arbitrary", "arbitrary"),
        name="attn_sample",
    )(q, cache_kt, cache_vt, kb, vb, cq, ct_cache, ct_new)


def _cache_cumsum_body(x_ref, o_ref):
    rows, past = x_ref.shape
    tri = (lax.broadcasted_iota(jnp.int32, (LANES, LANES), 0)
           <= lax.broadcasted_iota(jnp.int32, (LANES, LANES), 1)).astype(F32)
    carry = jnp.zeros((rows, 1), F32)
    for b in range(past // LANES):
        blk = x_ref[:, b * LANES:(b + 1) * LANES]
        c = jnp.dot(blk, tri, precision=HIGHEST, preferred_element_type=F32) + carry
        o_ref[:, b * LANES:(b + 1) * LANES] = c
        carry = c[:, LANES - 1:LANES]


def _cache_cumsum(x):
    return pl.pallas_call(
        _cache_cumsum_body,
        out_shape=jax.ShapeDtypeStruct(x.shape, F32),
        compiler_params=pltpu.CompilerParams(vmem_limit_bytes=VMEM_LIMIT),
        name="cache_cumsum",
    )(x)


def _s5_prep_body(lr_ref, li_ref, ldt_ref, br_ref, bi_ref, ar_ref, ai_ref, bbr_ref, bbi_ref):
    lr = jnp.minimum(lr_ref[...], -1e-4)
    li = li_ref[...]
    dt = jnp.exp(ldt_ref[...])
    ldr, ldi = lr * dt, li * dt
    mag = jnp.exp(ldr)
    ar, ai = mag * jnp.cos(ldi), mag * jnp.sin(ldi)
    den = lr * lr + li * li
    nr = ar - 1.0
    qr = (nr * lr + ai * li) / den
    qi = (ai * lr - nr * li) / den
    ar_ref[...] = ar
    ai_ref[...] = ai
    br, bi = br_ref[...], bi_ref[...]
    bbr_ref[...] = qr[:, None, :] * br - qi[:, None, :] * bi
    bbi_ref[...] = qr[:, None, :] * bi + qi[:, None, :] * br


def _s5_prep(lam_re, lam_im, log_dt, b_re, b_im):
    g, p = lam_re.shape
    gp = jax.ShapeDtypeStruct((g, p), F32)
    gcp = jax.ShapeDtypeStruct((g, SSM_GROUP, p), F32)
    return pl.pallas_call(
        _s5_prep_body, out_shape=(gp, gp, gcp, gcp), name="s5_prep",
    )(lam_re, lam_im, log_dt.reshape(g, 1), jnp.swapaxes(b_re, 1, 2), jnp.swapaxes(b_im, 1, 2))


def _block_diag(t):
    g, c, p = t.shape
    eye = jnp.eye(g, dtype=t.dtype)
    return (t[:, :, None, :] * eye[:, None, :, None]).reshape(g * c, g * p)


def _ssm_body(u_ref, z_ref, gb_ref, h0r_ref, h0i_ref, cprev_ref,
              bre_ref, bim_ref, cre_ref, cim_ref, ar_ref, ai_ref, d_ref, wglu_ref, bglu_ref, cw_ref,
              ssm_ref, conv_ref, hr_out, hi_out, cst_out,
              sre, sim, hr_s, hi_s, zprev_s, *, lb):
    j = pl.program_id(1)
    nb = SUBLANES
    tiles = STATE_LANES // LANES

    @pl.when(j == 0)
    def _():
        hr_s[...] = h0r_ref[...]
        hi_s[...] = h0i_ref[...]
        zprev_s[...] = cprev_ref[...]

    u2 = jnp.swapaxes(u_ref[...], 0, 1).reshape(lb * nb, SSM_DIM)
    ub = u2.astype(BF16)
    bu_re = jnp.dot(ub, bre_ref[...], preferred_element_type=F32)
    bu_im = jnp.dot(ub, bim_ref[...], preferred_element_type=F32)
    for c in range(tiles):
        sre[c] = bu_re[:, c * LANES:(c + 1) * LANES]
        sim[c] = bu_im[:, c * LANES:(c + 1) * LANES]
    ar = [jnp.broadcast_to(ar_ref[:, c * LANES:(c + 1) * LANES], (nb, LANES)) for c in range(tiles)]
    ai = [jnp.broadcast_to(ai_ref[:, c * LANES:(c + 1) * LANES], (nb, LANES)) for c in range(tiles)]

    def step(t, carry):
        hr, hi = carry
        rows = pl.ds(pl.multiple_of(t * nb, nb), nb)
        nr, ni = [], []
        for c in range(tiles):
            r = ar[c] * hr[c] - ai[c] * hi[c] + sre[c, rows, :]
            i = ar[c] * hi[c] + ai[c] * hr[c] + sim[c, rows, :]
            sre[c, rows, :] = r
            sim[c, rows, :] = i
            nr.append(r)
            ni.append(i)
        return tuple(nr), tuple(ni)

    init = (tuple(hr_s[:, c * LANES:(c + 1) * LANES] for c in range(tiles)),
            tuple(hi_s[:, c * LANES:(c + 1) * LANES] for c in range(tiles)))
    hr, hi = lax.fori_loop(0, lb, step, init)
    for c in range(tiles):
        hr_s[:, c * LANES:(c + 1) * LANES] = hr[c]
        hi_s[:, c * LANES:(c + 1) * LANES] = hi[c]
        hr_out[:, c * LANES:(c + 1) * LANES] = hr[c]
        hi_out[:, c * LANES:(c + 1) * LANES] = hi[c]

    h_re = jnp.concatenate([sre[c] for c in range(tiles)], axis=-1).astype(BF16)
    h_im = jnp.concatenate([sim[c] for c in range(tiles)], axis=-1).astype(BF16)
    y = (jnp.dot(h_re, cre_ref[...], preferred_element_type=F32)
         - jnp.dot(h_im, cim_ref[...], preferred_element_type=F32))
    y = y + d_ref[...] * u2
    g = jax.nn.gelu(y)
    gate = jax.nn.sigmoid(jnp.dot(g.astype(BF16), wglu_ref[...], preferred_element_type=F32) + bglu_ref[...])
    ssm_ref[...] = jnp.swapaxes((g * gate).reshape(lb, nb, SSM_DIM), 0, 1)

    tt = lax.broadcasted_iota(jnp.int32, (lb, CONV_DIM), 0)
    w0, w1, w2 = cw_ref[0:1, :], cw_ref[1:2, :], cw_ref[2:3, :]
    for b in range(nb):
        zb = z_ref[b]
        prev = zprev_s[b]
        z1 = jnp.where(tt == 0, prev[1:2], pltpu.roll(zb, 1, 0))
        z2 = jnp.where(tt == 0, prev[0:1], jnp.where(tt == 1, prev[1:2], pltpu.roll(zb, 2, 0)))
        conv_ref[b] = gb_ref[b] * (w0 * z2 + w1 * z1 + w2 * zb)
        last = zb[lb - (CONV_WIDTH - 1):lb]
        zprev_s[b] = last
        cst_out[b] = last


def _ssm_conv(u, z, gb, h0r, h0i, cprev, lw, lb):
    bsz, seq, _ = u.shape
    nb = SUBLANES
    tok = pl.BlockSpec((nb, lb, SSM_DIM), lambda g, j: (g, j, 0))
    st = pl.BlockSpec((nb, STATE_LANES), lambda g, j: (g, 0))
    cs = pl.BlockSpec((nb, CONV_WIDTH - 1, CONV_DIM), lambda g, j: (g, 0, 0))
    return pl.pallas_call(
        functools.partial(_ssm_body, lb=lb),
        grid=(bsz // nb, seq // lb),
        in_specs=[tok, tok, tok, st, st, cs,
                  _const_spec((SSM_DIM, STATE_LANES)), _const_spec((SSM_DIM, STATE_LANES)),
                  _const_spec((STATE_LANES, SSM_DIM)), _const_spec((STATE_LANES, SSM_DIM)),
                  _const_spec((1, STATE_LANES)), _const_spec((1, STATE_LANES)),
                  _const_spec((1, SSM_DIM)), _const_spec((SSM_DIM, SSM_DIM)), _const_spec((1, SSM_DIM)),
                  _const_spec((CONV_WIDTH, CONV_DIM))],
        out_specs=(tok, tok, st, st, cs),
        out_shape=(jax.ShapeDtypeStruct((bsz, seq, SSM_DIM), F32),
                   jax.ShapeDtypeStruct((bsz, seq, CONV_DIM), F32),
                   jax.ShapeDtypeStruct((bsz, STATE_LANES), F32),
                   jax.ShapeDtypeStruct((bsz, STATE_LANES), F32),
                   jax.ShapeDtypeStruct((bsz, CONV_WIDTH - 1, CONV_DIM), F32)),
        scratch_shapes=[pltpu.VMEM((STATE_LANES // LANES, nb * lb, LANES), F32),
                        pltpu.VMEM((STATE_LANES // LANES, nb * lb, LANES), F32),
                        pltpu.VMEM((nb, STATE_LANES), F32), pltpu.VMEM((nb, STATE_LANES), F32),
                        pltpu.VMEM((nb, CONV_WIDTH - 1, CONV_DIM), F32)],
        compiler_params=_params("arbitrary", "arbitrary"),
        name="ssm_conv",
    )(u, z, gb, h0r, h0i, cprev, lw["b_re_blk"], lw["b_im_blk"], lw["c_re_blk"], lw["c_im_blk"],
      lw["a_re"], lw["a_im"], lw["d"], lw["w_glu"], lw["b_glu"], lw["conv_w"])


def _rms_rows(t, w):
    return t * lax.rsqrt(jnp.mean(t * t, axis=-1, keepdims=True) + RMS_EPS) * w


def _mix_ffn_body(x_ref, att_ref, ssm_ref, conv_ref, bn_ref, wout_ref, ln2_ref, wup_ref, wdown_ref, o_ref,
                  *, ff_chunk):
    bn = bn_ref[...]
    mix = jnp.concatenate(
        [_rms_rows(att_ref[...], bn[:, 0:ATT_DIM]),
         _rms_rows(ssm_ref[...], bn[:, ATT_DIM:ATT_DIM + SSM_DIM]),
         _rms_rows(conv_ref[...], bn[:, ATT_DIM + SSM_DIM:])], axis=-1).astype(BF16)
    x1 = x_ref[...] + jnp.dot(mix, wout_ref[...], preferred_element_type=F32)
    h2 = _rms_rows(x1, ln2_ref[...]).astype(BF16)
    acc = x1
    for c in range(D_FF // ff_chunk):
        f = jnp.maximum(jnp.dot(h2, wup_ref[:, c * ff_chunk:(c + 1) * ff_chunk], preferred_element_type=F32), 0.0)
        acc = acc + jnp.dot((f * f).astype(BF16), wdown_ref[c * ff_chunk:(c + 1) * ff_chunk, :],
                            preferred_element_type=F32)
    o_ref[...] = acc


def _mix_ffn(x, att, ssm, conv, lw, tb):
    bsz, seq, _ = x.shape
    rows = bsz * seq
    tok = lambda w: pl.BlockSpec((tb, w), lambda i: (i, 0))
    once = lambda shape: pl.BlockSpec(shape, lambda i: (0, 0), pipeline_mode=pl.Buffered(1))
    out = pl.pallas_call(
        functools.partial(_mix_ffn_body, ff_chunk=1024),
        grid=(rows // tb,),
        in_specs=[tok(D_MODEL), tok(ATT_DIM), tok(SSM_DIM), tok(CONV_DIM),
                  once((1, D_MODEL)), once((D_MODEL, D_MODEL)), once((1, D_MODEL)),
                  once((D_MODEL, D_FF)), once((D_FF, D_MODEL))],
        out_specs=tok(D_MODEL),
        out_shape=jax.ShapeDtypeStruct((rows, D_MODEL), F32),
        compiler_params=_params("arbitrary"),
        name="mix_ffn",
    )(x.reshape(rows, D_MODEL), att.reshape(rows, ATT_DIM), ssm.reshape(rows, SSM_DIM),
      conv.reshape(rows, CONV_DIM), lw["bn"], lw["w_out"], lw["ln2"], lw["w_up"], lw["w_down"])
    return out.reshape(bsz, seq, D_MODEL)


def _layer_weights(l, ln1_w, w_in, b_forget, q_norm_w, k_norm_w, conv_w, ssm_lam_re, ssm_lam_im, ssm_log_dt,
                   ssm_b_re, ssm_b_im, ssm_c_re, ssm_c_im, ssm_d, w_glu, b_glu, branch_norm_w, w_out,
                   ln2_w, w_up, w_down):
    w = w_in[l]
    fg0 = 3 * ATT_DIM
    w_re = jnp.concatenate([w[:, :fg0], w[:, fg0 + N_HEADS:], w[:, fg0:fg0 + N_HEADS],
                            jnp.zeros((D_MODEL, LANES - N_HEADS), F32)], axis=1).astype(BF16)
    head_id = jnp.arange(ATT_DIM) // HEAD_DIM
    a_re, a_im, bb_re, bb_im = _s5_prep(ssm_lam_re[l], ssm_lam_im[l], ssm_log_dt[l], ssm_b_re[l], ssm_b_im[l])
    return dict(
        ln1=ln1_w[l].reshape(1, D_MODEL), w_in=w_re,
        qn=jnp.tile(q_norm_w[l], N_HEADS).reshape(1, ATT_DIM), kn=jnp.tile(k_norm_w[l], N_HEADS).reshape(1, ATT_DIM),
        bf=jnp.pad(b_forget[l], (0, LANES - N_HEADS)).reshape(1, LANES),
        mbd=(head_id[:, None] == head_id[None, :]).astype(BF16),
        a_re=a_re.reshape(1, STATE_LANES), a_im=a_im.reshape(1, STATE_LANES),
        b_re_blk=_block_diag(bb_re).astype(BF16), b_im_blk=_block_diag(bb_im).astype(BF16),
        c_re_blk=_block_diag(ssm_c_re[l]).T.astype(BF16), c_im_blk=_block_diag(ssm_c_im[l]).T.astype(BF16),
        d=ssm_d[l].reshape(1, SSM_DIM), w_glu=w_glu[l].astype(BF16), b_glu=b_glu[l].reshape(1, SSM_DIM),
        conv_w=conv_w[l], bn=branch_norm_w[l].reshape(1, D_MODEL), w_out=w_out[l].astype(BF16),
        ln2=ln2_w[l].reshape(1, D_MODEL), w_up=w_up[l].astype(BF16), w_down=w_down[l].astype(BF16))


def _run_layer(x, lw, layer, depth, prev, past, tiles):
    bsz, seq, _ = x.shape
    if past is None:
        c0 = jnp.zeros((bsz, 1, LANES), F32)
        h0r = jnp.zeros((bsz, STATE_LANES), F32)
        h0i = h0r
        cprev = jnp.zeros((bsz, CONV_WIDTH - 1, CONV_DIM), F32)
        qx, kx, vb, k_all, v_all, lft, u, z, gb = _inproj(x, c0, lw, tiles["tok"], layer, depth, prev, True)
        att = _attn_prompt(qx, kx, vb, tiles["attn"], tiles["attn_pairs"])
    else:
        cache_kt, cache_vt, clf_t, s_re, s_im, s_conv = past
        plen = cache_kt.shape[-1]
        ct_cache = _cache_cumsum(clf_t.reshape(bsz * N_HEADS, plen)).reshape(bsz, N_HEADS, plen)
        c0 = jnp.pad(ct_cache[:, :, plen - 1], ((0, 0), (0, LANES - N_HEADS))).reshape(bsz, 1, LANES)
        h0r = s_re.reshape(bsz, STATE_LANES)
        h0i = s_im.reshape(bsz, STATE_LANES)
        cprev = s_conv
        q, kb, vb, k_all, v_all, lft, cq, ct, u, z, gb = _inproj(x, c0, lw, tiles["tok"], layer, depth, prev, False)
        att = _attn_sample(q, kb, vb, cq, ct, cache_kt, cache_vt, ct_cache, layer)
    ssm, conv, hr, hi, cst = _ssm_conv(u, z, gb, h0r, h0i, cprev, lw, tiles["scan"])
    y = _mix_ffn(x, att, ssm, conv, lw, tiles["ffn"])
    small = (lft, hr.reshape(bsz, SSM_GROUPS, SSM_STATE), hi.reshape(bsz, SSM_GROUPS, SSM_STATE), cst)
    return y, (k_all, v_all), small


def _tiles(seq):
    tok = min(seq, 512)
    return dict(tok=tok, attn=min(seq, 512), attn_pairs=2, scan=min(seq, 128), ffn=tok)


def _forward(x_prompt, x_sample, cache_k, cache_v, cache_logf, state_ssm_re, state_ssm_im, state_conv, *weights):
    depth = cache_k.shape[0]
    xp, xs = x_prompt, x_sample
    bp, lp = xp.shape[0], xp.shape[1]
    tiles_p = _tiles(lp)
    tiles_s = _tiles(xs.shape[1])
    cache_kt = jnp.transpose(cache_k, (0, 1, 3, 4, 2))
    cache_vt = jnp.transpose(cache_v, (0, 1, 3, 4, 2))
    cache_lft = jnp.swapaxes(cache_logf, 2, 3)
    kv_p, kv_s, small_p, small_s = None, None, [], []
    for l in range(depth):
        lw = _layer_weights(l, *weights)
        xp, kv_p, sp = _run_layer(xp, lw, l, depth, kv_p, None, tiles_p)
        xs, kv_s, ss = _run_layer(xs, lw, l, depth, kv_s,
                                  (cache_kt, cache_vt, cache_lft[l], state_ssm_re[l], state_ssm_im[l], state_conv[l]),
                                  tiles_s)
        small_p.append(sp)
        small_s.append(ss)
    stk = lambda lst, i: jnp.stack([s[i] for s in lst])
    from_t = lambda t: jnp.transpose(t.reshape(depth, bp, N_HEADS, HEAD_DIM, lp), (0, 1, 4, 2, 3))
    return (xp, xs,
            from_t(kv_p[0]), from_t(kv_p[1]), jnp.swapaxes(stk(small_p, 0), 2, 3),
            stk(small_p, 1), stk(small_p, 2), stk(small_p, 3),
            kv_s[0], kv_s[1], jnp.swapaxes(stk(small_s, 0), 2, 3),
            stk(small_s, 1), stk(small_s, 2), stk(small_s, 3))


def kernel(x_prompt, x_sample, cache_k, cache_v, cache_logf, state_ssm_re, state_ssm_im, state_conv, ln1_w, w_in, b_forget, q_norm_w, k_norm_w, conv_w, ssm_lam_re, ssm_lam_im, ssm_log_dt, ssm_b_re, ssm_b_im, ssm_c_re, ssm_c_im, ssm_d, w_glu, b_glu, branch_norm_w, w_out, ln2_w, w_up, w_down):
    return _forward(x_prompt, x_sample, cache_k, cache_v, cache_logf, state_ssm_re, state_ssm_im, state_conv,
                    ln1_w, w_in, b_forget, q_norm_w, k_norm_w, conv_w, ssm_lam_re, ssm_lam_im, ssm_log_dt,
                    ssm_b_re, ssm_b_im, ssm_c_re, ssm_c_im, ssm_d, w_glu, b_glu, branch_norm_w, w_out,
                    ln2_w, w_up, w_down)
```

```python
import functools
import math

import jax
import jax.numpy as jnp
from jax import lax
from jax.experimental import pallas as pl
from jax.experimental.pallas import tpu as pltpu

F32 = jnp.float32
BF16 = jnp.bfloat16
HIGHEST = lax.Precision.HIGHEST

D_MODEL = 1024
N_HEADS = 8
HEAD_DIM = 64
ATT_DIM = N_HEADS * HEAD_DIM
SSM_DIM = 256
SSM_GROUP = 16
SSM_GROUPS = SSM_DIM // SSM_GROUP
SSM_STATE = 64
STATE_LANES = SSM_GROUPS * SSM_STATE
CONV_DIM = 256
CONV_WIDTH = 3
D_FF = 4 * D_MODEL
RMS_EPS = 1e-6
NEG_INF = -1e30
LOG2E = math.log2(math.e)

LANES = 128
SUBLANES = 8
HEAD_PAIRS = ATT_DIM // LANES
PAIR_EXT = 2 * LANES
AUG_STRIDE = 8
BIAS_PIECES = 3
PROJ_PAD = 3 * ATT_DIM + SSM_DIM + 3 * CONV_DIM + LANES
VMEM_LIMIT = 56 * 1024 * 1024


def _params(*sem):
    return pltpu.CompilerParams(dimension_semantics=sem, vmem_limit_bytes=VMEM_LIMIT)


def _const_spec(shape):
    zeros = (0,) * len(shape)
    return pl.BlockSpec(shape, lambda *_: zeros)


def _nt_dot(a, b, **kw):
    return lax.dot_general(a, b, (((1,), (1,)), ((), ())), preferred_element_type=F32, **kw)


def _bias_selectors():
    h = jnp.arange(N_HEADS)
    base = LANES * (h // 2) + AUG_STRIDE * (h % 2)
    sq = jnp.zeros((LANES, ATT_DIM), F32)
    sk = jnp.zeros((ATT_DIM, LANES), F32)
    for r in range(BIAS_PIECES):
        sq = sq.at[N_HEADS * r + h, base + r].set(1.0)
        sk = sk.at[base + BIAS_PIECES + r, N_HEADS * r + h].set(-1.0)
    return sq.astype(BF16), sk.astype(BF16)


def _split3(t):
    p1 = t.astype(BF16)
    r1 = t - p1.astype(F32)
    p2 = r1.astype(BF16)
    p3 = (r1 - p2.astype(F32)).astype(BF16)
    return p1, p2, p3


def _inproj_body(*refs, cum_block, prompt, n_prev):
    (x_ref, ln1_ref, w_ref, qnw_ref, knw_ref, bf_ref, mbd_ref, tri_ref, sel_ref, c0_ref, sq_ref, sk_ref) = refs[:12]
    outs = refs[12 + n_prev:]
    carry_ref = outs[-1]
    i = pl.program_id(1)
    x = x_ref[...]
    ms = jnp.mean(x * x, axis=-1, keepdims=True)
    h = (x * lax.rsqrt(ms + RMS_EPS) * ln1_ref[...]).astype(BF16)
    proj = jnp.dot(h, w_ref[...], preferred_element_type=F32)
    tb = x.shape[0]

    def head_norm(t, w):
        ss = jnp.dot((t * t).astype(BF16), mbd_ref[...], preferred_element_type=F32)
        return t * lax.rsqrt(ss * (1.0 / HEAD_DIM) + RMS_EPS) * w

    q = head_norm(proj[:, 0:ATT_DIM], qnw_ref[...]) * (LOG2E * HEAD_DIM ** -0.5)
    k = head_norm(proj[:, ATT_DIM:2 * ATT_DIM], knw_ref[...])
    v = proj[:, 2 * ATT_DIM:3 * ATT_DIM]
    o = 3 * ATT_DIM
    u = proj[:, o:o + SSM_DIM]
    hc = proj[:, o + SSM_DIM:o + SSM_DIM + CONV_DIM]
    gb = proj[:, o + SSM_DIM + CONV_DIM:o + SSM_DIM + 2 * CONV_DIM]
    gc = proj[:, o + SSM_DIM + 2 * CONV_DIM:o + SSM_DIM + 3 * CONV_DIM]
    lf = jax.nn.log_sigmoid(proj[:, PROJ_PAD - LANES:PROJ_PAD] + bf_ref[...])

    @pl.when(i == 0)
    def _():
        carry_ref[...] = c0_ref[...]

    carry = carry_ref[...]
    pieces = []
    for s in range(tb // cum_block):
        blk = jnp.concatenate(_split3(lf[s * cum_block:(s + 1) * cum_block]), axis=-1)
        r = jnp.dot(tri_ref[...], blk, preferred_element_type=F32)
        c = r[:, 0:LANES] + r[:, LANES:2 * LANES] + r[:, 2 * LANES:3 * LANES] + carry
        carry = c[cum_block - 1:cum_block]
        pieces.append(c)
    c_all = pieces[0] if len(pieces) == 1 else jnp.concatenate(pieces, axis=0)
    carry_ref[...] = carry
    c2 = c_all * LOG2E
    lf_t = _nt_dot(sel_ref[...], lf, precision=HIGHEST)

    if prompt:
        qx_ref, kx_ref, vb_ref, kt_ref, vt_ref, lft_ref, u_ref, z_ref, gb_ref = outs[:-1]
        head_lane = lax.broadcasted_iota(jnp.int32, (tb, LANES), 1) < N_HEADS
        p1, p2, p3 = (p.astype(F32) for p in _split3(jnp.where(head_lane, c2, 0.0)))
        pcs = (p1 + pltpu.roll(p2, N_HEADS, 1) + pltpu.roll(p3, 2 * N_HEADS, 1)).astype(BF16)
        lane = lax.broadcasted_iota(jnp.int32, (tb, ATT_DIM), 1) % LANES
        q_one = (lane < 2 * AUG_STRIDE) & (lane % AUG_STRIDE >= BIAS_PIECES) & (lane % AUG_STRIDE < 2 * BIAS_PIECES)
        qa = jnp.dot(pcs, sq_ref[...], preferred_element_type=F32) + q_one.astype(F32)
        row = lax.broadcasted_iota(jnp.int32, (ATT_DIM, tb), 0) % LANES
        k_one = (row < 2 * AUG_STRIDE) & (row % AUG_STRIDE < BIAS_PIECES)
        ka_t = _nt_dot(sk_ref[...], pcs) + k_one.astype(F32)
        k_t = k.T
        qb, qab, ktb, kab = q.astype(BF16), qa.astype(BF16), k_t.astype(BF16), ka_t.astype(BF16)
        qx, kx = [], []
        for p in range(HEAD_PAIRS):
            sl = slice(p * LANES, (p + 1) * LANES)
            qx += [qb[:, sl], qab[:, sl]]
            kx += [ktb[sl], kab[sl]]
        qx_ref[...] = jnp.concatenate(qx, axis=-1)
        kx_ref[...] = jnp.concatenate(kx, axis=0)
        vb_ref[...] = v.astype(BF16)
        kt_ref[...] = k_t
        vt_ref[...] = v.T
    else:
        q_ref, kb_ref, vb_ref, kf_ref, vf_ref, lft_ref, cq_ref, ct_ref, u_ref, z_ref, gb_ref = outs[:-1]
        q_ref[...] = q.astype(BF16)
        kb_ref[...] = k.astype(BF16)
        vb_ref[...] = v.astype(BF16)
        kf_ref[...] = k.reshape(tb, N_HEADS, HEAD_DIM)
        vf_ref[...] = v.reshape(tb, N_HEADS, HEAD_DIM)
        cq_ref[...] = c2[:, 0:N_HEADS]
        ct_ref[...] = _nt_dot(sel_ref[...], c2, precision=HIGHEST)
    lft_ref[...] = lf_t
    u_ref[...] = u
    z_ref[...] = gc * hc
    gb_ref[...] = gb


def _inproj(x, c0, lw, tb, layer, depth, prev, prompt):
    bsz, seq, _ = x.shape
    cum_block = min(tb, LANES)
    tri = (lax.broadcasted_iota(jnp.int32, (cum_block, cum_block), 1)
           <= lax.broadcasted_iota(jnp.int32, (cum_block, cum_block), 0)).astype(BF16)
    sel =(lax.broadcasted_iota(jnp.int32, (N_HEADS, LANES), 0)
           == lax.broadcasted_iota(jnp.int32, (N_HEADS, LANES), 1)).astype(F32)
    sq, sk = _bias_selectors()
    tok = lambda w: pl.BlockSpec((None, tb, w), lambda b, i: (b, i, 0))
    head_major = pl.BlockSpec((None, N_HEADS, tb), lambda b, i: (b, 0, i))
    tok_shape = lambda w, dt: jax.ShapeDtypeStruct((bsz, seq, w), dt)
    heads_shape = jax.ShapeDtypeStruct((bsz, N_HEADS, seq), F32)
    if prompt:
        state_shape = jax.ShapeDtypeStruct((depth, bsz, ATT_DIM, seq), F32)
        state_spec = pl.BlockSpec((None, None, ATT_DIM, tb), lambda b, i: (layer, b, 0, i))
        out_shapes = (tok_shape(HEAD_PAIRS * PAIR_EXT, BF16),
                      jax.ShapeDtypeStruct((bsz, HEAD_PAIRS * PAIR_EXT, seq), BF16),
                      tok_shape(ATT_DIM, BF16), state_shape, state_shape, heads_shape,
                      tok_shape(SSM_DIM, F32), tok_shape(CONV_DIM, F32), tok_shape(CONV_DIM, F32))
        out_specs = (tok(HEAD_PAIRS * PAIR_EXT),
                     pl.BlockSpec((None, HEAD_PAIRS * PAIR_EXT, tb), lambda b, i: (b, 0, i)),
                     tok(ATT_DIM), state_spec, state_spec, head_major,
                     tok(SSM_DIM), tok(CONV_DIM), tok(CONV_DIM))
        state_out = (3, 4)
    else:
        state_shape = jax.ShapeDtypeStruct((depth, bsz, seq, N_HEADS, HEAD_DIM), F32)
        state_spec = pl.BlockSpec((None, None, tb, N_HEADS, HEAD_DIM), lambda b, i: (layer, b, i, 0, 0))
        out_shapes = (tok_shape(ATT_DIM, BF16), tok_shape(ATT_DIM, BF16), tok_shape(ATT_DIM, BF16),
                      state_shape, state_shape, heads_shape,
                      tok_shape(N_HEADS, F32), heads_shape,
                      tok_shape(SSM_DIM, F32), tok_shape(CONV_DIM, F32), tok_shape(CONV_DIM, F32))
        out_specs = (tok(ATT_DIM), tok(ATT_DIM), tok(ATT_DIM), state_spec, state_spec, head_major,
                     tok(N_HEADS), head_major, tok(SSM_DIM), tok(CONV_DIM), tok(CONV_DIM))
        state_out = (3, 4)
    in_specs = [tok(D_MODEL), _const_spec((1, D_MODEL)), _const_spec((D_MODEL, PROJ_PAD)),
                _const_spec((1, ATT_DIM)), _const_spec((1, ATT_DIM)), _const_spec((1, LANES)),
                _const_spec((ATT_DIM, ATT_DIM)), _const_spec((cum_block, cum_block)),
                _const_spec((N_HEADS, LANES)),
                pl.BlockSpec((None, 1, LANES), lambda b, i: (b, 0, 0)),
                _const_spec(sq.shape), _const_spec(sk.shape)]
    args = [x, lw["ln1"], lw["w_in"], lw["qn"], lw["kn"], lw["bf"], lw["mbd"], tri, sel, c0, sq, sk]
    aliases = {}
    n_prev = 0
    if prev is not None:
        n_prev = len(prev)
        for n, buf in enumerate(prev):
            aliases[len(args)] = state_out[n]
            args.append(buf)
            in_specs.append(pl.BlockSpec(memory_space=pl.ANY))
    return pl.pallas_call(
        functools.partial(_inproj_body, cum_block=cum_block, prompt=prompt, n_prev=n_prev),
        grid=(bsz, seq // tb),
        in_specs=in_specs, out_specs=out_specs, out_shape=out_shapes,
        scratch_shapes=[pltpu.VMEM((1, LANES), F32)],
        input_output_aliases=aliases,
        compiler_params=_params("arbitrary", "arbitrary"),
        name="inproj_prompt" if prompt else "inproj_sample",
    )(*args)


def _attn_body(q_ref, kx_ref, v_ref, o_ref, m_s, acc_s, s_s, *, tq, pairs):
    qi = pl.program_id(2)
    lane_x = lax.broadcasted_iota(jnp.int32, (tq, PAIR_EXT), 1)
    lane = lax.broadcasted_iota(jnp.int32, (tq, LANES), 1)
    head_lanes, q_heads = [], []
    for a in range(2):
        head_lanes.append((lane >= HEAD_DIM * a) & (lane < HEAD_DIM * (a + 1)))
    for p in range(pairs):
        qx = q_ref[:, p * PAIR_EXT:(p + 1) * PAIR_EXT]
        for a in range(2):
            mine = (((lane_x >= HEAD_DIM * a) & (lane_x < HEAD_DIM * (a + 1)))
                    | ((lane_x >= LANES + AUG_STRIDE * a) & (lane_x < LANES + AUG_STRIDE * (a + 1))))
            q_heads.append(jnp.where(mine, qx, jnp.zeros_like(qx)))
    row = lax.broadcasted_iota(jnp.int32, (tq, tq), 0)
    col = lax.broadcasted_iota(jnp.int32, (tq, tq), 1)

    def scores(j, p):
        start = pl.multiple_of(j * tq, tq)
        kx = kx_ref[p * PAIR_EXT:(p + 1) * PAIR_EXT, pl.ds(start, tq)]
        for a in range(2):
            s_s[2 * p + a] = jnp.dot(q_heads[2 * p + a], kx, preferred_element_type=F32)

    def absorb(j, p, masked):
        start = pl.multiple_of(j * tq, tq)
        v = v_ref[pl.ds(start, tq), p * LANES:(p + 1) * LANES]
        for a in range(2):
            h = 2 * p + a
            s = s_s[h]
            if masked:
                s = jnp.where(col <= row, s, NEG_INF)
            va = jnp.where(head_lanes[a], v, jnp.ones_like(v))
            m_old = m_s[h]
            m_new = jnp.maximum(m_old, jnp.broadcast_to(jnp.max(s, axis=-1, keepdims=True), (tq, LANES)))
            prob = jnp.concatenate([jnp.exp2(s[:, c * LANES:(c + 1) * LANES] - m_new)
                                    for c in range(tq // LANES)], axis=-1).astype(BF16)
            out = jnp.dot(prob, va, preferred_element_type=F32)
            acc_s[h] = jnp.exp2(m_old - m_new) * acc_s[h] + out
            m_s[h] = m_new

    def block(j, masked, more):
        for p in range(pairs):
            if p + 1 < pairs:
                scores(j, p + 1)
            elif more:
                scores(j + 1, 0)
            absorb(j, p, masked)

    for h in range(2 * pairs):
        m_s[h] = jnp.full((tq, LANES), NEG_INF, F32)
        acc_s[h] = jnp.zeros((tq, LANES), F32)
    scores(0, 0)

    def body(j, carry):
        block(j, False, True)
        return carry

    lax.fori_loop(0, qi, body, 0)
    block(qi, True, False)

    for p in range(pairs):
        outs = [acc_s[2 * p + a] / pltpu.roll(acc_s[2 * p + a], HEAD_DIM, 1) for a in range(2)]
        o_ref[:, p * LANES:(p + 1) * LANES] = jnp.where(head_lanes[0], outs[0], outs[1])


def _attn_prompt(qx, kx, vb, tq, pairs):
    bsz, seq, _ = qx.shape
    return pl.pallas_call(
        functools.partial(_attn_body, tq=tq, pairs=pairs),
        grid=(bsz, HEAD_PAIRS // pairs, seq // tq),
        in_specs=[pl.BlockSpec((None, tq, pairs * PAIR_EXT), lambda b, h, i: (b, i, h)),
                  pl.BlockSpec((None, pairs * PAIR_EXT, seq), lambda b, h, i: (b, h, 0)),
                  pl.BlockSpec((None, seq, pairs * LANES), lambda b, h, i: (b, 0, h))],
        out_specs=pl.BlockSpec((None, tq, pairs * LANES), lambda b, h, i: (b, i, h)),
        out_shape=jax.ShapeDtypeStruct((bsz, seq, ATT_DIM), F32),
        scratch_shapes=[pltpu.VMEM((2 * pairs, tq, LANES), F32), pltpu.VMEM((2 * pairs, tq, LANES), F32),
                        pltpu.VMEM((2 * pairs, tq, tq), F32)],
        compiler_params=_params("arbitrary", "arbitrary", "arbitrary"),
        name="attn_prompt",
    )(qx, kx, vb)


def _attn_sample_body(q_ref, ckt_ref, cvt_ref, kn_ref, vn_ref, cq_ref, ctc_ref, ctn_ref, o_ref):
    hp = pl.program_id(1)
    tq = q_ref.shape[0]
    past = ckt_ref.shape[-1]
    q = q_ref[...]
    cq8 = cq_ref[...]
    lane = lax.broadcasted_iota(jnp.int32, (tq, LANES), 1)
    lane8 = lax.broadcasted_iota(jnp.int32, (tq, N_HEADS), 1)
    row = lax.broadcasted_iota(jnp.int32, (tq, tq), 0)
    col = lax.broadcasted_iota(jnp.int32, (tq, tq), 1)
    q2 = jnp.concatenate(
        [jnp.where((lane >= HEAD_DIM * a) & (lane < HEAD_DIM * (a + 1)), q, jnp.zeros_like(q)) for a in range(2)],
        axis=0)
    kt = ckt_ref[...].reshape(LANES, past).astype(BF16)
    vt = cvt_ref[...].reshape(LANES, past).astype(BF16)
    s_cache = jnp.dot(q2, kt, preferred_element_type=F32)
    s_new = _nt_dot(q2, kn_ref[...])
    sc, sn = [], []
    for a in range(2):
        head = 2 * hp + a
        cqa = jnp.sum(jnp.where(lane8 == head, cq8, 0.0), axis=-1, keepdims=True)
        ck_cache = ctc_ref[pl.ds(head, 1), :] * LOG2E
        ck_new = ctn_ref[pl.ds(head, 1), :]
        sc.append(s_cache[a * tq:(a + 1) * tq] + cqa - ck_cache)
        sn.append(jnp.where(col <= row, s_new[a * tq:(a + 1) * tq] + cqa - ck_new, NEG_INF))
    s_cache = jnp.concatenate(sc, axis=0)
    s_new = jnp.concatenate(sn, axis=0)
    m = jnp.maximum(jnp.max(s_cache, axis=-1, keepdims=True), jnp.max(s_new, axis=-1, keepdims=True))
    p_cache = jnp.exp2(s_cache - m)
    p_new = jnp.exp2(s_new - m)
    l = jnp.sum(p_cache, axis=-1, keepdims=True) + jnp.sum(p_new, axis=-1, keepdims=True)
    o = (_nt_dot(p_cache.astype(BF16), vt)
         + jnp.dot(p_new.astype(BF16), vn_ref[...], preferred_element_type=F32)) / l
    o_ref[...] = jnp.where(lane < HEAD_DIM, o[0:tq], o[tq:2 * tq])


def _attn_sample(q, kb, vb, cq, ct_new, cache_kt, cache_vt, ct_cache, layer):
    bsz, seq, _ = q.shape
    past = cache_kt.shape[-1]
    cache_blk = pl.BlockSpec((None, None, 2, HEAD_DIM, past), lambda b, h: (layer, b, h, 0, 0))
    new_blk = pl.BlockSpec((None, seq, LANES), lambda b, h: (b, 0, h))
    return pl.pallas_call(
        _attn_sample_body,
        grid=(bsz, HEAD_PAIRS),
        in_specs=[new_blk, cache_blk, cache_blk, new_blk, new_blk,
                  pl.BlockSpec((None, seq, N_HEADS), lambda b, h: (b, 0, 0)),
                  pl.BlockSpec((None, N_HEADS, past), lambda b, h: (b, 0, 0)),
                  pl.BlockSpec((None, N_HEADS, seq), lambda b, h: (b, 0, 0))],
        out_specs=new_blk,
        out_shape=jax.ShapeDtypeStruct((bsz, seq, ATT_DIM), F32),
        compiler_params=_params("arbitrary", "arbitrary"),
        name="attn_sample",
    )(q, cache_kt, cache_vt, kb, vb, cq, ct_cache, ct_new)


def _cache_cumsum_body(x_ref, o_ref):
    rows, past = x_ref.shape
    tri = (lax.broadcasted_iota(jnp.int32, (LANES, LANES), 0)
           <= lax.broadcasted_iota(jnp.int32, (LANES, LANES), 1)).astype(F32)
    carry = jnp.zeros((rows, 1), F32)
    for b in range(past // LANES):
        blk = x_ref[:, b * LANES:(b + 1) * LANES]
        c = jnp.dot(blk, tri, precision=HIGHEST, preferred_element_type=F32) + carry
        o_ref[:, b * LANES:(b + 1) * LANES] = c
        carry = c[:, LANES - 1:LANES]


def _cache_cumsum(x):
    return pl.pallas_call(
        _cache_cumsum_body,
        out_shape=jax.ShapeDtypeStruct(x.shape, F32),
        compiler_params=pltpu.CompilerParams(vmem_limit_bytes=VMEM_LIMIT),
        name="cache_cumsum",
    )(x)


def _s5_prep_body(lr_ref, li_ref, ldt_ref, br_ref, bi_ref, ar_ref, ai_ref, bbr_ref, bbi_ref):
    lr = jnp.minimum(lr_ref[...], -1e-4)
    li = li_ref[...]
    dt = jnp.exp(ldt_ref[...])
    ldr, ldi = lr * dt, li * dt
    mag = jnp.exp(ldr)
    ar, ai = mag * jnp.cos(ldi), mag * jnp.sin(ldi)
    den = lr * lr + li * li
    nr = ar - 1.0
    qr = (nr * lr + ai * li) / den
    qi = (ai * lr - nr * li) / den
    ar_ref[...] = ar
    ai_ref[...] = ai
    br, bi = br_ref[...], bi_ref[...]
    bbr_ref[...] = qr[:, None, :] * br - qi[:, None, :] * bi
    bbi_ref[...] = qr[:, None, :] * bi + qi[:, None, :] * br


def _s5_prep(lam_re, lam_im, log_dt, b_re, b_im):
    g, p = lam_re.shape
    gp = jax.ShapeDtypeStruct((g, p), F32)
    gcp = jax.ShapeDtypeStruct((g, SSM_GROUP, p), F32)
    return pl.pallas_call(
        _s5_prep_body, out_shape=(gp, gp, gcp, gcp), name="s5_prep",
    )(lam_re, lam_im, log_dt.reshape(g, 1), jnp.swapaxes(b_re, 1, 2), jnp.swapaxes(b_im, 1, 2))


def _block_diag(t):
    g, c, p = t.shape
    eye = jnp.eye(g, dtype=t.dtype)
    return (t[:, :, None, :] * eye[:, None, :, None]).reshape(g * c, g * p)


def _ssm_body(u_ref, z_ref, gb_ref, h0r_ref, h0i_ref, cprev_ref,
              bre_ref, bim_ref, cre_ref, cim_ref, ar_ref, ai_ref, d_ref, wglu_ref, bglu_ref, cw_ref,
              ssm_ref, conv_ref, hr_out, hi_out, cst_out,
              sre, sim, hr_s, hi_s, zprev_s, *, lb):
    j = pl.program_id(1)
    nb = SUBLANES
    tiles = STATE_LANES // LANES

    @pl.when(j == 0)
    def _():
        hr_s[...] = h0r_ref[...]
        hi_s[...] = h0i_ref[...]
        zprev_s[...] = cprev_ref[...]

    u2 = jnp.swapaxes(u_ref[...], 0, 1).reshape(lb * nb, SSM_DIM)
    ub = u2.astype(BF16)
    bu_re = jnp.dot(ub, bre_ref[...], preferred_element_type=F32)
    bu_im = jnp.dot(ub, bim_ref[...], preferred_element_type=F32)
    for c in range(tiles):
        sre[c] = bu_re[:, c * LANES:(c + 1) * LANES]
        sim[c] = bu_im[:, c * LANES:(c + 1) * LANES]
    ar = [jnp.broadcast_to(ar_ref[:, c * LANES:(c + 1) * LANES], (nb, LANES)) for c in range(tiles)]
    ai = [jnp.broadcast_to(ai_ref[:, c * LANES:(c + 1) * LANES], (nb, LANES)) for c in range(tiles)]

    def step(t, carry):
        hr, hi = carry
        rows = pl.ds(pl.multiple_of(t * nb, nb), nb)
        nr, ni = [], []
        for c in range(tiles):
            r = ar[c] * hr[c] - ai[c] * hi[c] + sre[c, rows, :]
            i = ar[c] * hi[c] + ai[c] * hr[c] + sim[c, rows, :]
            sre[c, rows, :] = r
            sim[c, rows, :] = i
            nr.append(r)
            ni.append(i)
        return tuple(nr), tuple(ni)

    init = (tuple(hr_s[:, c * LANES:(c + 1) * LANES] for c in range(tiles)),
            tuple(hi_s[:, c * LANES:(c + 1) * LANES] for c in range(tiles)))
    hr, hi = lax.fori_loop(0, lb, step, init)
    for c in range(tiles):
        hr_s[:, c * LANES:(c + 1) * LANES] = hr[c]
        hi_s[:, c * LANES:(c + 1) * LANES] = hi[c]
        hr_out[:, c * LANES:(c + 1) * LANES] = hr[c]
        hi_out[:, c * LANES:(c + 1) * LANES] = hi[c]

    h_re = jnp.concatenate([sre[c] for c in range(tiles)], axis=-1).astype(BF16)
    h_im = jnp.concatenate([sim[c] for c in range(tiles)], axis=-1).astype(BF16)
    y = (jnp.dot(h_re, cre_ref[...], preferred_element_type=F32)
         - jnp.dot(h_im, cim_ref[...], preferred_element_type=F32))
    y = y + d_ref[...] * u2
    g = jax.nn.gelu(y)
    gate = jax.nn.sigmoid(jnp.dot(g.astype(BF16), wglu_ref[...], preferred_element_type=F32) + bglu_ref[...])
    ssm_ref[...] = jnp.swapaxes((g * gate).reshape(lb, nb, SSM_DIM), 0, 1)

    tt = lax.broadcasted_iota(jnp.int32, (lb, CONV_DIM), 0)
    w0, w1, w2 = cw_ref[0:1, :], cw_ref[1:2, :], cw_ref[2:3, :]
    for b in range(nb):
        zb = z_ref[b]
        prev = zprev_s[b]
        z1 = jnp.where(tt == 0, prev[1:2], pltpu.roll(zb, 1, 0))
        z2 = jnp.where(tt == 0, prev[0:1], jnp.where(tt == 1, prev[1:2], pltpu.roll(zb, 2, 0)))
        conv_ref[b] = gb_ref[b] * (w0 * z2 + w1 * z1 + w2 * zb)
        last = zb[lb - (CONV_WIDTH - 1):lb]
        zprev_s[b] = last
        cst_out[b] = last


def _ssm_conv(u, z, gb, h0r, h0i, cprev, lw, lb):
    bsz, seq, _ = u.shape
    nb = SUBLANES
    tok = pl.BlockSpec((nb, lb, SSM_DIM), lambda g, j: (g, j, 0))
    st = pl.BlockSpec((nb, STATE_LANES), lambda g, j: (g, 0))
    cs = pl.BlockSpec((nb, CONV_WIDTH - 1, CONV_DIM), lambda g, j: (g, 0, 0))
    return pl.pallas_call(
        functools.partial(_ssm_body, lb=lb),
        grid=(bsz // nb, seq // lb),
        in_specs=[tok, tok, tok, st, st, cs,
                  _const_spec((SSM_DIM, STATE_LANES)), _const_spec((SSM_DIM, STATE_LANES)),
                  _const_spec((STATE_LANES, SSM_DIM)), _const_spec((STATE_LANES, SSM_DIM)),
                  _const_spec((1, STATE_LANES)), _const_spec((1, STATE_LANES)),
                  _const_spec((1, SSM_DIM)), _const_spec((SSM_DIM, SSM_DIM)), _const_spec((1, SSM_DIM)),
                  _const_spec((CONV_WIDTH, CONV_DIM))],
        out_specs=(tok, tok, st, st, cs),
        out_shape=(jax.ShapeDtypeStruct((bsz, seq, SSM_DIM), F32),
                   jax.ShapeDtypeStruct((bsz, seq, CONV_DIM), F32),
                   jax.ShapeDtypeStruct((bsz, STATE_LANES), F32),
                   jax.ShapeDtypeStruct((bsz, STATE_LANES), F32),
                   jax.ShapeDtypeStruct((bsz, CONV_WIDTH - 1, CONV_DIM), F32)),
        scratch_shapes=[pltpu.VMEM((STATE_LANES // LANES, nb * lb, LANES), F32),
                        pltpu.VMEM((STATE_LANES // LANES, nb * lb, LANES), F32),
                        pltpu.VMEM((nb, STATE_LANES), F32), pltpu.VMEM((nb, STATE_LANES), F32),
                        pltpu.VMEM((nb, CONV_WIDTH - 1, CONV_DIM), F32)],
        compiler_params=_params("arbitrary", "arbitrary"),
        name="ssm_conv",
    )(u, z, gb, h0r, h0i, cprev, lw["b_re_blk"], lw["b_im_blk"], lw["c_re_blk"], lw["c_im_blk"],
      lw["a_re"], lw["a_im"], lw["d"], lw["w_glu"], lw["b_glu"], lw["conv_w"])


def _rms_rows(t, w):
    return t * lax.rsqrt(jnp.mean(t * t, axis=-1, keepdims=True) + RMS_EPS) * w


def _mix_ffn_body(x_ref, att_ref, ssm_ref, conv_ref, bn_ref, wout_ref, ln2_ref, wup_ref, wdown_ref, o_ref,
                  *, ff_chunk):
    bn = bn_ref[...]
    mix = jnp.concatenate(
        [_rms_rows(att_ref[...], bn[:, 0:ATT_DIM]),
         _rms_rows(ssm_ref[...], bn[:, ATT_DIM:ATT_DIM + SSM_DIM]),
         _rms_rows(conv_ref[...], bn[:, ATT_DIM + SSM_DIM:])], axis=-1).astype(BF16)
    x1 = x_ref[...] + jnp.dot(mix, wout_ref[...], preferred_element_type=F32)
    h2 = _rms_rows(x1, ln2_ref[...]).astype(BF16)
    acc = x1
    for c in range(D_FF // ff_chunk):
        f = jnp.maximum(jnp.dot(h2, wup_ref[:, c * ff_chunk:(c + 1) * ff_chunk], preferred_element_type=F32), 0.0)
        acc = acc + jnp.dot((f * f).astype(BF16), wdown_ref[c * ff_chunk:(c + 1) * ff_chunk, :],
                            preferred_element_type=F32)
    o_ref[...] = acc


def _mix_ffn(x, att, ssm, conv, lw, tb):
    bsz, seq, _ = x.shape
    rows = bsz * seq
    tok = lambda w: pl.BlockSpec((tb, w), lambda i: (i, 0))
    once = lambda shape: pl.BlockSpec(shape, lambda i: (0, 0), pipeline_mode=pl.Buffered(1))
    out = pl.pallas_call(
        functools.partial(_mix_ffn_body, ff_chunk=1024),
        grid=(rows // tb,),
        in_specs=[tok(D_MODEL), tok(ATT_DIM), tok(SSM_DIM), tok(CONV_DIM),
                  once((1, D_MODEL)), once((D_MODEL, D_MODEL)), once((1, D_MODEL)),
                  once((D_MODEL, D_FF)), once((D_FF, D_MODEL))],
        out_specs=tok(D_MODEL),
        out_shape=jax.ShapeDtypeStruct((rows, D_MODEL), F32),
        compiler_params=_params("arbitrary"),
        name="mix_ffn",
    )(x.reshape(rows, D_MODEL), att.reshape(rows, ATT_DIM), ssm.reshape(rows, SSM_DIM),
      conv.reshape(rows, CONV_DIM), lw["bn"], lw["w_out"], lw["ln2"], lw["w_up"], lw["w_down"])
    return out.reshape(bsz, seq, D_MODEL)


def _layer_weights(l, ln1_w, w_in, b_forget, q_norm_w, k_norm_w, conv_w, ssm_lam_re, ssm_lam_im, ssm_log_dt,
                   ssm_b_re, ssm_b_im, ssm_c_re, ssm_c_im, ssm_d, w_glu, b_glu, branch_norm_w, w_out,
                   ln2_w, w_up, w_down):
    w = w_in[l]
    fg0 = 3 * ATT_DIM
    w_re = jnp.concatenate([w[:, :fg0], w[:, fg0 + N_HEADS:], w[:, fg0:fg0 + N_HEADS],
                            jnp.zeros((D_MODEL, LANES - N_HEADS), F32)], axis=1).astype(BF16)
    head_id = jnp.arange(ATT_DIM) // HEAD_DIM
    a_re, a_im, bb_re, bb_im = _s5_prep(ssm_lam_re[l], ssm_lam_im[l], ssm_log_dt[l], ssm_b_re[l], ssm_b_im[l])
    return dict(
        ln1=ln1_w[l].reshape(1, D_MODEL), w_in=w_re,
        qn=jnp.tile(q_norm_w[l], N_HEADS).reshape(1, ATT_DIM), kn=jnp.tile(k_norm_w[l], N_HEADS).reshape(1, ATT_DIM),
        bf=jnp.pad(b_forget[l], (0, LANES - N_HEADS)).reshape(1, LANES),
        mbd=(head_id[:, None] == head_id[None, :]).astype(BF16),
        a_re=a_re.reshape(1, STATE_LANES), a_im=a_im.reshape(1, STATE_LANES),
        b_re_blk=_block_diag(bb_re).astype(BF16), b_im_blk=_block_diag(bb_im).astype(BF16),
        c_re_blk=_block_diag(ssm_c_re[l]).T.astype(BF16), c_im_blk=_block_diag(ssm_c_im[l]).T.astype(BF16),
        d=ssm_d[l].reshape(1, SSM_DIM), w_glu=w_glu[l].astype(BF16), b_glu=b_glu[l].reshape(1, SSM_DIM),
        conv_w=conv_w[l], bn=branch_norm_w[l].reshape(1, D_MODEL), w_out=w_out[l].astype(BF16),
        ln2=ln2_w[l].reshape(1, D_MODEL), w_up=w_up[l].astype(BF16), w_down=w_down[l].astype(BF16))


def _run_layer(x, lw, layer, depth, prev, past, tiles):
    bsz, seq, _ = x.shape
    if past is None:
        c0 = jnp.zeros((bsz, 1, LANES), F32)
        h0r = jnp.zeros((bsz, STATE_LANES), F32)
        h0i = h0r
        cprev = jnp.zeros((bsz, CONV_WIDTH - 1, CONV_DIM), F32)
        qx, kx, vb, k_all, v_all, lft, u, z, gb = _inproj(x, c0, lw, tiles["tok"], layer, depth, prev, True)
        att = _attn_prompt(qx, kx, vb, tiles["attn"], tiles["attn_pairs"])
    else:
        cache_kt, cache_vt, clf_t, s_re, s_im, s_conv = past
        plen = cache_kt.shape[-1]
        ct_cache = _cache_cumsum(clf_t.reshape(bsz * N_HEADS, plen)).reshape(bsz, N_HEADS, plen)
        c0 = jnp.pad(ct_cache[:, :, plen - 1], ((0, 0), (0, LANES - N_HEADS))).reshape(bsz, 1, LANES)
        h0r = s_re.reshape(bsz, STATE_LANES)
        h0i = s_im.reshape(bsz, STATE_LANES)
        cprev = s_conv
        q, kb, vb, k_all, v_all, lft, cq, ct, u, z, gb = _inproj(x, c0, lw, tiles["tok"], layer, depth, prev, False)
        att = _attn_sample(q, kb, vb, cq, ct, cache_kt, cache_vt, ct_cache, layer)
    ssm, conv, hr, hi, cst = _ssm_conv(u, z, gb, h0r, h0i, cprev, lw, tiles["scan"])
    y = _mix_ffn(x, att, ssm, conv, lw, tiles["ffn"])
    small = (lft, hr.reshape(bsz, SSM_GROUPS, SSM_STATE), hi.reshape(bsz, SSM_GROUPS, SSM_STATE), cst)
    return y, (k_all, v_all), small


def _tiles(seq):
    tok = min(seq, 512)
    return dict(tok=tok, attn=min(seq, 512), attn_pairs=2, scan=min(seq, 128), ffn=tok)


def _forward(x_prompt, x_sample, cache_k, cache_v, cache_logf, state_ssm_re, state_ssm_im, state_conv, *weights):
    depth = cache_k.shape[0]
    xp, xs = x_prompt, x_sample
    bp, lp = xp.shape[0], xp.shape[1]
    tiles_p = _tiles(lp)
    tiles_s = _tiles(xs.shape[1])
    cache_kt = jnp.transpose(cache_k, (0, 1, 3, 4, 2))
    cache_vt = jnp.transpose(cache_v, (0, 1, 3, 4, 2))
    cache_lft = jnp.swapaxes(cache_logf, 2, 3)
    kv_p, kv_s, small_p, small_s = None, None, [], []
    for l in range(depth):
        lw = _layer_weights(l, *weights)
        xp, kv_p, sp = _run_layer(xp, lw, l, depth, kv_p, None, tiles_p)
        xs, kv_s, ss = _run_layer(xs, lw, l, depth, kv_s,
                                  (cache_kt, cache_vt, cache_lft[l], state_ssm_re[l], state_ssm_im[l], state_conv[l]),
                                  tiles_s)
        small_p.append(sp)
        small_s.append(ss)
    stk = lambda lst, i: jnp.stack([s[i] for s in lst])
    from_t = lambda t: jnp.transpose(t.reshape(depth, bp, N_HEADS, HEAD_DIM, lp), (0, 1, 4, 2, 3))
    return (xp, xs,
            from_t(kv_p[0]), from_t(kv_p[1]), jnp.swapaxes(stk(small_p, 0), 2, 3),
            stk(small_p, 1), stk(small_p, 2), stk(small_p, 3),
            kv_s[0], kv_s[1], jnp.swapaxes(stk(small_s, 0), 2, 3),
            stk(small_s, 1), stk(small_s, 2), stk(small_s, 3))


def kernel(x_prompt, x_sample, cache_k, cache_v, cache_logf, state_ssm_re, state_ssm_im, state_conv, ln1_w, w_in, b_forget, q_norm_w, k_norm_w, conv_w, ssm_lam_re, ssm_lam_im, ssm_log_dt, ssm_b_re, ssm_b_im, ssm_c_re, ssm_c_im, ssm_d, w_glu, b_glu, branch_norm_w, w_out, ln2_w, w_up, w_down):
    return _forward(x_prompt, x_sample, cache_k, cache_v, cache_logf, state_ssm_re, state_ssm_im, state_conv,
                    ln1_w, w_in, b_forget, q_norm_w, k_norm_w, conv_w, ssm_lam_re, ssm_lam_im, ssm_log_dt,
                    ssm_b_re, ssm_b_im, ssm_c_re, ssm_c_im, ssm_d, w_glu, b_glu, branch_norm_w, w_out,
                    ln2_w, w_up, w_down)
```

```python
import functools
import math

import jax
import jax.numpy as jnp
from jax import lax
from jax.experimental import pallas as pl
from jax.experimental.pallas import tpu as pltpu

F32 = jnp.float32
BF16 = jnp.bfloat16
HIGHEST = lax.Precision.HIGHEST

D_MODEL = 1024
N_HEADS = 8
HEAD_DIM = 64
ATT_DIM = N_HEADS * HEAD_DIM
SSM_DIM = 256
SSM_GROUP = 16
SSM_GROUPS = SSM_DIM // SSM_GROUP
SSM_STATE = 64
STATE_LANES = SSM_GROUPS * SSM_STATE
CONV_DIM = 256
CONV_WIDTH = 3
D_FF = 4 * D_MODEL
RMS_EPS = 1e-6
NEG_INF = -1e30
LOG2E = math.log2(math.e)

LANES = 128
SUBLANES = 8
HEAD_PAIRS = ATT_DIM // LANES
PAIR_EXT = 2 * LANES
AUG_STRIDE = 8
BIAS_PIECES = 3
PROJ_PAD = 3 * ATT_DIM + SSM_DIM + 3 * CONV_DIM + LANES
VMEM_LIMIT = 56 * 1024 * 1024


def _params(*sem):
    return pltpu.CompilerParams(dimension_semantics=sem, vmem_limit_bytes=VMEM_LIMIT)


def _const_spec(shape):
    zeros = (0,) * len(shape)
    return pl.BlockSpec(shape, lambda *_: zeros)


def _nt_dot(a, b, **kw):
    return lax.dot_general(a, b, (((1,), (1,)), ((), ())), preferred_element_type=F32, **kw)


def _bias_selectors():
    h = jnp.arange(N_HEADS)
    base = LANES * (h // 2) + AUG_STRIDE * (h % 2)
    sq = jnp.zeros((LANES, ATT_DIM), F32)
    sk = jnp.zeros((ATT_DIM, LANES), F32)
    for r in range(BIAS_PIECES):
        sq = sq.at[N_HEADS * r + h, base + r].set(1.0)
        sk = sk.at[base + BIAS_PIECES + r, N_HEADS * r + h].set(-1.0)
    return sq.astype(BF16), sk.astype(BF16)


def _split3(t):
    p1 = t.astype(BF16)
    r1 = t - p1.astype(F32)
    p2 = r1.astype(BF16)
    p3 = (r1 - p2.astype(F32)).astype(BF16)
    return p1, p2, p3


def _inproj_body(*refs, cum_block, prompt, n_prev):
    (x_ref, ln1_ref, w_ref, qnw_ref, knw_ref, bf_ref, mbd_ref, tri_ref, sel_ref, c0_ref, sq_ref, sk_ref) = refs[:12]
    outs = refs[12 + n_prev:]
    carry_ref = outs[-1]
    i = pl.program_id(1)
    x = x_ref[...]
    ms = jnp.mean(x * x, axis=-1, keepdims=True)
    h = (x * lax.rsqrt(ms + RMS_EPS) * ln1_ref[...]).astype(BF16)
    proj = jnp.dot(h, w_ref[...], preferred_element_type=F32)
    tb = x.shape[0]

    def head_norm(t, w):
        ss = jnp.dot((t * t).astype(BF16), mbd_ref[...], preferred_element_type=F32)
        return t * lax.rsqrt(ss * (1.0 / HEAD_DIM) + RMS_EPS) * w

    q = head_norm(proj[:, 0:ATT_DIM], qnw_ref[...]) * (LOG2E * HEAD_DIM ** -0.5)
    k = head_norm(proj[:, ATT_DIM:2 * ATT_DIM], knw_ref[...])
    v = proj[:, 2 * ATT_DIM:3 * ATT_DIM]
    o = 3 * ATT_DIM
    u = proj[:, o:o + SSM_DIM]
    hc = proj[:, o + SSM_DIM:o + SSM_DIM + CONV_DIM]
    gb = proj[:, o + SSM_DIM + CONV_DIM:o + SSM_DIM + 2 * CONV_DIM]
    gc = proj[:, o + SSM_DIM + 2 * CONV_DIM:o + SSM_DIM + 3 * CONV_DIM]
    lf = jax.nn.log_sigmoid(proj[:, PROJ_PAD - LANES:PROJ_PAD] + bf_ref[...])

    @pl.when(i == 0)
    def _():
        carry_ref[...] = c0_ref[...]

    carry = carry_ref[...]
    pieces = []
    for s in range(tb // cum_block):
        blk = jnp.concatenate(_split3(lf[s * cum_block:(s + 1) * cum_block]), axis=-1)
        r = jnp.dot(tri_ref[...], blk, preferred_element_type=F32)
        c = r[:, 0:LANES] + r[:, LANES:2 * LANES] + r[:, 2 * LANES:3 * LANES] + carry
        carry = c[cum_block - 1:cum_block]
        pieces.append(c)
    c_all = pieces[0] if len(pieces) == 1 else jnp.concatenate(pieces, axis=0)
    carry_ref[...] = carry
    c2 = c_all * LOG2E
    lf_t = _nt_dot(sel_ref[...], lf, precision=HIGHEST)

    if prompt:
        qx_ref, kx_ref, vb_ref, kt_ref, vt_ref, lft_ref, u_ref, z_ref, gb_ref = outs[:-1]
        head_lane = lax.broadcasted_iota(jnp.int32, (tb, LANES), 1) < N_HEADS
        p1, p2, p3 = (p.astype(F32) for p in _split3(jnp.where(head_lane, c2, 0.0)))
        pcs = (p1 + pltpu.roll(p2, N_HEADS, 1) + pltpu.roll(p3, 2 * N_HEADS, 1)).astype(BF16)
        lane = lax.broadcasted_iota(jnp.int32, (tb, ATT_DIM), 1) % LANES
        q_one = (lane < 2 * AUG_STRIDE) & (lane % AUG_STRIDE >= BIAS_PIECES) & (lane % AUG_STRIDE < 2 * BIAS_PIECES)
        qa = jnp.dot(pcs, sq_ref[...], preferred_element_type=F32) + q_one.astype(F32)
        row = lax.broadcasted_iota(jnp.int32, (ATT_DIM, tb), 0) % LANES
        k_one = (row < 2 * AUG_STRIDE) & (row % AUG_STRIDE < BIAS_PIECES)
        ka_t = _nt_dot(sk_ref[...], pcs) + k_one.astype(F32)
        k_t = k.T
        qb, qab, ktb, kab = q.astype(BF16), qa.astype(BF16), k_t.astype(BF16), ka_t.astype(BF16)
        qx, kx = [], []
        for p in range(HEAD_PAIRS):
            sl = slice(p * LANES, (p + 1) * LANES)
            qx += [qb[:, sl], qab[:, sl]]
            kx += [ktb[sl], kab[sl]]
        qx_ref[...] = jnp.concatenate(qx, axis=-1)
        kx_ref[...] = jnp.concatenate(kx, axis=0)
        vb_ref[...] = v.astype(BF16)
        kt_ref[...] = k_t
        vt_ref[...] = v.T
    else:
        q_ref, kb_ref, vb_ref, kf_ref, vf_ref, lft_ref, cq_ref, ct_ref, u_ref, z_ref, gb_ref = outs[:-1]
        q_ref[...] = q.astype(BF16)
        kb_ref[...] = k.astype(BF16)
        vb_ref[...] = v.astype(BF16)
        kf_ref[...] = k.reshape(tb, N_HEADS, HEAD_DIM)
        vf_ref[...] = v.reshape(tb, N_HEADS, HEAD_DIM)
        cq_ref[...] = c2[:, 0:N_HEADS]
        ct_ref[...] = _nt_dot(sel_ref[...], c2, precision=HIGHEST)
    lft_ref[...] = lf_t
    u_ref[...] = u
    z_ref[...] = gc * hc
    gb_ref[...] = gb


def _inproj(x, c0, lw, tb, layer, depth, prev, prompt):
    bsz, seq, _ = x.shape
    cum_block = min(tb, LANES)
    tri = (lax.broadcasted_iota(jnp.int32, (cum_block, cum_block), 1)
           <= lax.broadcasted_iota(jnp.int32, (cum_block, cum_block), 0)).astype(BF16)
    sel =(lax.broadcasted_iota(jnp.int32, (N_HEADS, LANES), 0)
           == lax.broadcasted_iota(jnp.int32, (N_HEADS, LANES), 1)).astype(F32)
    sq, sk = _bias_selectors()
    tok = lambda w: pl.BlockSpec((None, tb, w), lambda b, i: (b, i, 0))
    head_major = pl.BlockSpec((None, N_HEADS, tb), lambda b, i: (b, 0, i))
    tok_shape = lambda w, dt: jax.ShapeDtypeStruct((bsz, seq, w), dt)
    heads_shape = jax.ShapeDtypeStruct((bsz, N_HEADS, seq), F32)
    if prompt:
        state_shape = jax.ShapeDtypeStruct((depth, bsz, ATT_DIM, seq), F32)
        state_spec = pl.BlockSpec((None, None, ATT_DIM, tb), lambda b, i: (layer, b, 0, i))
        out_shapes = (tok_shape(HEAD_PAIRS * PAIR_EXT, BF16),
                      jax.ShapeDtypeStruct((bsz, HEAD_PAIRS * PAIR_EXT, seq), BF16),
                      tok_shape(ATT_DIM, BF16), state_shape, state_shape, heads_shape,
                      tok_shape(SSM_DIM, F32), tok_shape(CONV_DIM, F32), tok_shape(CONV_DIM, F32))
        out_specs = (tok(HEAD_PAIRS * PAIR_EXT),
                     pl.BlockSpec((None, HEAD_PAIRS * PAIR_EXT, tb), lambda b, i: (b, 0, i)),
                     tok(ATT_DIM), state_spec, state_spec, head_major,
                     tok(SSM_DIM), tok(CONV_DIM), tok(CONV_DIM))
        state_out = (3, 4)
    else:
        state_shape = jax.ShapeDtypeStruct((depth, bsz, seq, N_HEADS, HEAD_DIM), F32)
        state_spec = pl.BlockSpec((None, None, tb, N_HEADS, HEAD_DIM), lambda b, i: (layer, b, i, 0, 0))
        out_shapes = (tok_shape(ATT_DIM, BF16), tok_shape(ATT_DIM, BF16), tok_shape(ATT_DIM, BF16),
                      state_shape, state_shape, heads_shape,
                      tok_shape(N_HEADS, F32), heads_shape,
                      tok_shape(SSM_DIM, F32), tok_shape(CONV_DIM, F32), tok_shape(CONV_DIM, F32))
        out_specs = (tok(ATT_DIM), tok(ATT_DIM), tok(ATT_DIM), state_spec, state_spec, head_major,
                     tok(N_HEADS), head_major, tok(SSM_DIM), tok(CONV_DIM), tok(CONV_DIM))
        state_out = (3, 4)
    in_specs = [tok(D_MODEL), _const_spec((1, D_MODEL)), _const_spec((D_MODEL, PROJ_PAD)),
                _const_spec((1, ATT_DIM)), _const_spec((1, ATT_DIM)), _const_spec((1, LANES)),
                _const_spec((ATT_DIM, ATT_DIM)), _const_spec((cum_block, cum_block)),
                _const_spec((N_HEADS, LANES)),
                pl.BlockSpec((None, 1, LANES), lambda b, i: (b, 0, 0)),
                _const_spec(sq.shape), _const_spec(sk.shape)]
    args = [x, lw["ln1"], lw["w_in"], lw["qn"], lw["kn"], lw["bf"], lw["mbd"], tri, sel, c0, sq, sk]
    aliases = {}
    n_prev = 0
    if prev is not None:
        n_prev = len(prev)
        for n, buf in enumerate(prev):
            aliases[len(args)] = state_out[n]
            args.append(buf)
            in_specs.append(pl.BlockSpec(memory_space=pl.ANY))
    return pl.pallas_call(
        functools.partial(_inproj_body, cum_block=cum_block, prompt=prompt, n_prev=n_prev),
        grid=(bsz, seq // tb),
        in_specs=in_specs, out_specs=out_specs, out_shape=out_shapes,
        scratch_shapes=[pltpu.VMEM((1, LANES), F32)],
        input_output_aliases=aliases,
        compiler_params=_params("arbitrary", "arbitrary"),
        name="inproj_prompt" if prompt else "inproj_sample",
    )(*args)


def _attn_body(q_ref, kx_ref, v_ref, o_ref, m_s, acc_s, s_s, *, tq, pairs):
    nq = q_ref.shape[0] // tq
    lane_x = lax.broadcasted_iota(jnp.int32, (tq, PAIR_EXT), 1)
    lane = lax.broadcasted_iota(jnp.int32, (tq, LANES), 1)
    head_lanes = [(lane >= HEAD_DIM * a) & (lane < HEAD_DIM * (a + 1)) for a in range(2)]
    mine = [(((lane_x >= HEAD_DIM * a) & (lane_x < HEAD_DIM * (a + 1)))
             | ((lane_x >= LANES + AUG_STRIDE * a) & (lane_x < LANES + AUG_STRIDE * (a + 1)))) for a in range(2)]
    row = lax.broadcasted_iota(jnp.int32, (tq, tq), 0)
    col = lax.broadcasted_iota(jnp.int32, (tq, tq), 1)

    def scores(qi, j, p):
        rows = pl.ds(pl.multiple_of(qi * tq, tq), tq)
        qx = q_ref[rows, p * PAIR_EXT:(p + 1) * PAIR_EXT]
        start = pl.multiple_of(j * tq, tq)
        kx = kx_ref[p * PAIR_EXT:(p + 1) * PAIR_EXT, pl.ds(start, tq)]
        for a in range(2):
            s_s[2 * p + a] = jnp.dot(jnp.where(mine[a], qx, jnp.zeros_like(qx)), kx, preferred_element_type=F32)

    def absorb(j, p, masked):
        start = pl.multiple_of(j * tq, tq)
        v = v_ref[pl.ds(start, tq), p * LANES:(p + 1) * LANES]
        for a in range(2):
            h = 2 * p + a
            s = s_s[h]
            if masked:
                s = jnp.where(col <= row, s, NEG_INF)
            va = jnp.where(head_lanes[a], v, jnp.ones_like(v))
            m_old = m_s[h]
            m_new = jnp.maximum(m_old, jnp.broadcast_to(jnp.max(s, axis=-1, keepdims=True), (tq, LANES)))
            prob = jnp.concatenate([jnp.exp2(s[:, c * LANES:(c + 1) * LANES] - m_new)
                                    for c in range(tq // LANES)], axis=-1).astype(BF16)
            out = jnp.dot(prob, va, preferred_element_type=F32)
            acc_s[h] = jnp.exp2(m_old - m_new) * acc_s[h] + out
            m_s[h] = m_new

    def reset():
        for h in range(2 * pairs):
            m_s[h] = jnp.full((tq, LANES), NEG_INF, F32)
            acc_s[h] = jnp.zeros((tq, LANES), F32)

    def key_block(qi, j, masked, nxt):
        for p in range(pairs):
            if p + 1 < pairs:
                scores(qi, j, p + 1)
            else:
                scores(nxt[0], nxt[1], 0)
            absorb(j, p, masked)

    def query_block(qi, carry):
        def body(j, c):
            key_block(qi, j, False, (qi, j + 1))
            return c

        lax.fori_loop(0, qi, body, 0)
        key_block(qi, qi, True, (jnp.minimum(qi + 1, nq - 1), 0))
        rows = pl.ds(pl.multiple_of(qi * tq, tq), tq)
        for p in range(pairs):
            outs = [acc_s[2 * p + a] / pltpu.roll(acc_s[2 * p + a], HEAD_DIM, 1) for a in range(2)]
            o_ref[rows, p * LANES:(p + 1) * LANES] = jnp.where(head_lanes[0], outs[0], outs[1])
        reset()
        return carry

    reset()
    scores(0, 0, 0)
    lax.fori_loop(0, nq, query_block, 0)


def _attn_prompt(qx, kx, vb, tq, pairs):
    bsz, seq, _ = qx.shape
    return pl.pallas_call(
        functools.partial(_attn_body, tq=tq, pairs=pairs),
        grid=(bsz, HEAD_PAIRS // pairs),
        in_specs=[pl.BlockSpec((None, seq, pairs * PAIR_EXT), lambda b, h: (b, 0, h)),
                  pl.BlockSpec((None, pairs * PAIR_EXT, seq), lambda b, h: (b, h, 0)),
                  pl.BlockSpec((None, seq, pairs * LANES), lambda b, h: (b, 0, h))],
        out_specs=pl.BlockSpec((None, seq, pairs * LANES), lambda b, h: (b, 0, h)),
        out_shape=jax.ShapeDtypeStruct((bsz, seq, ATT_DIM), F32),
        scratch_shapes=[pltpu.VMEM((2 * pairs, tq, LANES), F32), pltpu.VMEM((2 * pairs, tq, LANES), F32),
                        pltpu.VMEM((2 * pairs, tq, tq), F32)],
        compiler_params=_params("arbitrary", "arbitrary"),
        name="attn_prompt",
    )(qx, kx, vb)


def _attn_sample_body(q_ref, ckt_ref, cvt_ref, kn_ref, vn_ref, cq_ref, ctc_ref, ctn_ref, o_ref):
    hp = pl.program_id(1)
    tq = q_ref.shape[0]
    past = ckt_ref.shape[-1]
    q = q_ref[...]
    cq8 = cq_ref[...]
    lane = lax.broadcasted_iota(jnp.int32, (tq, LANES), 1)
    lane8 = lax.broadcasted_iota(jnp.int32, (tq, N_HEADS), 1)
    row = lax.broadcasted_iota(jnp.int32, (tq, tq), 0)
    col = lax.broadcasted_iota(jnp.int32, (tq, tq), 1)
    q2 = jnp.concatenate(
        [jnp.where((lane >= HEAD_DIM * a) & (lane < HEAD_DIM * (a + 1)), q, jnp.zeros_like(q)) for a in range(2)],
        axis=0)
    kt = ckt_ref[...].reshape(LANES, past).astype(BF16)
    vt = cvt_ref[...].reshape(LANES, past).astype(BF16)
    s_cache = jnp.dot(q2, kt, preferred_element_type=F32)
    s_new = _nt_dot(q2, kn_ref[...])
    sc, sn = [], []
    for a in range(2):
        head = 2 * hp + a
        cqa = jnp.sum(jnp.where(lane8 == head, cq8, 0.0), axis=-1, keepdims=True)
        ck_cache = ctc_ref[pl.ds(head, 1), :] * LOG2E
        ck_new = ctn_ref[pl.ds(head, 1), :]
        sc.append(s_cache[a * tq:(a + 1) * tq] + cqa - ck_cache)
        sn.append(jnp.where(col <= row, s_new[a * tq:(a + 1) * tq] + cqa - ck_new, NEG_INF))
    s_cache = jnp.concatenate(sc, axis=0)
    s_new = jnp.concatenate(sn, axis=0)
    m = jnp.maximum(jnp.max(s_cache, axis=-1, keepdims=True), jnp.max(s_new, axis=-1, keepdims=True))
    p_cache = jnp.exp2(s_cache - m)
    p_new = jnp.exp2(s_new - m)
    l = jnp.sum(p_cache, axis=-1, keepdims=True) + jnp.sum(p_new, axis=-1, keepdims=True)
    o = (_nt_dot(p_cache.astype(BF16), vt)
         + jnp.dot(p_new.astype(BF16), vn_ref[...], preferred_element_type=F32)) / l
    o_ref[...] = jnp.where(lane < HEAD_DIM, o[0:tq], o[tq:2 * tq])


def _attn_sample(q, kb, vb, cq, ct_new, cache_kt, cache_vt, ct_cache, layer):
    bsz, seq, _ = q.shape
    past = cache_kt.shape[-1]
    cache_blk = pl.BlockSpec((None, None, 2, HEAD_DIM, past), lambda b, h: (layer, b, h, 0, 0))
    new_blk = pl.BlockSpec((None, seq, LANES), lambda b, h: (b, 0, h))
    return pl.pallas_call(
        _attn_sample_body,
        grid=(bsz, HEAD_PAIRS),
        in_specs=[new_blk, cache_blk, cache_blk, new_blk, new_blk,
                  pl.BlockSpec((None, seq, N_HEADS), lambda b, h: (b, 0, 0)),
                  pl.BlockSpec((None, N_HEADS, past), lambda b, h: (b, 0, 0)),
                  pl.BlockSpec((None, N_HEADS, seq), lambda b, h: (b, 0, 0))],
        out_specs=new_blk,
        out_shape=jax.ShapeDtypeStruct((bsz, seq, ATT_DIM), F32),
        compiler_params=_params("arbitrary", "arbitrary"),
        name="attn_sample",
    )(q, cache_kt, cache_vt, kb, vb, cq, ct_cache, ct_new)


def _cache_cumsum_body(x_ref, o_ref):
    rows, past = x_ref.shape
    tri = (lax.broadcasted_iota(jnp.int32, (LANES, LANES), 0)
           <= lax.broadcasted_iota(jnp.int32, (LANES, LANES), 1)).astype(F32)
    carry = jnp.zeros((rows, 1), F32)
    for b in range(past // LANES):
        blk = x_ref[:, b * LANES:(b + 1) * LANES]
        c = jnp.dot(blk, tri, precision=HIGHEST, preferred_element_type=F32) + carry
        o_ref[:, b * LANES:(b + 1) * LANES] = c
        carry = c[:, LANES - 1:LANES]


def _cache_cumsum(x):
    return pl.pallas_call(
        _cache_cumsum_body,
        out_shape=jax.ShapeDtypeStruct(x.shape, F32),
        compiler_params=pltpu.CompilerParams(vmem_limit_bytes=VMEM_LIMIT),
        name="cache_cumsum",
    )(x)


def _s5_prep_body(lr_ref, li_ref, ldt_ref, br_ref, bi_ref, ar_ref, ai_ref, bbr_ref, bbi_ref):
    lr = jnp.minimum(lr_ref[...], -1e-4)
    li = li_ref[...]
    dt = jnp.exp(ldt_ref[...])
    ldr, ldi = lr * dt, li * dt
    mag = jnp.exp(ldr)
    ar, ai = mag * jnp.cos(ldi), mag * jnp.sin(ldi)
    den = lr * lr + li * li
    nr = ar - 1.0
    qr = (nr * lr + ai * li) / den
    qi = (ai * lr - nr * li) / den
    ar_ref[...] = ar
    ai_ref[...] = ai
    br, bi = br_ref[...], bi_ref[...]
    bbr_ref[...] = qr[:, None, :] * br - qi[:, None, :] * bi
    bbi_ref[...] = qr[:, None, :] * bi + qi[:, None, :] * br


def _s5_prep(lam_re, lam_im, log_dt, b_re, b_im):
    g, p = lam_re.shape
    gp = jax.ShapeDtypeStruct((g, p), F32)
    gcp = jax.ShapeDtypeStruct((g, SSM_GROUP, p), F32)
    return pl.pallas_call(
        _s5_prep_body, out_shape=(gp, gp, gcp, gcp), name="s5_prep",
    )(lam_re, lam_im, log_dt.reshape(g, 1), jnp.swapaxes(b_re, 1, 2), jnp.swapaxes(b_im, 1, 2))


def _block_diag(t):
    g, c, p = t.shape
    eye = jnp.eye(g, dtype=t.dtype)
    return (t[:, :, None, :] * eye[:, None, :, None]).reshape(g * c, g * p)


def _ssm_body(u_ref, z_ref, gb_ref, h0r_ref, h0i_ref, cprev_ref,
              bre_ref, bim_ref, cre_ref, cim_ref, ar_ref, ai_ref, d_ref, wglu_ref, bglu_ref, cw_ref,
              ssm_ref, conv_ref, hr_out, hi_out, cst_out,
              sre, sim, hr_s, hi_s, zprev_s, *, lb):
    j = pl.program_id(1)
    nb = SUBLANES
    tiles = STATE_LANES // LANES

    @pl.when(j == 0)
    def _():
        hr_s[...] = h0r_ref[...]
        hi_s[...] = h0i_ref[...]
        zprev_s[...] = cprev_ref[...]

    u2 = jnp.swapaxes(u_ref[...], 0, 1).reshape(lb * nb, SSM_DIM)
    ub = u2.astype(BF16)
    bu_re = jnp.dot(ub, bre_ref[...], preferred_element_type=F32)
    bu_im = jnp.dot(ub, bim_ref[...], preferred_element_type=F32)
    for c in range(tiles):
        sre[c] = bu_re[:, c * LANES:(c + 1) * LANES]
        sim[c] = bu_im[:, c * LANES:(c + 1) * LANES]
    ar = [jnp.broadcast_to(ar_ref[:, c * LANES:(c + 1) * LANES], (nb, LANES)) for c in range(tiles)]
    ai = [jnp.broadcast_to(ai_ref[:, c * LANES:(c + 1) * LANES], (nb, LANES)) for c in range(tiles)]

    def step(t, carry):
        hr, hi = carry
        rows = pl.ds(pl.multiple_of(t * nb, nb), nb)
        nr, ni = [], []
        for c in range(tiles):
            r = ar[c] * hr[c] - ai[c] * hi[c] + sre[c, rows, :]
            i = ar[c] * hi[c] + ai[c] * hr[c] + sim[c, rows, :]
            sre[c, rows, :] = r
            sim[c, rows, :] = i
            nr.append(r)
            ni.append(i)
        return tuple(nr), tuple(ni)

    init = (tuple(hr_s[:, c * LANES:(c + 1) * LANES] for c in range(tiles)),
            tuple(hi_s[:, c * LANES:(c + 1) * LANES] for c in range(tiles)))
    hr, hi = lax.fori_loop(0, lb, step, init, unroll=2)
    for c in range(tiles):
        hr_s[:, c * LANES:(c + 1) * LANES] = hr[c]
        hi_s[:, c * LANES:(c + 1) * LANES] = hi[c]
        hr_out[:, c * LANES:(c + 1) * LANES] = hr[c]
        hi_out[:, c * LANES:(c + 1) * LANES] = hi[c]

    h_re = jnp.concatenate([sre[c] for c in range(tiles)], axis=-1).astype(BF16)
    h_im = jnp.concatenate([sim[c] for c in range(tiles)], axis=-1).astype(BF16)
    y = (jnp.dot(h_re, cre_ref[...], preferred_element_type=F32)
         - jnp.dot(h_im, cim_ref[...], preferred_element_type=F32))
    y = y + d_ref[...] * u2
    g = jax.nn.gelu(y)
    gate = jax.nn.sigmoid(jnp.dot(g.astype(BF16), wglu_ref[...], preferred_element_type=F32) + bglu_ref[...])
    ssm_ref[...] = jnp.swapaxes((g * gate).reshape(lb, nb, SSM_DIM), 0, 1)

    tt = lax.broadcasted_iota(jnp.int32, (lb, CONV_DIM), 0)
    w0, w1, w2 = cw_ref[0:1, :], cw_ref[1:2, :], cw_ref[2:3, :]
    for b in range(nb):
        zb = z_ref[b]
        prev = zprev_s[b]
        z1 = jnp.where(tt == 0, prev[1:2], pltpu.roll(zb, 1, 0))
        z2 = jnp.where(tt == 0, prev[0:1], jnp.where(tt == 1, prev[1:2], pltpu.roll(zb, 2, 0)))
        conv_ref[b] = gb_ref[b] * (w0 * z2 + w1 * z1 + w2 * zb)
        last = zb[lb - (CONV_WIDTH - 1):lb]
        zprev_s[b] = last
        cst_out[b] = last


def _ssm_conv(u, z, gb, h0r, h0i, cprev, lw, lb):
    bsz, seq, _ = u.shape
    nb = SUBLANES
    tok = pl.BlockSpec((nb, lb, SSM_DIM), lambda g, j: (g, j, 0))
    st = pl.BlockSpec((nb, STATE_LANES), lambda g, j: (g, 0))
    cs = pl.BlockSpec((nb, CONV_WIDTH - 1, CONV_DIM), lambda g, j: (g, 0, 0))
    return pl.pallas_call(
        functools.partial(_ssm_body, lb=lb),
        grid=(bsz // nb, seq // lb),
        in_specs=[tok, tok, tok, st, st, cs,
                  _const_spec((SSM_DIM, STATE_LANES)), _const_spec((SSM_DIM, STATE_LANES)),
                  _const_spec((STATE_LANES, SSM_DIM)), _const_spec((STATE_LANES, SSM_DIM)),
                  _const_spec((1, STATE_LANES)), _const_spec((1, STATE_LANES)),
                  _const_spec((1, SSM_DIM)), _const_spec((SSM_DIM, SSM_DIM)), _const_spec((1, SSM_DIM)),
                  _const_spec((CONV_WIDTH, CONV_DIM))],
        out_specs=(tok, tok, st, st, cs),
        out_shape=(jax.ShapeDtypeStruct((bsz, seq, SSM_DIM), F32),
                   jax.ShapeDtypeStruct((bsz, seq, CONV_DIM), F32),
                   jax.ShapeDtypeStruct((bsz, STATE_LANES), F32),
                   jax.ShapeDtypeStruct((bsz, STATE_LANES), F32),
                   jax.ShapeDtypeStruct((bsz, CONV_WIDTH - 1, CONV_DIM), F32)),
        scratch_shapes=[pltpu.VMEM((STATE_LANES // LANES, nb * lb, LANES), F32),
                        pltpu.VMEM((STATE_LANES // LANES, nb * lb, LANES), F32),
                        pltpu.VMEM((nb, STATE_LANES), F32), pltpu.VMEM((nb, STATE_LANES), F32),
                        pltpu.VMEM((nb, CONV_WIDTH - 1, CONV_DIM), F32)],
        compiler_params=_params("arbitrary", "arbitrary"),
        name="ssm_conv",
    )(u, z, gb, h0r, h0i, cprev, lw["b_re_blk"], lw["b_im_blk"], lw["c_re_blk"], lw["c_im_blk"],
      lw["a_re"], lw["a_im"], lw["d"], lw["w_glu"], lw["b_glu"], lw["conv_w"])


def _rms_rows(t, w):
    return t * lax.rsqrt(jnp.mean(t * t, axis=-1, keepdims=True) + RMS_EPS) * w


def _mix_ffn_body(x_ref, att_ref, ssm_ref, conv_ref, bn_ref, wout_ref, ln2_ref, wup_ref, wdown_ref, o_ref,
                  *, ff_chunk):
    bn = bn_ref[...]
    mix = jnp.concatenate(
        [_rms_rows(att_ref[...], bn[:, 0:ATT_DIM]),
         _rms_rows(ssm_ref[...], bn[:, ATT_DIM:ATT_DIM + SSM_DIM]),
         _rms_rows(conv_ref[...], bn[:, ATT_DIM + SSM_DIM:])], axis=-1).astype(BF16)
    x1 = x_ref[...] + jnp.dot(mix, wout_ref[...], preferred_element_type=F32)
    h2 = _rms_rows(x1, ln2_ref[...]).astype(BF16)
    acc = x1
    for c in range(D_FF // ff_chunk):
        f = jnp.maximum(jnp.dot(h2, wup_ref[:, c * ff_chunk:(c + 1) * ff_chunk], preferred_element_type=F32), 0.0)
        acc = acc + jnp.dot((f * f).astype(BF16), wdown_ref[c * ff_chunk:(c + 1) * ff_chunk, :],
                            preferred_element_type=F32)
    o_ref[...] = acc


def _mix_ffn(x, att, ssm, conv, lw, tb):
    bsz, seq, _ = x.shape
    rows = bsz * seq
    tok = lambda w: pl.BlockSpec((tb, w), lambda i: (i, 0))
    once = lambda shape: pl.BlockSpec(shape, lambda i: (0, 0), pipeline_mode=pl.Buffered(1))
    out = pl.pallas_call(
        functools.partial(_mix_ffn_body, ff_chunk=1024),
        grid=(rows // tb,),
        in_specs=[tok(D_MODEL), tok(ATT_DIM), tok(SSM_DIM), tok(CONV_DIM),
                  once((1, D_MODEL)), once((D_MODEL, D_MODEL)), once((1, D_MODEL)),
                  once((D_MODEL, D_FF)), once((D_FF, D_MODEL))],
        out_specs=tok(D_MODEL),
        out_shape=jax.ShapeDtypeStruct((rows, D_MODEL), F32),
        compiler_params=_params("arbitrary"),
        name="mix_ffn",
    )(x.reshape(rows, D_MODEL), att.reshape(rows, ATT_DIM), ssm.reshape(rows, SSM_DIM),
      conv.reshape(rows, CONV_DIM), lw["bn"], lw["w_out"], lw["ln2"], lw["w_up"], lw["w_down"])
    return out.reshape(bsz, seq, D_MODEL)


def _layer_weights(l, ln1_w, w_in, b_forget, q_norm_w, k_norm_w, conv_w, ssm_lam_re, ssm_lam_im, ssm_log_dt,
                   ssm_b_re, ssm_b_im, ssm_c_re, ssm_c_im, ssm_d, w_glu, b_glu, branch_norm_w, w_out,
                   ln2_w, w_up, w_down):
    w = w_in[l]
    fg0 = 3 * ATT_DIM
    w_re = jnp.concatenate([w[:, :fg0], w[:, fg0 + N_HEADS:], w[:, fg0:fg0 + N_HEADS],
                            jnp.zeros((D_MODEL, LANES - N_HEADS), F32)], axis=1).astype(BF16)
    head_id = jnp.arange(ATT_DIM) // HEAD_DIM
    a_re, a_im, bb_re, bb_im = _s5_prep(ssm_lam_re[l], ssm_lam_im[l], ssm_log_dt[l], ssm_b_re[l], ssm_b_im[l])
    return dict(
        ln1=ln1_w[l].reshape(1, D_MODEL), w_in=w_re,
        qn=jnp.tile(q_norm_w[l], N_HEADS).reshape(1, ATT_DIM), kn=jnp.tile(k_norm_w[l], N_HEADS).reshape(1, ATT_DIM),
        bf=jnp.pad(b_forget[l], (0, LANES - N_HEADS)).reshape(1, LANES),
        mbd=(head_id[:, None] == head_id[None, :]).astype(BF16),
        a_re=a_re.reshape(1, STATE_LANES), a_im=a_im.reshape(1, STATE_LANES),
        b_re_blk=_block_diag(bb_re).astype(BF16), b_im_blk=_block_diag(bb_im).astype(BF16),
        c_re_blk=_block_diag(ssm_c_re[l]).T.astype(BF16), c_im_blk=_block_diag(ssm_c_im[l]).T.astype(BF16),
        d=ssm_d[l].reshape(1, SSM_DIM), w_glu=w_glu[l].astype(BF16), b_glu=b_glu[l].reshape(1, SSM_DIM),
        conv_w=conv_w[l], bn=branch_norm_w[l].reshape(1, D_MODEL), w_out=w_out[l].astype(BF16),
        ln2=ln2_w[l].reshape(1, D_MODEL), w_up=w_up[l].astype(BF16), w_down=w_down[l].astype(BF16))


def _run_layer(x, lw, layer, depth, prev, past, tiles):
    bsz, seq, _ = x.shape
    if past is None:
        c0 = jnp.zeros((bsz, 1, LANES), F32)
        h0r = jnp.zeros((bsz, STATE_LANES), F32)
        h0i = h0r
        cprev = jnp.zeros((bsz, CONV_WIDTH - 1, CONV_DIM), F32)
        qx, kx, vb, k_all, v_all, lft, u, z, gb = _inproj(x, c0, lw, tiles["tok"], layer, depth, prev, True)
        att = _attn_prompt(qx, kx, vb, tiles["attn"], tiles["attn_pairs"])
    else:
        cache_kt, cache_vt, clf_t, s_re, s_im, s_conv = past
        plen = cache_kt.shape[-1]
        ct_cache = _cache_cumsum(clf_t.reshape(bsz * N_HEADS, plen)).reshape(bsz, N_HEADS, plen)
        c0 = jnp.pad(ct_cache[:, :, plen - 1], ((0, 0), (0, LANES - N_HEADS))).reshape(bsz, 1, LANES)
        h0r = s_re.reshape(bsz, STATE_LANES)
        h0i = s_im.reshape(bsz, STATE_LANES)
        cprev = s_conv
        q, kb, vb, k_all, v_all, lft, cq, ct, u, z, gb = _inproj(x, c0, lw, tiles["tok"], layer, depth, prev, False)
        att = _attn_sample(q, kb, vb, cq, ct, cache_kt, cache_vt, ct_cache, layer)
    ssm, conv, hr, hi, cst = _ssm_conv(u, z, gb, h0r, h0i, cprev, lw, tiles["scan"])
    y = _mix_ffn(x, att, ssm, conv, lw, tiles["ffn"])
    small = (lft, hr.reshape(bsz, SSM_GROUPS, SSM_STATE), hi.reshape(bsz, SSM_GROUPS, SSM_STATE), cst)
    return y, (k_all, v_all), small


def _tiles(bsz, seq):
    tok = min(seq, 512)
    return dict(tok=tok, attn=min(seq, 512), attn_pairs=2, scan=min(seq, 128), ffn=min(bsz * seq, 512))


def _forward(x_prompt, x_sample, cache_k, cache_v, cache_logf, state_ssm_re, state_ssm_im, state_conv, *weights):
    depth = cache_k.shape[0]
    xp, xs = x_prompt, x_sample
    bp, lp = xp.shape[0], xp.shape[1]
    tiles_p = _tiles(bp, lp)
    tiles_s = _tiles(xs.shape[0], xs.shape[1])
    cache_kt = jnp.transpose(cache_k, (0, 1, 3, 4, 2))
    cache_vt = jnp.transpose(cache_v, (0, 1, 3, 4, 2))
    cache_lft = jnp.swapaxes(cache_logf, 2, 3)
    kv_p, kv_s, small_p, small_s = None, None, [], []
    for l in range(depth):
        lw = _layer_weights(l, *weights)
        xp, kv_p, sp = _run_layer(xp, lw, l, depth, kv_p, None, tiles_p)
        xs, kv_s, ss = _run_layer(xs, lw, l, depth, kv_s,
                                  (cache_kt, cache_vt, cache_lft[l], state_ssm_re[l], state_ssm_im[l], state_conv[l]),
                                  tiles_s)
        small_p.append(sp)
        small_s.append(ss)
    stk = lambda lst, i: jnp.stack([s[i] for s in lst])
    from_t = lambda t: jnp.transpose(t.reshape(depth, bp, N_HEADS, HEAD_DIM, lp), (0, 1, 4, 2, 3))
    return (xp, xs,
            from_t(kv_p[0]), from_t(kv_p[1]), jnp.swapaxes(stk(small_p, 0), 2, 3),
            stk(small_p, 1), stk(small_p, 2), stk(small_p, 3),
            kv_s[0], kv_s[1], jnp.swapaxes(stk(small_s, 0), 2, 3),
            stk(small_s, 1), stk(small_s, 2), stk(small_s, 3))


def kernel(x_prompt, x_sample, cache_k, cache_v, cache_logf, state_ssm_re, state_ssm_im, state_conv, ln1_w, w_in, b_forget, q_norm_w, k_norm_w, conv_w, ssm_lam_re, ssm_lam_im, ssm_log_dt, ssm_b_re, ssm_b_im, ssm_c_re, ssm_c_im, ssm_d, w_glu, b_glu, branch_norm_w, w_out, ln2_w, w_up, w_down):
    return _forward(x_prompt, x_sample, cache_k, cache_v, cache_logf, state_ssm_re, state_ssm_im, state_conv,
                    ln1_w, w_in, b_forget, q_norm_w, k_norm_w, conv_w, ssm_lam_re, ssm_lam_im, ssm_log_dt,
                    ssm_b_re, ssm_b_im, ssm_c_re, ssm_c_im, ssm_d, w_glu, b_glu, branch_norm_w, w_out,
                    ln2_w, w_up, w_down)
```

```python
import functools
import math

import jax
import jax.numpy as jnp
from jax import lax
from jax.experimental import pallas as pl
from jax.experimental.pallas import tpu as pltpu

F32 = jnp.float32
BF16 = jnp.bfloat16
HIGHEST = lax.Precision.HIGHEST

D_MODEL = 1024
N_HEADS = 8
HEAD_DIM = 64
ATT_DIM = N_HEADS * HEAD_DIM
SSM_DIM = 256
SSM_GROUP = 16
SSM_GROUPS = SSM_DIM // SSM_GROUP
SSM_STATE = 64
STATE_LANES = SSM_GROUPS * SSM_STATE
CONV_DIM = 256
CONV_WIDTH = 3
D_FF = 4 * D_MODEL
RMS_EPS = 1e-6
NEG_INF = -1e30
LOG2E = math.log2(math.e)

LANES = 128
SUBLANES = 8
HEAD_PAIRS = ATT_DIM // LANES
PAIR_EXT = 2 * LANES
AUG_STRIDE = 8
BIAS_PIECES = 3
PROJ_PAD = 3 * ATT_DIM + SSM_DIM + 3 * CONV_DIM + LANES
VMEM_LIMIT = 56 * 1024 * 1024


def _params(*sem):
    return pltpu.CompilerParams(dimension_semantics=sem, vmem_limit_bytes=VMEM_LIMIT)


def _const_spec(shape):
    zeros = (0,) * len(shape)
    return pl.BlockSpec(shape, lambda *_: zeros)


def _nt_dot(a, b, **kw):
    return lax.dot_general(a, b, (((1,), (1,)), ((), ())), preferred_element_type=F32, **kw)


def _bias_selectors():
    h = jnp.arange(N_HEADS)
    base = LANES * (h // 2) + AUG_STRIDE * (h % 2)
    sq = jnp.zeros((ATT_DIM, LANES), F32)
    sk = jnp.zeros((LANES, ATT_DIM), F32)
    for r in range(BIAS_PIECES):
        sq = sq.at[base + r, N_HEADS * r + h].set(1.0)
        sk = sk.at[N_HEADS * r + h, base + BIAS_PIECES + r].set(-1.0)
    return sq.astype(BF16), sk.astype(BF16)


def _split3(t):
    p1 = t.astype(BF16)
    r1 = t - p1.astype(F32)
    p2 = r1.astype(BF16)
    p3 = (r1 - p2.astype(F32)).astype(BF16)
    return p1, p2, p3


def _inproj_body(*refs, cum_block, prompt, n_prev):
    (x_ref, ln1_ref, w_ref, qnw_ref, knw_ref, bf_ref, mbd_ref, tri_ref, sel_ref, c0_ref, sq_ref, sk_ref) = refs[:12]
    outs = refs[12 + n_prev:]
    carry_ref = outs[-1]
    i = pl.program_id(1)
    x = x_ref[...]
    ms = jnp.mean(x * x, axis=-1, keepdims=True)
    h = (x * lax.rsqrt(ms + RMS_EPS) * ln1_ref[...]).astype(BF16)
    proj = jnp.dot(h, w_ref[...], preferred_element_type=F32)
    tb = x.shape[0]

    def head_norm(t, w):
        ss = jnp.dot((t * t).astype(BF16), mbd_ref[...], preferred_element_type=F32)
        return t * lax.rsqrt(ss * (1.0 / HEAD_DIM) + RMS_EPS) * w

    q = head_norm(proj[:, 0:ATT_DIM], qnw_ref[...]) * (LOG2E * HEAD_DIM ** -0.5)
    k = head_norm(proj[:, ATT_DIM:2 * ATT_DIM], knw_ref[...])
    v = proj[:, 2 * ATT_DIM:3 * ATT_DIM]
    o = 3 * ATT_DIM
    u = proj[:, o:o + SSM_DIM]
    hc = proj[:, o + SSM_DIM:o + SSM_DIM + CONV_DIM]
    gb = proj[:, o + SSM_DIM + CONV_DIM:o + SSM_DIM + 2 * CONV_DIM]
    gc = proj[:, o + SSM_DIM + 2 * CONV_DIM:o + SSM_DIM + 3 * CONV_DIM]
    lf = jax.nn.log_sigmoid(proj[:, PROJ_PAD - LANES:PROJ_PAD] + bf_ref[...])

    @pl.when(i == 0)
    def _():
        carry_ref[...] = c0_ref[...]

    carry = carry_ref[...]
    pieces = []
    for s in range(tb // cum_block):
        blk = jnp.concatenate(_split3(lf[s * cum_block:(s + 1) * cum_block]), axis=-1)
        r = jnp.dot(tri_ref[...], blk, preferred_element_type=F32)
        c = r[:, 0:LANES] + r[:, LANES:2 * LANES] + r[:, 2 * LANES:3 * LANES] + carry
        carry = c[cum_block - 1:cum_block]
        pieces.append(c)
    c_all = pieces[0] if len(pieces) == 1 else jnp.concatenate(pieces, axis=0)
    carry_ref[...] = carry
    c2 = c_all * LOG2E
    lf_t = _nt_dot(sel_ref[...], lf, precision=HIGHEST)

    if prompt:
        qx_ref, kx_ref, vb_ref, kt_ref, vt_ref, lft_ref, u_ref, z_ref, gb_ref = outs[:-1]
        head_lane = lax.broadcasted_iota(jnp.int32, (tb, LANES), 1) < N_HEADS
        p1, p2, p3 = (p.astype(F32) for p in _split3(jnp.where(head_lane, c2, 0.0)))
        pcs = (p1 + pltpu.roll(p2, N_HEADS, 1) + pltpu.roll(p3, 2 * N_HEADS, 1)).astype(BF16)
        row = lax.broadcasted_iota(jnp.int32, (ATT_DIM, tb), 0) % LANES
        q_one = (row < 2 * AUG_STRIDE) & (row % AUG_STRIDE >= BIAS_PIECES) & (row % AUG_STRIDE < 2 * BIAS_PIECES)
        qa_t = _nt_dot(sq_ref[...], pcs) + q_one.astype(F32)
        lane = lax.broadcasted_iota(jnp.int32, (tb, ATT_DIM), 1) % LANES
        k_one = (lane < 2 * AUG_STRIDE) & (lane % AUG_STRIDE < BIAS_PIECES)
        ka = jnp.dot(pcs, sk_ref[...], preferred_element_type=F32) + k_one.astype(F32)
        k_t, v_t = k.T, v.T
        qtb, qatb, kb, kab = q.T.astype(BF16), qa_t.astype(BF16), k.astype(BF16), ka.astype(BF16)
        qx, kx = [], []
        for p in range(HEAD_PAIRS):
            sl = slice(p * LANES, (p + 1) * LANES)
            qx += [qtb[sl], qatb[sl]]
            kx += [kb[:, sl], kab[:, sl]]
        qx_ref[...] = jnp.concatenate(qx, axis=0)
        kx_ref[...] = jnp.concatenate(kx, axis=-1)
        vb_ref[...] = v_t.astype(BF16)
        kt_ref[...] = k_t
        vt_ref[...] = v_t
    else:
        q_ref, kb_ref, vb_ref, kf_ref, vf_ref, lft_ref, cq_ref, ct_ref, u_ref, z_ref, gb_ref = outs[:-1]
        q_ref[...] = q.astype(BF16)
        kb_ref[...] = k.astype(BF16)
        vb_ref[...] = v.astype(BF16)
        kf_ref[...] = k.reshape(tb, N_HEADS, HEAD_DIM)
        vf_ref[...] = v.reshape(tb, N_HEADS, HEAD_DIM)
        cq_ref[...] = c2[:, 0:N_HEADS]
        ct_ref[...] = _nt_dot(sel_ref[...], c2, precision=HIGHEST)
    lft_ref[...] = lf_t
    u_ref[...] = u
    z_ref[...] = gc * hc
    gb_ref[...] = gb


def _inproj(x, c0, lw, tb, layer, depth, prev, prompt):
    bsz, seq, _ = x.shape
    cum_block = min(tb, LANES)
    tri = (lax.broadcasted_iota(jnp.int32, (cum_block, cum_block), 1)
           <= lax.broadcasted_iota(jnp.int32, (cum_block, cum_block), 0)).astype(BF16)
    sel =(lax.broadcasted_iota(jnp.int32, (N_HEADS, LANES), 0)
           == lax.broadcasted_iota(jnp.int32, (N_HEADS, LANES), 1)).astype(F32)
    sq, sk = _bias_selectors()
    tok = lambda w: pl.BlockSpec((None, tb, w), lambda b, i: (b, i, 0))
    head_major = pl.BlockSpec((None, N_HEADS, tb), lambda b, i: (b, 0, i))
    tok_shape = lambda w, dt: jax.ShapeDtypeStruct((bsz, seq, w), dt)
    heads_shape = jax.ShapeDtypeStruct((bsz, N_HEADS, seq), F32)
    if prompt:
        state_shape = jax.ShapeDtypeStruct((depth, bsz, ATT_DIM, seq), F32)
        state_spec = pl.BlockSpec((None, None, ATT_DIM, tb), lambda b, i: (layer, b, 0, i))
        rows_major = lambda r: pl.BlockSpec((None, r, tb), lambda b, i: (b, 0, i))
        out_shapes = (jax.ShapeDtypeStruct((bsz, HEAD_PAIRS * PAIR_EXT, seq), BF16),
                      tok_shape(HEAD_PAIRS * PAIR_EXT, BF16),
                      jax.ShapeDtypeStruct((bsz, ATT_DIM, seq), BF16),
                      state_shape, state_shape, heads_shape,
                      tok_shape(SSM_DIM, F32), tok_shape(CONV_DIM, F32), tok_shape(CONV_DIM, F32))
        out_specs = (rows_major(HEAD_PAIRS * PAIR_EXT), tok(HEAD_PAIRS * PAIR_EXT), rows_major(ATT_DIM),
                     state_spec, state_spec, head_major,
                     tok(SSM_DIM), tok(CONV_DIM), tok(CONV_DIM))
        state_out = (3, 4)
    else:
        state_shape = jax.ShapeDtypeStruct((depth, bsz, seq, N_HEADS, HEAD_DIM), F32)
        state_spec = pl.BlockSpec((None, None, tb, N_HEADS, HEAD_DIM), lambda b, i: (layer, b, i, 0, 0))
        out_shapes = (tok_shape(ATT_DIM, BF16), tok_shape(ATT_DIM, BF16), tok_shape(ATT_DIM, BF16),
                      state_shape, state_shape, heads_shape,
                      tok_shape(N_HEADS, F32), heads_shape,
                      tok_shape(SSM_DIM, F32), tok_shape(CONV_DIM, F32), tok_shape(CONV_DIM, F32))
        out_specs = (tok(ATT_DIM), tok(ATT_DIM), tok(ATT_DIM), state_spec, state_spec, head_major,
                     tok(N_HEADS), head_major, tok(SSM_DIM), tok(CONV_DIM), tok(CONV_DIM))
        state_out = (3, 4)
    in_specs = [tok(D_MODEL), _const_spec((1, D_MODEL)), _const_spec((D_MODEL, PROJ_PAD)),
                _const_spec((1, ATT_DIM)), _const_spec((1, ATT_DIM)), _const_spec((1, LANES)),
                _const_spec((ATT_DIM, ATT_DIM)), _const_spec((cum_block, cum_block)),
                _const_spec((N_HEADS, LANES)),
                pl.BlockSpec((None, 1, LANES), lambda b, i: (b, 0, 0)),
                _const_spec(sq.shape), _const_spec(sk.shape)]
    args = [x, lw["ln1"], lw["w_in"], lw["qn"], lw["kn"], lw["bf"], lw["mbd"], tri, sel, c0, sq, sk]
    aliases = {}
    n_prev = 0
    if prev is not None:
        n_prev = len(prev)
        for n, buf in enumerate(prev):
            aliases[len(args)] = state_out[n]
            args.append(buf)
            in_specs.append(pl.BlockSpec(memory_space=pl.ANY))
    return pl.pallas_call(
        functools.partial(_inproj_body, cum_block=cum_block, prompt=prompt, n_prev=n_prev),
        grid=(bsz, seq // tb),
        in_specs=in_specs, out_specs=out_specs, out_shape=out_shapes,
        scratch_shapes=[pltpu.VMEM((1, LANES), F32)],
        input_output_aliases=aliases,
        compiler_params=_params("arbitrary", "arbitrary"),
        name="inproj_prompt" if prompt else "inproj_sample",
    )(*args)


def _attn_body(qx_ref, kx_ref, vt_ref, o_ref, m_s, acc_s, s_s, *, tq, pairs):
    nq = o_ref.shape[0] // tq
    row_x = lax.broadcasted_iota(jnp.int32, (PAIR_EXT, tq), 0)
    row_v = lax.broadcasted_iota(jnp.int32, (LANES, tq), 0)
    head_rows = [(row_v >= HEAD_DIM * a) & (row_v < HEAD_DIM * (a + 1)) for a in range(2)]
    mine = [(((row_x >= HEAD_DIM * a) & (row_x < HEAD_DIM * (a + 1)))
             | ((row_x >= LANES + AUG_STRIDE * a) & (row_x < LANES + AUG_STRIDE * (a + 1)))) for a in range(2)]
    key = lax.broadcasted_iota(jnp.int32, (tq, tq), 0)
    qry = lax.broadcasted_iota(jnp.int32, (tq, tq), 1)

    def scores(qi, j, p):
        cols = pl.ds(pl.multiple_of(qi * tq, tq), tq)
        qt = qx_ref[p * PAIR_EXT:(p + 1) * PAIR_EXT, cols]
        rows = pl.ds(pl.multiple_of(j * tq, tq), tq)
        kx = kx_ref[rows, p * PAIR_EXT:(p + 1) * PAIR_EXT]
        for a in range(2):
            s_s[2 * p + a] = jnp.dot(kx, jnp.where(mine[a], qt, jnp.zeros_like(qt)), preferred_element_type=F32)

    def absorb(j, p, masked):
        cols = pl.ds(pl.multiple_of(j * tq, tq), tq)
        vt = vt_ref[p * LANES:(p + 1) * LANES, cols]
        for a in range(2):
            h = 2 * p + a
            s = s_s[h]
            if masked:
                s = jnp.where(key <= qry, s, NEG_INF)
            va = jnp.where(head_rows[a], vt, jnp.ones_like(vt))
            m_old = m_s[h]
            m_new = jnp.maximum(m_old, jnp.max(s, axis=0, keepdims=True))
            prob = jnp.exp2(s - m_new).astype(BF16)
            out = jnp.dot(va, prob, preferred_element_type=F32)
            acc_s[h] = jnp.exp2(m_old - m_new) * acc_s[h] + out
            m_s[h] = m_new

    def reset():
        for h in range(2 * pairs):
            m_s[h] = jnp.full((1, tq), NEG_INF, F32)
            acc_s[h] = jnp.zeros((LANES, tq), F32)

    def key_block(qi, j, masked, nxt):
        for p in range(pairs):
            if p + 1 < pairs:
                scores(qi, j, p + 1)
            else:
                scores(nxt[0], nxt[1], 0)
            absorb(j, p, masked)

    def query_block(qi, carry):
        def body(j, c):
            key_block(qi, j, False, (qi, j + 1))
            return c

        lax.fori_loop(0, qi, body, 0)
        key_block(qi, qi, True, (jnp.minimum(qi + 1, nq - 1), 0))
        rows = pl.ds(pl.multiple_of(qi * tq, tq), tq)
        for p in range(pairs):
            a0, a1 = acc_s[2 * p], acc_s[2 * p + 1]
            out_t = jnp.concatenate([a0[0:HEAD_DIM] / a0[HEAD_DIM:LANES],
                                     a1[HEAD_DIM:LANES] / a1[0:HEAD_DIM]], axis=0)
            o_ref[rows, p * LANES:(p + 1) * LANES] = out_t.T
        reset()
        return carry

    reset()
    scores(0, 0, 0)
    lax.fori_loop(0, nq, query_block, 0)


def _attn_prompt(qx_t, kx, vt, tq, pairs):
    bsz, seq, _ = kx.shape
    return pl.pallas_call(
        functools.partial(_attn_body, tq=tq, pairs=pairs),
        grid=(bsz, HEAD_PAIRS // pairs),
        in_specs=[pl.BlockSpec((None, pairs * PAIR_EXT, seq), lambda b, h: (b, h, 0)),
                  pl.BlockSpec((None, seq, pairs * PAIR_EXT), lambda b, h: (b, 0, h)),
                  pl.BlockSpec((None, pairs * LANES, seq), lambda b, h: (b, h, 0))],
        out_specs=pl.BlockSpec((None, seq, pairs * LANES), lambda b, h: (b, 0, h)),
        out_shape=jax.ShapeDtypeStruct((bsz, seq, ATT_DIM), F32),
        scratch_shapes=[pltpu.VMEM((2 * pairs, 1, tq), F32), pltpu.VMEM((2 * pairs, LANES, tq), F32),
                        pltpu.VMEM((2 * pairs, tq, tq), F32)],
        compiler_params=_params("arbitrary", "arbitrary"),
        name="attn_prompt",
    )(qx_t, kx, vt)


def _attn_sample_body(q_ref, ckt_ref, cvt_ref, kn_ref, vn_ref, cq_ref, ctc_ref, ctn_ref, o_ref, *, pairs):
    tq = q_ref.shape[0]
    past = ckt_ref.shape[-1]
    cq8 = cq_ref[...]
    lane = lax.broadcasted_iota(jnp.int32, (tq, LANES), 1)
    lane8 = lax.broadcasted_iota(jnp.int32, (tq, N_HEADS), 1)
    row = lax.broadcasted_iota(jnp.int32, (tq, tq), 0)
    col = lax.broadcasted_iota(jnp.int32, (tq, tq), 1)
    for pp in range(pairs):
        hp = pl.program_id(1) * pairs + pp
        lanes = slice(pp * LANES, (pp + 1) * LANES)
        q = q_ref[:, lanes]
        q2 = jnp.concatenate(
            [jnp.where((lane >= HEAD_DIM * a) & (lane < HEAD_DIM * (a + 1)), q, jnp.zeros_like(q)) for a in range(2)],
            axis=0)
        kt = ckt_ref[2 * pp:2 * pp + 2].reshape(LANES, past).astype(BF16)
        vt = cvt_ref[2 * pp:2 * pp + 2].reshape(LANES, past).astype(BF16)
        s_cache = jnp.dot(q2, kt, preferred_element_type=F32)
        s_new = _nt_dot(q2, kn_ref[:, lanes])
        sc, sn = [], []
        for a in range(2):
            head = 2 * hp + a
            cqa = jnp.sum(jnp.where(lane8 == head, cq8, 0.0), axis=-1, keepdims=True)
            ck_cache = ctc_ref[pl.ds(head, 1), :] * LOG2E
            ck_new = ctn_ref[pl.ds(head, 1), :]
            sc.append(s_cache[a * tq:(a + 1) * tq] + cqa - ck_cache)
            sn.append(jnp.where(col <= row, s_new[a * tq:(a + 1) * tq] + cqa - ck_new, NEG_INF))
        s_cache = jnp.concatenate(sc, axis=0)
        s_new = jnp.concatenate(sn, axis=0)
        m = jnp.maximum(jnp.max(s_cache, axis=-1, keepdims=True), jnp.max(s_new, axis=-1, keepdims=True))
        p_cache = jnp.exp2(s_cache - m)
        p_new = jnp.exp2(s_new - m)
        l = jnp.sum(p_cache, axis=-1, keepdims=True) + jnp.sum(p_new, axis=-1, keepdims=True)
        o = (_nt_dot(p_cache.astype(BF16), vt)
             + jnp.dot(p_new.astype(BF16), vn_ref[:, lanes], preferred_element_type=F32)) / l
        o_ref[:, lanes] = jnp.where(lane < HEAD_DIM, o[0:tq], o[tq:2 * tq])


def _attn_sample(q, kb, vb, cq, ct_new, cache_kt, cache_vt, ct_cache, layer, pairs=2):
    bsz, seq, _ = q.shape
    past = cache_kt.shape[-1]
    cache_blk = pl.BlockSpec((None, None, 2 * pairs, HEAD_DIM, past), lambda b, h: (layer, b, h, 0, 0))
    new_blk = pl.BlockSpec((None, seq, pairs * LANES), lambda b, h: (b, 0, h))
    return pl.pallas_call(
        functools.partial(_attn_sample_body, pairs=pairs),
        grid=(bsz, HEAD_PAIRS // pairs),
        in_specs=[new_blk, cache_blk, cache_blk, new_blk, new_blk,
                  pl.BlockSpec((None, seq, N_HEADS), lambda b, h: (b, 0, 0)),
                  pl.BlockSpec((None, N_HEADS, past), lambda b, h: (b, 0, 0)),
                  pl.BlockSpec((None, N_HEADS, seq), lambda b, h: (b, 0, 0))],
        out_specs=new_blk,
        out_shape=jax.ShapeDtypeStruct((bsz, seq, ATT_DIM), F32),
        compiler_params=_params("arbitrary", "arbitrary"),
        name="attn_sample",
    )(q, cache_kt, cache_vt, kb, vb, cq, ct_cache, ct_new)


def _cache_cumsum_body(x_ref, o_ref):
    rows, past = x_ref.shape
    tri = (lax.broadcasted_iota(jnp.int32, (LANES, LANES), 0)
           <= lax.broadcasted_iota(jnp.int32, (LANES, LANES), 1)).astype(F32)
    carry = jnp.zeros((rows, 1), F32)
    for b in range(past // LANES):
        blk = x_ref[:, b * LANES:(b + 1) * LANES]
        c = jnp.dot(blk, tri, precision=HIGHEST, preferred_element_type=F32) + carry
        o_ref[:, b * LANES:(b + 1) * LANES] = c
        carry = c[:, LANES - 1:LANES]


def _cache_cumsum(x):
    return pl.pallas_call(
        _cache_cumsum_body,
        out_shape=jax.ShapeDtypeStruct(x.shape, F32),
        compiler_params=pltpu.CompilerParams(vmem_limit_bytes=VMEM_LIMIT),
        name="cache_cumsum",
    )(x)


def _s5_prep_body(lr_ref, li_ref, ldt_ref, br_ref, bi_ref, ar_ref, ai_ref, bbr_ref, bbi_ref):
    lr = jnp.minimum(lr_ref[...], -1e-4)
    li = li_ref[...]
    dt = jnp.exp(ldt_ref[...])
    ldr, ldi = lr * dt, li * dt
    mag = jnp.exp(ldr)
    ar, ai = mag * jnp.cos(ldi), mag * jnp.sin(ldi)
    den = lr * lr + li * li
    nr = ar - 1.0
    qr = (nr * lr + ai * li) / den
    qi = (ai * lr - nr * li) / den
    ar_ref[...] = ar
    ai_ref[...] = ai
    br, bi = br_ref[...], bi_ref[...]
    bbr_ref[...] = qr[:, None, :] * br - qi[:, None, :] * bi
    bbi_ref[...] = qr[:, None, :] * bi + qi[:, None, :] * br


def _s5_prep(lam_re, lam_im, log_dt, b_re, b_im):
    g, p = lam_re.shape
    gp = jax.ShapeDtypeStruct((g, p), F32)
    gcp = jax.ShapeDtypeStruct((g, SSM_GROUP, p), F32)
    return pl.pallas_call(
        _s5_prep_body, out_shape=(gp, gp, gcp, gcp), name="s5_prep",
    )(lam_re, lam_im, log_dt.reshape(g, 1), jnp.swapaxes(b_re, 1, 2), jnp.swapaxes(b_im, 1, 2))


def _block_diag(t):
    g, c, p = t.shape
    eye = jnp.eye(g, dtype=t.dtype)
    return (t[:, :, None, :] * eye[:, None, :, None]).reshape(g * c, g * p)


def _ssm_body(u_ref, z_ref, gb_ref, h0r_ref, h0i_ref, cprev_ref,
              bre_ref, bim_ref, cre_ref, cim_ref, ar_ref, ai_ref, d_ref, wglu_ref, bglu_ref, cw_ref,
              ssm_ref, conv_ref, hr_out, hi_out, cst_out,
              sre, sim, hr_s, hi_s, zprev_s, *, lb):
    j = pl.program_id(1)
    nb = SUBLANES
    tiles = STATE_LANES // LANES

    @pl.when(j == 0)
    def _():
        hr_s[...] = h0r_ref[...]
        hi_s[...] = h0i_ref[...]
        zprev_s[...] = cprev_ref[...]

    u2 = jnp.swapaxes(u_ref[...], 0, 1).reshape(lb * nb, SSM_DIM)
    ub = u2.astype(BF16)
    bu_re = jnp.dot(ub, bre_ref[...], preferred_element_type=F32)
    bu_im = jnp.dot(ub, bim_ref[...], preferred_element_type=F32)
    for c in range(tiles):
        sre[c] = bu_re[:, c * LANES:(c + 1) * LANES]
        sim[c] = bu_im[:, c * LANES:(c + 1) * LANES]
    ar = [jnp.broadcast_to(ar_ref[:, c * LANES:(c + 1) * LANES], (nb, LANES)) for c in range(tiles)]
    ai = [jnp.broadcast_to(ai_ref[:, c * LANES:(c + 1) * LANES], (nb, LANES)) for c in range(tiles)]

    def step(t, carry):
        hr, hi = carry
        rows = pl.ds(pl.multiple_of(t * nb, nb), nb)
        nr, ni = [], []
        for c in range(tiles):
            r = ar[c] * hr[c] - ai[c] * hi[c] + sre[c, rows, :]
            i = ar[c] * hi[c] + ai[c] * hr[c] + sim[c, rows, :]
            sre[c, rows, :] = r
            sim[c, rows, :] = i
            nr.append(r)
            ni.append(i)
        return tuple(nr), tuple(ni)

    init = (tuple(hr_s[:, c * LANES:(c + 1) * LANES] for c in range(tiles)),
            tuple(hi_s[:, c * LANES:(c + 1) * LANES] for c in range(tiles)))
    hr, hi = lax.fori_loop(0, lb, step, init, unroll=2)
    for c in range(tiles):
        hr_s[:, c * LANES:(c + 1) * LANES] = hr[c]
        hi_s[:, c * LANES:(c + 1) * LANES] = hi[c]
        hr_out[:, c * LANES:(c + 1) * LANES] = hr[c]
        hi_out[:, c * LANES:(c + 1) * LANES] = hi[c]

    h_re = jnp.concatenate([sre[c] for c in range(tiles)], axis=-1).astype(BF16)
    h_im = jnp.concatenate([sim[c] for c in range(tiles)], axis=-1).astype(BF16)
    y = (jnp.dot(h_re, cre_ref[...], preferred_element_type=F32)
         - jnp.dot(h_im, cim_ref[...], preferred_element_type=F32))
    y = y + d_ref[...] * u2
    g = jax.nn.gelu(y)
    gate = jax.nn.sigmoid(jnp.dot(g.astype(BF16), wglu_ref[...], preferred_element_type=F32) + bglu_ref[...])
    ssm_ref[...] = jnp.swapaxes((g * gate).reshape(lb, nb, SSM_DIM), 0, 1)

    tt = lax.broadcasted_iota(jnp.int32, (lb, CONV_DIM), 0)
    w0, w1, w2 = cw_ref[0:1, :], cw_ref[1:2, :], cw_ref[2:3, :]
    for b in range(nb):
        zb = z_ref[b]
        prev = zprev_s[b]
        z1 = jnp.where(tt == 0, prev[1:2], pltpu.roll(zb, 1, 0))
        z2 = jnp.where(tt == 0, prev[0:1], jnp.where(tt == 1, prev[1:2], pltpu.roll(zb, 2, 0)))
        conv_ref[b] = gb_ref[b] * (w0 * z2 + w1 * z1 + w2 * zb)
        last = zb[lb - (CONV_WIDTH - 1):lb]
        zprev_s[b] = last
        cst_out[b] = last


def _ssm_conv(u, z, gb, h0r, h0i, cprev, lw, lb):
    bsz, seq, _ = u.shape
    nb = SUBLANES
    tok = pl.BlockSpec((nb, lb, SSM_DIM), lambda g, j: (g, j, 0))
    st = pl.BlockSpec((nb, STATE_LANES), lambda g, j: (g, 0))
    cs = pl.BlockSpec((nb, CONV_WIDTH - 1, CONV_DIM), lambda g, j: (g, 0, 0))
    return pl.pallas_call(
        functools.partial(_ssm_body, lb=lb),
        grid=(bsz // nb, seq // lb),
        in_specs=[tok, tok, tok, st, st, cs,
                  _const_spec((SSM_DIM, STATE_LANES)), _const_spec((SSM_DIM, STATE_LANES)),
                  _const_spec((STATE_LANES, SSM_DIM)), _const_spec((STATE_LANES, SSM_DIM)),
                  _const_spec((1, STATE_LANES)), _const_spec((1, STATE_LANES)),
                  _const_spec((1, SSM_DIM)), _const_spec((SSM_DIM, SSM_DIM)), _const_spec((1, SSM_DIM)),
                  _const_spec((CONV_WIDTH, CONV_DIM))],
        out_specs=(tok, tok, st, st, cs),
        out_shape=(jax.ShapeDtypeStruct((bsz, seq, SSM_DIM), F32),
                   jax.ShapeDtypeStruct((bsz, seq, CONV_DIM), F32),
                   jax.ShapeDtypeStruct((bsz, STATE_LANES), F32),
                   jax.ShapeDtypeStruct((bsz, STATE_LANES), F32),
                   jax.ShapeDtypeStruct((bsz, CONV_WIDTH - 1, CONV_DIM), F32)),
        scratch_shapes=[pltpu.VMEM((STATE_LANES // LANES, nb * lb, LANES), F32),
                        pltpu.VMEM((STATE_LANES // LANES, nb * lb, LANES), F32),
                        pltpu.VMEM((nb, STATE_LANES), F32), pltpu.VMEM((nb, STATE_LANES), F32),
                        pltpu.VMEM((nb, CONV_WIDTH - 1, CONV_DIM), F32)],
        compiler_params=_params("arbitrary", "arbitrary"),
        name="ssm_conv",
    )(u, z, gb, h0r, h0i, cprev, lw["b_re_blk"], lw["b_im_blk"], lw["c_re_blk"], lw["c_im_blk"],
      lw["a_re"], lw["a_im"], lw["d"], lw["w_glu"], lw["b_glu"], lw["conv_w"])


def _rms_rows(t, w):
    return t * lax.rsqrt(jnp.mean(t * t, axis=-1, keepdims=True) + RMS_EPS) * w


def _mix_ffn_body(x_ref, att_ref, ssm_ref, conv_ref, bn_ref, wout_ref, ln2_ref, wup_ref, wdown_ref, o_ref,
                  *, ff_chunk):
    bn = bn_ref[...]
    mix = jnp.concatenate(
        [_rms_rows(att_ref[...], bn[:, 0:ATT_DIM]),
         _rms_rows(ssm_ref[...], bn[:, ATT_DIM:ATT_DIM + SSM_DIM]),
         _rms_rows(conv_ref[...], bn[:, ATT_DIM + SSM_DIM:])], axis=-1).astype(BF16)
    x1 = x_ref[...] + jnp.dot(mix, wout_ref[...], preferred_element_type=F32)
    h2 = _rms_rows(x1, ln2_ref[...]).astype(BF16)
    acc = x1
    for c in range(D_FF // ff_chunk):
        f = jnp.maximum(jnp.dot(h2, wup_ref[:, c * ff_chunk:(c + 1) * ff_chunk], preferred_element_type=F32), 0.0)
        acc = acc + jnp.dot((f * f).astype(BF16), wdown_ref[c * ff_chunk:(c + 1) * ff_chunk, :],
                            preferred_element_type=F32)
    o_ref[...] = acc


def _mix_ffn(x, att, ssm, conv, lw, tb):
    bsz, seq, _ = x.shape
    rows = bsz * seq
    tok = lambda w: pl.BlockSpec((tb, w), lambda i: (i, 0))
    once = lambda shape: pl.BlockSpec(shape, lambda i: (0, 0), pipeline_mode=pl.Buffered(1))
    out = pl.pallas_call(
        functools.partial(_mix_ffn_body, ff_chunk=1024),
        grid=(rows // tb,),
        in_specs=[tok(D_MODEL), tok(ATT_DIM), tok(SSM_DIM), tok(CONV_DIM),
                  once((1, D_MODEL)), once((D_MODEL, D_MODEL)), once((1, D_MODEL)),
                  once((D_MODEL, D_FF)), once((D_FF, D_MODEL))],
        out_specs=tok(D_MODEL),
        out_shape=jax.ShapeDtypeStruct((rows, D_MODEL), F32),
        compiler_params=_params("arbitrary"),
        name="mix_ffn",
    )(x.reshape(rows, D_MODEL), att.reshape(rows, ATT_DIM), ssm.reshape(rows, SSM_DIM),
      conv.reshape(rows, CONV_DIM), lw["bn"], lw["w_out"], lw["ln2"], lw["w_up"], lw["w_down"])
    return out.reshape(bsz, seq, D_MODEL)


def _layer_weights(l, ln1_w, w_in, b_forget, q_norm_w, k_norm_w, conv_w, ssm_lam_re, ssm_lam_im, ssm_log_dt,
                   ssm_b_re, ssm_b_im, ssm_c_re, ssm_c_im, ssm_d, w_glu, b_glu, branch_norm_w, w_out,
                   ln2_w, w_up, w_down):
    w = w_in[l]
    fg0 = 3 * ATT_DIM
    w_re = jnp.concatenate([w[:, :fg0], w[:, fg0 + N_HEADS:], w[:, fg0:fg0 + N_HEADS],
                            jnp.zeros((D_MODEL, LANES - N_HEADS), F32)], axis=1).astype(BF16)
    head_id = jnp.arange(ATT_DIM) // HEAD_DIM
    a_re, a_im, bb_re, bb_im = _s5_prep(ssm_lam_re[l], ssm_lam_im[l], ssm_log_dt[l], ssm_b_re[l], ssm_b_im[l])
    return dict(
        ln1=ln1_w[l].reshape(1, D_MODEL), w_in=w_re,
        qn=jnp.tile(q_norm_w[l], N_HEADS).reshape(1, ATT_DIM), kn=jnp.tile(k_norm_w[l], N_HEADS).reshape(1, ATT_DIM),
        bf=jnp.pad(b_forget[l], (0, LANES - N_HEADS)).reshape(1, LANES),
        mbd=(head_id[:, None] == head_id[None, :]).astype(BF16),
        a_re=a_re.reshape(1, STATE_LANES), a_im=a_im.reshape(1, STATE_LANES),
        b_re_blk=_block_diag(bb_re).astype(BF16), b_im_blk=_block_diag(bb_im).astype(BF16),
        c_re_blk=_block_diag(ssm_c_re[l]).T.astype(BF16), c_im_blk=_block_diag(ssm_c_im[l]).T.astype(BF16),
        d=ssm_d[l].reshape(1, SSM_DIM), w_glu=w_glu[l].astype(BF16), b_glu=b_glu[l].reshape(1, SSM_DIM),
        conv_w=conv_w[l], bn=branch_norm_w[l].reshape(1, D_MODEL), w_out=w_out[l].astype(BF16),
        ln2=ln2_w[l].reshape(1, D_MODEL), w_up=w_up[l].astype(BF16), w_down=w_down[l].astype(BF16))


def _run_layer(x, lw, layer, depth, prev, past, tiles):
    bsz, seq, _ = x.shape
    if past is None:
        c0 = jnp.zeros((bsz, 1, LANES), F32)
        h0r = jnp.zeros((bsz, STATE_LANES), F32)
        h0i = h0r
        cprev = jnp.zeros((bsz, CONV_WIDTH - 1, CONV_DIM), F32)
        qx, kx, vb, k_all, v_all, lft, u, z, gb = _inproj(x, c0, lw, tiles["tok"], layer, depth, prev, True)
        att = _attn_prompt(qx, kx, vb, tiles["attn"], tiles["attn_pairs"])
    else:
        cache_kt, cache_vt, clf_t, s_re, s_im, s_conv = past
        plen = cache_kt.shape[-1]
        ct_cache = _cache_cumsum(clf_t.reshape(bsz * N_HEADS, plen)).reshape(bsz, N_HEADS, plen)
        c0 = jnp.pad(ct_cache[:, :, plen - 1], ((0, 0), (0, LANES - N_HEADS))).reshape(bsz, 1, LANES)
        h0r = s_re.reshape(bsz, STATE_LANES)
        h0i = s_im.reshape(bsz, STATE_LANES)
        cprev = s_conv
        q, kb, vb, k_all, v_all, lft, cq, ct, u, z, gb = _inproj(x, c0, lw, tiles["tok"], layer, depth, prev, False)
        att = _attn_sample(q, kb, vb, cq, ct, cache_kt, cache_vt, ct_cache, layer)
    ssm, conv, hr, hi, cst = _ssm_conv(u, z, gb, h0r, h0i, cprev, lw, tiles["scan"])
    y = _mix_ffn(x, att, ssm, conv, lw, tiles["ffn"])
    small = (lft, hr.reshape(bsz, SSM_GROUPS, SSM_STATE), hi.reshape(bsz, SSM_GROUPS, SSM_STATE), cst)
    return y, (k_all, v_all), small


def _tiles(bsz, seq):
    tok = min(seq, 512)
    return dict(tok=tok, attn=min(seq, 512), attn_pairs=2, scan=min(seq, 128), ffn=min(bsz * seq, 512))


def _forward(x_prompt, x_sample, cache_k, cache_v, cache_logf, state_ssm_re, state_ssm_im, state_conv, *weights):
    depth = cache_k.shape[0]
    xp, xs = x_prompt, x_sample
    bp, lp = xp.shape[0], xp.shape[1]
    tiles_p = _tiles(bp, lp)
    tiles_s = _tiles(xs.shape[0], xs.shape[1])
    cache_kt = jnp.transpose(cache_k, (0, 1, 3, 4, 2))
    cache_vt = jnp.transpose(cache_v, (0, 1, 3, 4, 2))
    cache_lft = jnp.swapaxes(cache_logf, 2, 3)
    kv_p, kv_s, small_p, small_s = None, None, [], []
    for l in range(depth):
        lw = _layer_weights(l, *weights)
        xp, kv_p, sp = _run_layer(xp, lw, l, depth, kv_p, None, tiles_p)
        xs, kv_s, ss = _run_layer(xs, lw, l, depth, kv_s,
                                  (cache_kt, cache_vt, cache_lft[l], state_ssm_re[l], state_ssm_im[l], state_conv[l]),
                                  tiles_s)
        small_p.append(sp)
        small_s.append(ss)
    stk = lambda lst, i: jnp.stack([s[i] for s in lst])
    from_t = lambda t: jnp.transpose(t.reshape(depth, bp, N_HEADS, HEAD_DIM, lp), (0, 1, 4, 2, 3))
    return (xp, xs,
            from_t(kv_p[0]), from_t(kv_p[1]), jnp.swapaxes(stk(small_p, 0), 2, 3),
            stk(small_p, 1), stk(small_p, 2), stk(small_p, 3),
            kv_s[0], kv_s[1], jnp.swapaxes(stk(small_s, 0), 2, 3),
            stk(small_s, 1), stk(small_s, 2), stk(small_s, 3))


def kernel(x_prompt, x_sample, cache_k, cache_v, cache_logf, state_ssm_re, state_ssm_im, state_conv, ln1_w, w_in, b_forget, q_norm_w, k_norm_w, conv_w, ssm_lam_re, ssm_lam_im, ssm_log_dt, ssm_b_re, ssm_b_im, ssm_c_re, ssm_c_im, ssm_d, w_glu, b_glu, branch_norm_w, w_out, ln2_w, w_up, w_down):
    return _forward(x_prompt, x_sample, cache_k, cache_v, cache_logf, state_ssm_re, state_ssm_im, state_conv,
                    ln1_w, w_in, b_forget, q_norm_w, k_norm_w, conv_w, ssm_lam_re, ssm_lam_im, ssm_log_dt,
                    ssm_b_re, ssm_b_im, ssm_c_re, ssm_c_im, ssm_d, w_glu, b_glu, branch_norm_w, w_out,
                    ln2_w, w_up, w_down)
```

```python
import functools
import math

import jax
import jax.numpy as jnp
from jax import lax
from jax.experimental import pallas as pl
from jax.experimental.pallas import tpu as pltpu

F32 = jnp.float32
BF16 = jnp.bfloat16
HIGHEST = lax.Precision.HIGHEST

D_MODEL = 1024
N_HEADS = 8
HEAD_DIM = 64
ATT_DIM = N_HEADS * HEAD_DIM
SSM_DIM = 256
SSM_GROUP = 16
SSM_GROUPS = SSM_DIM // SSM_GROUP
SSM_STATE = 64
STATE_LANES = SSM_GROUPS * SSM_STATE
CONV_DIM = 256
CONV_WIDTH = 3
D_FF = 4 * D_MODEL
RMS_EPS = 1e-6
NEG_INF = -1e30
LOG2E = math.log2(math.e)

LANES = 128
SUBLANES = 8
HEAD_PAIRS = ATT_DIM // LANES
PAIR_EXT = 2 * LANES
AUG_STRIDE = 8
BIAS_PIECES = 3
PROJ_PAD = 3 * ATT_DIM + SSM_DIM + 3 * CONV_DIM + LANES
VMEM_LIMIT = 56 * 1024 * 1024


def _params(*sem):
    return pltpu.CompilerParams(dimension_semantics=sem, vmem_limit_bytes=VMEM_LIMIT)


def _const_spec(shape):
    zeros = (0,) * len(shape)
    return pl.BlockSpec(shape, lambda *_: zeros)


def _layer_spec(shape, layer, **kw):
    index = (layer,) + (0,) * len(shape)
    return pl.BlockSpec((None,) + tuple(shape), lambda *_: index, **kw)


def _nt_dot(a, b, **kw):
    return lax.dot_general(a, b, (((1,), (1,)), ((), ())), preferred_element_type=F32, **kw)


def _bias_selectors():
    h = jnp.arange(N_HEADS)
    base = LANES * (h // 2) + AUG_STRIDE * (h % 2)
    sq = jnp.zeros((ATT_DIM, LANES), F32)
    sk = jnp.zeros((LANES, ATT_DIM), F32)
    for r in range(BIAS_PIECES):
        sq = sq.at[base + r, N_HEADS * r + h].set(1.0)
        sk = sk.at[N_HEADS * r + h, base + BIAS_PIECES + r].set(-1.0)
    return sq.astype(BF16), sk.astype(BF16)


def _split3(t):
    p1 = t.astype(BF16)
    r1 = t - p1.astype(F32)
    p2 = r1.astype(BF16)
    p3 = (r1 - p2.astype(F32)).astype(BF16)
    return p1, p2, p3


def _inproj_body(*refs, cum_block, prompt, n_prev):
    (x_ref, ln1_ref, w_ref, qnw_ref, knw_ref, bf_ref, mbd_ref, tri_ref, sel_ref, c0_ref, sq_ref, sk_ref) = refs[:12]
    outs = refs[12 + n_prev:]
    carry_ref = outs[-1]
    i = pl.program_id(1)
    x = x_ref[...]
    ms = jnp.mean(x * x, axis=-1, keepdims=True)
    h = (x * lax.rsqrt(ms + RMS_EPS) * ln1_ref[...]).astype(BF16)
    proj = jnp.dot(h, w_ref[...], preferred_element_type=F32)
    tb = x.shape[0]

    def head_norm(t, w):
        ss = jnp.dot((t * t).astype(BF16), mbd_ref[...], preferred_element_type=F32)
        return t * lax.rsqrt(ss * (1.0 / HEAD_DIM) + RMS_EPS) * w

    q = head_norm(proj[:, 0:ATT_DIM], qnw_ref[...]) * (LOG2E * HEAD_DIM ** -0.5)
    k = head_norm(proj[:, ATT_DIM:2 * ATT_DIM], knw_ref[...])
    v = proj[:, 2 * ATT_DIM:3 * ATT_DIM]
    o = 3 * ATT_DIM
    u = proj[:, o:o + SSM_DIM]
    hc = proj[:, o + SSM_DIM:o + SSM_DIM + CONV_DIM]
    gb = proj[:, o + SSM_DIM + CONV_DIM:o + SSM_DIM + 2 * CONV_DIM]
    gc = proj[:, o + SSM_DIM + 2 * CONV_DIM:o + SSM_DIM + 3 * CONV_DIM]
    lf = jax.nn.log_sigmoid(proj[:, PROJ_PAD - LANES:PROJ_PAD] + bf_ref[...])

    @pl.when(i == 0)
    def _():
        carry_ref[...] = c0_ref[...]

    carry = carry_ref[...]
    pieces = []
    for s in range(tb // cum_block):
        blk = jnp.concatenate(_split3(lf[s * cum_block:(s + 1) * cum_block]), axis=-1)
        r = jnp.dot(tri_ref[...], blk, preferred_element_type=F32)
        c = r[:, 0:LANES] + r[:, LANES:2 * LANES] + r[:, 2 * LANES:3 * LANES] + carry
        carry = c[cum_block - 1:cum_block]
        pieces.append(c)
    c_all = pieces[0] if len(pieces) == 1 else jnp.concatenate(pieces, axis=0)
    carry_ref[...] = carry
    c2 = c_all * LOG2E
    lf_t = _nt_dot(sel_ref[...], lf, precision=HIGHEST)

    if prompt:
        qx_ref, kx_ref, vb_ref, kt_ref, vt_ref, lft_ref, u_ref, z_ref, gb_ref = outs[:-1]
        head_lane = lax.broadcasted_iota(jnp.int32, (tb, LANES), 1) < N_HEADS
        p1, p2, p3 = (p.astype(F32) for p in _split3(jnp.where(head_lane, c2, 0.0)))
        pcs = (p1 + pltpu.roll(p2, N_HEADS, 1) + pltpu.roll(p3, 2 * N_HEADS, 1)).astype(BF16)
        row = lax.broadcasted_iota(jnp.int32, (ATT_DIM, tb), 0) % LANES
        q_one = (row < 2 * AUG_STRIDE) & (row % AUG_STRIDE >= BIAS_PIECES) & (row % AUG_STRIDE < 2 * BIAS_PIECES)
        qa_t = _nt_dot(sq_ref[...], pcs) + q_one.astype(F32)
        lane = lax.broadcasted_iota(jnp.int32, (tb, ATT_DIM), 1) % LANES
        k_one = (lane < 2 * AUG_STRIDE) & (lane % AUG_STRIDE < BIAS_PIECES)
        ka = jnp.dot(pcs, sk_ref[...], preferred_element_type=F32) + k_one.astype(F32)
        k_t, v_t = k.T, v.T
        qtb, qatb, kb, kab = q.T.astype(BF16), qa_t.astype(BF16), k.astype(BF16), ka.astype(BF16)
        qx, kx = [], []
        for p in range(HEAD_PAIRS):
            sl = slice(p * LANES, (p + 1) * LANES)
            qx += [qtb[sl], qatb[sl]]
            kx += [kb[:, sl], kab[:, sl]]
        qx_ref[...] = jnp.concatenate(qx, axis=0)
        kx_ref[...] = jnp.concatenate(kx, axis=-1)
        vb_ref[...] = v_t.astype(BF16)
        kt_ref[...] = k_t
        vt_ref[...] = v_t
    else:
        q_ref, kb_ref, vb_ref, kf_ref, vf_ref, lft_ref, cq_ref, ct_ref, u_ref, z_ref, gb_ref = outs[:-1]
        q_ref[...] = q.astype(BF16)
        kb_ref[...] = k.astype(BF16)
        vb_ref[...] = v.astype(BF16)
        kf_ref[...] = k.reshape(tb, N_HEADS, HEAD_DIM)
        vf_ref[...] = v.reshape(tb, N_HEADS, HEAD_DIM)
        cq_ref[...] = c2[:, 0:N_HEADS]
        ct_ref[...] = _nt_dot(sel_ref[...], c2, precision=HIGHEST)
    lft_ref[...] = lf_t
    u_ref[...] = u
    z_ref[...] = gc * hc
    gb_ref[...] = gb


def _inproj(x, c0, lw, tb, layer, depth, prev, prompt):
    bsz, seq, _ = x.shape
    cum_block = min(tb, LANES)
    tri = (lax.broadcasted_iota(jnp.int32, (cum_block, cum_block), 1)
           <= lax.broadcasted_iota(jnp.int32, (cum_block, cum_block), 0)).astype(BF16)
    sel =(lax.broadcasted_iota(jnp.int32, (N_HEADS, LANES), 0)
           == lax.broadcasted_iota(jnp.int32, (N_HEADS, LANES), 1)).astype(F32)
    sq, sk = _bias_selectors()
    tok = lambda w: pl.BlockSpec((None, tb, w), lambda b, i: (b, i, 0))
    head_major = pl.BlockSpec((None, N_HEADS, tb), lambda b, i: (b, 0, i))
    tok_shape = lambda w, dt: jax.ShapeDtypeStruct((bsz, seq, w), dt)
    heads_shape = jax.ShapeDtypeStruct((bsz, N_HEADS, seq), F32)
    if prompt:
        state_shape = jax.ShapeDtypeStruct((depth, bsz, ATT_DIM, seq), F32)
        state_spec = pl.BlockSpec((None, None, ATT_DIM, tb), lambda b, i: (layer, b, 0, i))
        rows_major = lambda r: pl.BlockSpec((None, r, tb), lambda b, i: (b, 0, i))
        out_shapes = (jax.ShapeDtypeStruct((bsz, HEAD_PAIRS * PAIR_EXT, seq), BF16),
                      tok_shape(HEAD_PAIRS * PAIR_EXT, BF16),
                      jax.ShapeDtypeStruct((bsz, ATT_DIM, seq), BF16),
                      state_shape, state_shape, heads_shape,
                      tok_shape(SSM_DIM, F32), tok_shape(CONV_DIM, F32), tok_shape(CONV_DIM, F32))
        out_specs = (rows_major(HEAD_PAIRS * PAIR_EXT), tok(HEAD_PAIRS * PAIR_EXT), rows_major(ATT_DIM),
                     state_spec, state_spec, head_major,
                     tok(SSM_DIM), tok(CONV_DIM), tok(CONV_DIM))
        state_out = (3, 4)
    else:
        state_shape = jax.ShapeDtypeStruct((depth, bsz, seq, N_HEADS, HEAD_DIM), F32)
        state_spec = pl.BlockSpec((None, None, tb, N_HEADS, HEAD_DIM), lambda b, i: (layer, b, i, 0, 0))
        out_shapes = (tok_shape(ATT_DIM, BF16), tok_shape(ATT_DIM, BF16), tok_shape(ATT_DIM, BF16),
                      state_shape, state_shape, heads_shape,
                      tok_shape(N_HEADS, F32), heads_shape,
                      tok_shape(SSM_DIM, F32), tok_shape(CONV_DIM, F32), tok_shape(CONV_DIM, F32))
        out_specs = (tok(ATT_DIM), tok(ATT_DIM), tok(ATT_DIM), state_spec, state_spec, head_major,
                     tok(N_HEADS), head_major, tok(SSM_DIM), tok(CONV_DIM), tok(CONV_DIM))
        state_out = (3, 4)
    in_specs = [tok(D_MODEL), _layer_spec((1, D_MODEL), layer), _layer_spec((D_MODEL, PROJ_PAD), layer),
                _layer_spec((1, ATT_DIM), layer), _layer_spec((1, ATT_DIM), layer), _layer_spec((1, LANES), layer),
                _const_spec((ATT_DIM, ATT_DIM)), _const_spec((cum_block, cum_block)),
                _const_spec((N_HEADS, LANES)),
                pl.BlockSpec((None, 1, LANES), lambda b, i: (b, 0, 0)),
                _const_spec(sq.shape), _const_spec(sk.shape)]
    args = [x, lw["ln1"], lw["w_in"], lw["qn"], lw["kn"], lw["bf"], lw["mbd"], tri, sel, c0, sq, sk]
    aliases = {}
    n_prev = 0
    if prev is not None:
        n_prev = len(prev)
        for n, buf in enumerate(prev):
            aliases[len(args)] = state_out[n]
            args.append(buf)
            in_specs.append(pl.BlockSpec(memory_space=pl.ANY))
    return pl.pallas_call(
        functools.partial(_inproj_body, cum_block=cum_block, prompt=prompt, n_prev=n_prev),
        grid=(bsz, seq // tb),
        in_specs=in_specs, out_specs=out_specs, out_shape=out_shapes,
        scratch_shapes=[pltpu.VMEM((1, LANES), F32)],
        input_output_aliases=aliases,
        compiler_params=_params("arbitrary", "arbitrary"),
        name="inproj_prompt" if prompt else "inproj_sample",
    )(*args)


def _attn_body(qx_ref, kx_ref, vt_ref, o_ref, m_s, mb_s, acc_s, s_s, *, tq, pairs):
    nq = o_ref.shape[0] // tq
    row_x = lax.broadcasted_iota(jnp.int32, (PAIR_EXT, tq), 0)
    row_v = lax.broadcasted_iota(jnp.int32, (LANES, tq), 0)
    head_rows = [(row_v >= HEAD_DIM * a) & (row_v < HEAD_DIM * (a + 1)) for a in range(2)]
    mine = [(((row_x >= HEAD_DIM * a) & (row_x < HEAD_DIM * (a + 1)))
             | ((row_x >= LANES + AUG_STRIDE * a) & (row_x < LANES + AUG_STRIDE * (a + 1)))) for a in range(2)]
    key = lax.broadcasted_iota(jnp.int32, (tq, tq), 0)
    qry = lax.broadcasted_iota(jnp.int32, (tq, tq), 1)

    def scores(qi, j, p):
        cols = pl.ds(pl.multiple_of(qi * tq, tq), tq)
        qt = qx_ref[p * PAIR_EXT:(p + 1) * PAIR_EXT, cols]
        rows = pl.ds(pl.multiple_of(j * tq, tq), tq)
        kx = kx_ref[rows, p * PAIR_EXT:(p + 1) * PAIR_EXT]
        for a in range(2):
            s = jnp.dot(kx, jnp.where(mine[a], qt, jnp.zeros_like(qt)), preferred_element_type=F32)
            s_s[2 * p + a] = s
            mb_s[2 * p + a] = jnp.max(s, axis=0, keepdims=True)

    def absorb(j, p, masked):
        cols = pl.ds(pl.multiple_of(j * tq, tq), tq)
        vt = vt_ref[p * LANES:(p + 1) * LANES, cols]
        for a in range(2):
            h = 2 * p + a
            s = s_s[h]
            if masked:
                s = jnp.where(key <= qry, s, NEG_INF)
                m_blk = jnp.max(s, axis=0, keepdims=True)
            else:
                m_blk = mb_s[h]
            va = jnp.where(head_rows[a], vt, jnp.ones_like(vt))
            m_old = m_s[h]
            m_new = jnp.maximum(m_old, m_blk)
            prob = jnp.exp2(s - m_new).astype(BF16)
            out = jnp.dot(va, prob, preferred_element_type=F32)
            acc_s[h] = jnp.exp2(m_old - m_new) * acc_s[h] + out
            m_s[h] = m_new

    def reset():
        for h in range(2 * pairs):
            m_s[h] = jnp.full((1, tq), NEG_INF, F32)
            acc_s[h] = jnp.zeros((LANES, tq), F32)

    def key_block(qi, j, masked, nxt):
        for p in range(pairs):
            if p + 1 < pairs:
                scores(qi, j, p + 1)
            else:
                scores(nxt[0], nxt[1], 0)
            absorb(j, p, masked)

    def query_block(qi, carry):
        def body(j, c):
            key_block(qi, j, False, (qi, j + 1))
            return c

        lax.fori_loop(0, qi, body, 0)
        key_block(qi, qi, True, (jnp.minimum(qi + 1, nq - 1), 0))
        rows = pl.ds(pl.multiple_of(qi * tq, tq), tq)
        for p in range(pairs):
            a0, a1 = acc_s[2 * p], acc_s[2 * p + 1]
            out_t = jnp.concatenate([a0[0:HEAD_DIM] / a0[HEAD_DIM:LANES],
                                     a1[HEAD_DIM:LANES] / a1[0:HEAD_DIM]], axis=0)
            o_ref[rows, p * LANES:(p + 1) * LANES] = out_t.T
        reset()
        return carry

    reset()
    scores(0, 0, 0)
    lax.fori_loop(0, nq, query_block, 0)


def _attn_prompt(qx_t, kx, vt, tq, pairs):
    bsz, seq, _ = kx.shape
    return pl.pallas_call(
        functools.partial(_attn_body, tq=tq, pairs=pairs),
        grid=(bsz, HEAD_PAIRS // pairs),
        in_specs=[pl.BlockSpec((None, pairs * PAIR_EXT, seq), lambda b, h: (b, h, 0)),
                  pl.BlockSpec((None, seq, pairs * PAIR_EXT), lambda b, h: (b, 0, h)),
                  pl.BlockSpec((None, pairs * LANES, seq), lambda b, h: (b, h, 0))],
        out_specs=pl.BlockSpec((None, seq, pairs * LANES), lambda b, h: (b, 0, h)),
        out_shape=jax.ShapeDtypeStruct((bsz, seq, ATT_DIM), F32),
        scratch_shapes=[pltpu.VMEM((2 * pairs, 1, tq), F32), pltpu.VMEM((2 * pairs, 1, tq), F32),
                        pltpu.VMEM((2 * pairs, LANES, tq), F32),
                        pltpu.VMEM((2 * pairs, tq, tq), F32)],
        compiler_params=_params("arbitrary", "arbitrary"),
        name="attn_prompt",
    )(qx_t, kx, vt)


def _attn_sample_body(q_ref, ckt_ref, cvt_ref, kn_ref, vn_ref, cq_ref, ctc_ref, ctn_ref, o_ref, *, pairs):
    tq = q_ref.shape[0]
    past = ckt_ref.shape[-1]
    cq8 = cq_ref[...]
    lane = lax.broadcasted_iota(jnp.int32, (tq, LANES), 1)
    lane8 = lax.broadcasted_iota(jnp.int32, (tq, N_HEADS), 1)
    row = lax.broadcasted_iota(jnp.int32, (tq, tq), 0)
    col = lax.broadcasted_iota(jnp.int32, (tq, tq), 1)
    for pp in range(pairs):
        hp = pl.program_id(1) * pairs + pp
        lanes = slice(pp * LANES, (pp + 1) * LANES)
        q = q_ref[:, lanes]
        q2 = jnp.concatenate(
            [jnp.where((lane >= HEAD_DIM * a) & (lane < HEAD_DIM * (a + 1)), q, jnp.zeros_like(q)) for a in range(2)],
            axis=0)
        kt = ckt_ref[2 * pp:2 * pp + 2].reshape(LANES, past).astype(BF16)
        vt = cvt_ref[2 * pp:2 * pp + 2].reshape(LANES, past).astype(BF16)
        s_cache = jnp.dot(q2, kt, preferred_element_type=F32)
        s_new = _nt_dot(q2, kn_ref[:, lanes])
        sc, sn = [], []
        for a in range(2):
            head = 2 * hp + a
            cqa = jnp.sum(jnp.where(lane8 == head, cq8, 0.0), axis=-1, keepdims=True)
            ck_cache = ctc_ref[pl.ds(head, 1), :] * LOG2E
            ck_new = ctn_ref[pl.ds(head, 1), :]
            sc.append(s_cache[a * tq:(a + 1) * tq] + cqa - ck_cache)
            sn.append(jnp.where(col <= row, s_new[a * tq:(a + 1) * tq] + cqa - ck_new, NEG_INF))
        s_cache = jnp.concatenate(sc, axis=0)
        s_new = jnp.concatenate(sn, axis=0)
        m = jnp.maximum(jnp.max(s_cache, axis=-1, keepdims=True), jnp.max(s_new, axis=-1, keepdims=True))
        p_cache = jnp.exp2(s_cache - m)
        p_new = jnp.exp2(s_new - m)
        l = jnp.sum(p_cache, axis=-1, keepdims=True) + jnp.sum(p_new, axis=-1, keepdims=True)
        o = (_nt_dot(p_cache.astype(BF16), vt)
             + jnp.dot(p_new.astype(BF16), vn_ref[:, lanes], preferred_element_type=F32)) / l
        o_ref[:, lanes] = jnp.where(lane < HEAD_DIM, o[0:tq], o[tq:2 * tq])


def _attn_sample(q, kb, vb, cq, ct_new, cache_kt, cache_vt, ct_cache, layer, pairs=2):
    bsz, seq, _ = q.shape
    past = cache_kt.shape[-1]
    cache_blk = pl.BlockSpec((None, None, 2 * pairs, HEAD_DIM, past), lambda b, h: (layer, b, h, 0, 0))
    new_blk = pl.BlockSpec((None, seq, pairs * LANES), lambda b, h: (b, 0, h))
    return pl.pallas_call(
        functools.partial(_attn_sample_body, pairs=pairs),
        grid=(bsz, HEAD_PAIRS // pairs),
        in_specs=[new_blk, cache_blk, cache_blk, new_blk, new_blk,
                  pl.BlockSpec((None, seq, N_HEADS), lambda b, h: (b, 0, 0)),
                  pl.BlockSpec((None, N_HEADS, past), lambda b, h: (b, 0, 0)),
                  pl.BlockSpec((None, N_HEADS, seq), lambda b, h: (b, 0, 0))],
        out_specs=new_blk,
        out_shape=jax.ShapeDtypeStruct((bsz, seq, ATT_DIM), F32),
        compiler_params=_params("arbitrary", "arbitrary"),
        name="attn_sample",
    )(q, cache_kt, cache_vt, kb, vb, cq, ct_cache, ct_new)


def _cache_cumsum_body(x_ref, o_ref):
    rows, past = x_ref.shape
    tri = (lax.broadcasted_iota(jnp.int32, (LANES, LANES), 0)
           <= lax.broadcasted_iota(jnp.int32, (LANES, LANES), 1)).astype(F32)
    carry = jnp.zeros((rows, 1), F32)
    for b in range(past // LANES):
        blk = x_ref[:, b * LANES:(b + 1) * LANES]
        c = jnp.dot(blk, tri, precision=HIGHEST, preferred_element_type=F32) + carry
        o_ref[:, b * LANES:(b + 1) * LANES] = c
        carry = c[:, LANES - 1:LANES]


def _cache_cumsum(x):
    return pl.pallas_call(
        _cache_cumsum_body,
        out_shape=jax.ShapeDtypeStruct(x.shape, F32),
        compiler_params=pltpu.CompilerParams(vmem_limit_bytes=VMEM_LIMIT),
        name="cache_cumsum",
    )(x)


def _s5_prep_body(lr_ref, li_ref, ldt_ref, br_ref, bi_ref, ar_ref, ai_ref, bbr_ref, bbi_ref):
    lr = jnp.minimum(lr_ref[...], -1e-4)
    li = li_ref[...]
    dt = jnp.exp(ldt_ref[...])
    ldr, ldi = lr * dt, li * dt
    mag = jnp.exp(ldr)
    ar, ai = mag * jnp.cos(ldi), mag * jnp.sin(ldi)
    den = lr * lr + li * li
    nr = ar - 1.0
    qr = (nr * lr + ai * li) / den
    qi = (ai * lr - nr * li) / den
    ar_ref[...] = ar
    ai_ref[...] = ai
    br, bi = br_ref[...], bi_ref[...]
    bbr_ref[...] = qr[:, None, :] * br - qi[:, None, :] * bi
    bbi_ref[...] = qr[:, None, :] * bi + qi[:, None, :] * br


def _s5_prep(lam_re, lam_im, log_dt, b_re, b_im):
    g, p = lam_re.shape
    gp = jax.ShapeDtypeStruct((g, p), F32)
    gcp = jax.ShapeDtypeStruct((g, SSM_GROUP, p), F32)
    return pl.pallas_call(
        _s5_prep_body, out_shape=(gp, gp, gcp, gcp), name="s5_prep",
    )(lam_re, lam_im, log_dt.reshape(g, 1), jnp.swapaxes(b_re, 1, 2), jnp.swapaxes(b_im, 1, 2))


def _block_diag(t):
    d, g, c, p = t.shape
    eye = jnp.eye(g, dtype=t.dtype)
    return (t[:, :, :, None, :] * eye[None, :, None, :, None]).reshape(d, g * c, g * p)


def _ssm_body(u_ref, z_ref, gb_ref, h0r_ref, h0i_ref, cprev_ref,
              bre_ref, bim_ref, cre_ref, cim_ref, ar_ref, ai_ref, d_ref, wglu_ref, bglu_ref, cw_ref,
              ssm_ref, conv_ref, hr_out, hi_out, cst_out,
              sre, sim, hr_s, hi_s, zprev_s, *, lb):
    j = pl.program_id(1)
    nb = SUBLANES
    tiles = STATE_LANES // LANES

    @pl.when(j == 0)
    def _():
        hr_s[...] = h0r_ref[...]
        hi_s[...] = h0i_ref[...]
        zprev_s[...] = cprev_ref[...]

    u2 = jnp.swapaxes(u_ref[...], 0, 1).reshape(lb * nb, SSM_DIM)
    ub = u2.astype(BF16)
    bu_re = jnp.dot(ub, bre_ref[...], preferred_element_type=F32)
    bu_im = jnp.dot(ub, bim_ref[...], preferred_element_type=F32)
    for c in range(tiles):
        sre[c] = bu_re[:, c * LANES:(c + 1) * LANES]
        sim[c] = bu_im[:, c * LANES:(c + 1) * LANES]
    ar = [jnp.broadcast_to(ar_ref[:, c * LANES:(c + 1) * LANES], (nb, LANES)) for c in range(tiles)]
    ai = [jnp.broadcast_to(ai_ref[:, c * LANES:(c + 1) * LANES], (nb, LANES)) for c in range(tiles)]

    def step(t, carry):
        hr, hi = carry
        rows = pl.ds(pl.multiple_of(t * nb, nb), nb)
        nr, ni = [], []
        for c in range(tiles):
            r = ar[c] * hr[c] - ai[c] * hi[c] + sre[c, rows, :]
            i = ar[c] * hi[c] + ai[c] * hr[c] + sim[c, rows, :]
            sre[c, rows, :] = r
            sim[c, rows, :] = i
            nr.append(r)
            ni.append(i)
        return tuple(nr), tuple(ni)

    init = (tuple(hr_s[:, c * LANES:(c + 1) * LANES] for c in range(tiles)),
            tuple(hi_s[:, c * LANES:(c + 1) * LANES] for c in range(tiles)))
    hr, hi = lax.fori_loop(0, lb, step, init, unroll=2)
    for c in range(tiles):
        hr_s[:, c * LANES:(c + 1) * LANES] = hr[c]
        hi_s[:, c * LANES:(c + 1) * LANES] = hi[c]
        hr_out[:, c * LANES:(c + 1) * LANES] = hr[c]
        hi_out[:, c * LANES:(c + 1) * LANES] = hi[c]

    h_re = jnp.concatenate([sre[c] for c in range(tiles)], axis=-1).astype(BF16)
    h_im = jnp.concatenate([sim[c] for c in range(tiles)], axis=-1).astype(BF16)
    y = (jnp.dot(h_re, cre_ref[...], preferred_element_type=F32)
         - jnp.dot(h_im, cim_ref[...], preferred_element_type=F32))
    y = y + d_ref[...] * u2
    g = jax.nn.gelu(y)
    gate = jax.nn.sigmoid(jnp.dot(g.astype(BF16), wglu_ref[...], preferred_element_type=F32) + bglu_ref[...])
    ssm_ref[...] = jnp.swapaxes((g * gate).reshape(lb, nb, SSM_DIM), 0, 1)

    tt = lax.broadcasted_iota(jnp.int32, (lb, CONV_DIM), 0)
    w0, w1, w2 = cw_ref[0:1, :], cw_ref[1:2, :], cw_ref[2:3, :]
    for b in range(nb):
        zb = z_ref[b]
        prev = zprev_s[b]
        z1 = jnp.where(tt == 0, prev[1:2], pltpu.roll(zb, 1, 0))
        z2 = jnp.where(tt == 0, prev[0:1], jnp.where(tt == 1, prev[1:2], pltpu.roll(zb, 2, 0)))
        conv_ref[b] = gb_ref[b] * (w0 * z2 + w1 * z1 + w2 * zb)
        last = zb[lb - (CONV_WIDTH - 1):lb]
        zprev_s[b] = last
        cst_out[b] = last


def _ssm_conv(u, z, gb, h0r, h0i, cprev, lw, layer, lb):
    bsz, seq, _ = u.shape
    nb = SUBLANES
    tok = pl.BlockSpec((nb, lb, SSM_DIM), lambda g, j: (g, j, 0))
    st = pl.BlockSpec((nb, STATE_LANES), lambda g, j: (g, 0))
    cs = pl.BlockSpec((nb, CONV_WIDTH - 1, CONV_DIM), lambda g, j: (g, 0, 0))
    par = lambda *shape: _layer_spec(shape, layer)
    return pl.pallas_call(
        functools.partial(_ssm_body, lb=lb),
        grid=(bsz // nb, seq // lb),
        in_specs=[tok, tok, tok, st, st, cs,
                  par(SSM_DIM, STATE_LANES), par(SSM_DIM, STATE_LANES),
                  par(STATE_LANES, SSM_DIM), par(STATE_LANES, SSM_DIM),
                  par(1, STATE_LANES), par(1, STATE_LANES),
                  par(1, SSM_DIM), par(SSM_DIM, SSM_DIM), par(1, SSM_DIM),
                  par(CONV_WIDTH, CONV_DIM)],
        out_specs=(tok, tok, st, st, cs),
        out_shape=(jax.ShapeDtypeStruct((bsz, seq, SSM_DIM), F32),
                   jax.ShapeDtypeStruct((bsz, seq, CONV_DIM), F32),
                   jax.ShapeDtypeStruct((bsz, STATE_LANES), F32),
                   jax.ShapeDtypeStruct((bsz, STATE_LANES), F32),
                   jax.ShapeDtypeStruct((bsz, CONV_WIDTH - 1, CONV_DIM), F32)),
        scratch_shapes=[pltpu.VMEM((STATE_LANES // LANES, nb * lb, LANES), F32),
                        pltpu.VMEM((STATE_LANES // LANES, nb * lb, LANES), F32),
                        pltpu.VMEM((nb, STATE_LANES), F32), pltpu.VMEM((nb, STATE_LANES), F32),
                        pltpu.VMEM((nb, CONV_WIDTH - 1, CONV_DIM), F32)],
        compiler_params=_params("arbitrary", "arbitrary"),
        name="ssm_conv",
    )(u, z, gb, h0r, h0i, cprev, lw["b_re_blk"], lw["b_im_blk"], lw["c_re_blk"], lw["c_im_blk"],
      lw["a_re"], lw["a_im"], lw["d"], lw["w_glu"], lw["b_glu"], lw["conv_w"])


def _rms_rows(t, w):
    return t * lax.rsqrt(jnp.mean(t * t, axis=-1, keepdims=True) + RMS_EPS) * w


def _mix_ffn_body(x_ref, att_ref, ssm_ref, conv_ref, bn_ref, wout_ref, ln2_ref, wup_ref, wdown_ref, o_ref,
                  *, ff_chunk):
    bn = bn_ref[...]
    mix = jnp.concatenate(
        [_rms_rows(att_ref[...], bn[:, 0:ATT_DIM]),
         _rms_rows(ssm_ref[...], bn[:, ATT_DIM:ATT_DIM + SSM_DIM]),
         _rms_rows(conv_ref[...], bn[:, ATT_DIM + SSM_DIM:])], axis=-1).astype(BF16)
    x1 = x_ref[...] + jnp.dot(mix, wout_ref[...], preferred_element_type=F32)
    h2 = _rms_rows(x1, ln2_ref[...]).astype(BF16)
    acc = x1
    for c in range(D_FF // ff_chunk):
        f = jnp.maximum(jnp.dot(h2, wup_ref[:, c * ff_chunk:(c + 1) * ff_chunk], preferred_element_type=F32), 0.0)
        acc = acc + jnp.dot((f * f).astype(BF16), wdown_ref[c * ff_chunk:(c + 1) * ff_chunk, :],
                            preferred_element_type=F32)
    o_ref[...] = acc


def _mix_ffn(x, att, ssm, conv, lw, layer, tb):
    bsz, seq, _ = x.shape
    rows = bsz * seq
    tok = lambda w: pl.BlockSpec((tb, w), lambda i: (i, 0))
    once = lambda *shape: _layer_spec(shape, layer, pipeline_mode=pl.Buffered(1))
    out = pl.pallas_call(
        functools.partial(_mix_ffn_body, ff_chunk=1024),
        grid=(rows // tb,),
        in_specs=[tok(D_MODEL), tok(ATT_DIM), tok(SSM_DIM), tok(CONV_DIM),
                  once(1, D_MODEL), once(D_MODEL, D_MODEL), once(1, D_MODEL),
                  once(D_MODEL, D_FF), once(D_FF, D_MODEL)],
        out_specs=tok(D_MODEL),
        out_shape=jax.ShapeDtypeStruct((rows, D_MODEL), F32),
        compiler_params=_params("arbitrary"),
        name="mix_ffn",
    )(x.reshape(rows, D_MODEL), att.reshape(rows, ATT_DIM), ssm.reshape(rows, SSM_DIM),
      conv.reshape(rows, CONV_DIM), lw["bn"], lw["w_out"], lw["ln2"], lw["w_up"], lw["w_down"])
    return out.reshape(bsz, seq, D_MODEL)


def _prep_weights(ln1_w, w_in, b_forget, q_norm_w, k_norm_w, conv_w, ssm_lam_re, ssm_lam_im, ssm_log_dt,
                  ssm_b_re, ssm_b_im, ssm_c_re, ssm_c_im, ssm_d, w_glu, b_glu, branch_norm_w, w_out,
                  ln2_w, w_up, w_down):
    depth = w_in.shape[0]
    fg0 = 3 * ATT_DIM
    w_re = jnp.concatenate([w_in[:, :, :fg0], w_in[:, :, fg0 + N_HEADS:], w_in[:, :, fg0:fg0 + N_HEADS],
                            jnp.zeros((depth, D_MODEL, LANES - N_HEADS), F32)], axis=2).astype(BF16)
    head_id = jnp.arange(ATT_DIM) // HEAD_DIM
    groups = depth * SSM_GROUPS
    a_re, a_im, bb_re, bb_im = _s5_prep(
        ssm_lam_re.reshape(groups, SSM_STATE), ssm_lam_im.reshape(groups, SSM_STATE), ssm_log_dt.reshape(groups),
        ssm_b_re.reshape(groups, SSM_STATE, SSM_GROUP), ssm_b_im.reshape(groups, SSM_STATE, SSM_GROUP))
    per_layer = lambda t: t.reshape(depth, SSM_GROUPS, SSM_GROUP, SSM_STATE)
    row = lambda t, w: t.reshape(depth, 1, w)
    return dict(
        ln1=row(ln1_w, D_MODEL), w_in=w_re,
        qn=row(jnp.tile(q_norm_w, (1, N_HEADS)), ATT_DIM), kn=row(jnp.tile(k_norm_w, (1, N_HEADS)), ATT_DIM),
        bf=row(jnp.pad(b_forget, ((0, 0), (0, LANES - N_HEADS))), LANES),
        mbd=(head_id[:, None] == head_id[None, :]).astype(BF16),
        a_re=row(a_re, STATE_LANES), a_im=row(a_im, STATE_LANES),
        b_re_blk=_block_diag(per_layer(bb_re)).astype(BF16), b_im_blk=_block_diag(per_layer(bb_im)).astype(BF16),
        c_re_blk=jnp.swapaxes(_block_diag(ssm_c_re), 1, 2).astype(BF16),
        c_im_blk=jnp.swapaxes(_block_diag(ssm_c_im), 1, 2).astype(BF16),
        d=row(ssm_d, SSM_DIM), w_glu=w_glu.astype(BF16), b_glu=row(b_glu, SSM_DIM),
        conv_w=conv_w, bn=row(branch_norm_w, D_MODEL), w_out=w_out.astype(BF16),
        ln2=row(ln2_w, D_MODEL), w_up=w_up.astype(BF16), w_down=w_down.astype(BF16))


def _run_layer(x, lw, layer, depth, prev, past, tiles):
    bsz, seq, _ = x.shape
    if past is None:
        c0 = jnp.zeros((bsz, 1, LANES), F32)
        h0r = jnp.zeros((bsz, STATE_LANES), F32)
        h0i = h0r
        cprev = jnp.zeros((bsz, CONV_WIDTH - 1, CONV_DIM), F32)
        qx, kx, vb, k_all, v_all, lft, u, z, gb = _inproj(x, c0, lw, tiles["tok"], layer, depth, prev, True)
        att = _attn_prompt(qx, kx, vb, tiles["attn"], tiles["attn_pairs"])
    else:
        cache_kt, cache_vt, clf_t, s_re, s_im, s_conv = past
        plen = cache_kt.shape[-1]
        ct_cache = _cache_cumsum(clf_t.reshape(bsz * N_HEADS, plen)).reshape(bsz, N_HEADS, plen)
        c0 = jnp.pad(ct_cache[:, :, plen - 1], ((0, 0), (0, LANES - N_HEADS))).reshape(bsz, 1, LANES)
        h0r = s_re.reshape(bsz, STATE_LANES)
        h0i = s_im.reshape(bsz, STATE_LANES)
        cprev = s_conv
        q, kb, vb, k_all, v_all, lft, cq, ct, u, z, gb = _inproj(x, c0, lw, tiles["tok"], layer, depth, prev, False)
        att = _attn_sample(q, kb, vb, cq, ct, cache_kt, cache_vt, ct_cache, layer)
    ssm, conv, hr, hi, cst = _ssm_conv(u, z, gb, h0r, h0i, cprev, lw, layer, tiles["scan"])
    y = _mix_ffn(x, att, ssm, conv, lw, layer, tiles["ffn"])
    small = (lft, hr.reshape(bsz, SSM_GROUPS, SSM_STATE), hi.reshape(bsz, SSM_GROUPS, SSM_STATE), cst)
    return y, (k_all, v_all), small


def _tiles(bsz, seq):
    tok = min(seq, 512)
    return dict(tok=tok, attn=min(seq, 512), attn_pairs=2, scan=min(seq, 128), ffn=min(bsz * seq, 512))


def _forward(x_prompt, x_sample, cache_k, cache_v, cache_logf, state_ssm_re, state_ssm_im, state_conv, *weights):
    depth = cache_k.shape[0]
    xp, xs = x_prompt, x_sample
    bp, lp = xp.shape[0], xp.shape[1]
    tiles_p = _tiles(bp, lp)
    tiles_s = _tiles(xs.shape[0], xs.shape[1])
    cache_kt = jnp.transpose(cache_k, (0, 1, 3, 4, 2))
    cache_vt = jnp.transpose(cache_v, (0, 1, 3, 4, 2))
    cache_lft = jnp.swapaxes(cache_logf, 2, 3)
    kv_p, kv_s, small_p, small_s = None, None, [], []
    lw = _prep_weights(*weights)
    for l in range(depth):
        xp, kv_p, sp = _run_layer(xp, lw, l, depth, kv_p, None, tiles_p)
        xs, kv_s, ss = _run_layer(xs, lw, l, depth, kv_s,
                                  (cache_kt, cache_vt, cache_lft[l], state_ssm_re[l], state_ssm_im[l], state_conv[l]),
                                  tiles_s)
        small_p.append(sp)
        small_s.append(ss)
    stk = lambda lst, i: jnp.stack([s[i] for s in lst])
    from_t = lambda t: jnp.transpose(t.reshape(depth, bp, N_HEADS, HEAD_DIM, lp), (0, 1, 4, 2, 3))
    return (xp, xs,
            from_t(kv_p[0]), from_t(kv_p[1]), jnp.swapaxes(stk(small_p, 0), 2, 3),
            stk(small_p, 1), stk(small_p, 2), stk(small_p, 3),
            kv_s[0], kv_s[1], jnp.swapaxes(stk(small_s, 0), 2, 3),
            stk(small_s, 1), stk(small_s, 2), stk(small_s, 3))


def kernel(x_prompt, x_sample, cache_k, cache_v, cache_logf, state_ssm_re, state_ssm_im, state_conv, ln1_w, w_in, b_forget, q_norm_w, k_norm_w, conv_w, ssm_lam_re, ssm_lam_im, ssm_log_dt, ssm_b_re, ssm_b_im, ssm_c_re, ssm_c_im, ssm_d, w_glu, b_glu, branch_norm_w, w_out, ln2_w, w_up, w_down):
    return _forward(x_prompt, x_sample, cache_k, cache_v, cache_logf, state_ssm_re, state_ssm_im, state_conv,
                    ln1_w, w_in, b_forget, q_norm_w, k_norm_w, conv_w, ssm_lam_re, ssm_lam_im, ssm_log_dt,
                    ssm_b_re, ssm_b_im, ssm_c_re, ssm_c_im, ssm_d, w_glu, b_glu, branch_norm_w, w_out,
                    ln2_w, w_up, w_down)
```

```python
import functools
import math

import jax
import jax.numpy as jnp
from jax import lax
from jax.experimental import pallas as pl
from jax.experimental.pallas import tpu as pltpu

F32 = jnp.float32
BF16 = jnp.bfloat16
HIGHEST = lax.Precision.HIGHEST

D_MODEL = 1024
N_HEADS = 8
HEAD_DIM = 64
ATT_DIM = N_HEADS * HEAD_DIM
SSM_DIM = 256
SSM_GROUP = 16
SSM_GROUPS = SSM_DIM // SSM_GROUP
SSM_STATE = 64
STATE_LANES = SSM_GROUPS * SSM_STATE
CONV_DIM = 256
CONV_WIDTH = 3
D_FF = 4 * D_MODEL
RMS_EPS = 1e-6
NEG_INF = -1e30
LOG2E = math.log2(math.e)

LANES = 128
SUBLANES = 8
HEAD_PAIRS = ATT_DIM // LANES
PAIR_EXT = 2 * LANES
AUG_STRIDE = 8
BIAS_PIECES = 3
PROJ_PAD = 3 * ATT_DIM + SSM_DIM + 3 * CONV_DIM + LANES
VMEM_LIMIT = 56 * 1024 * 1024


def _params(*sem):
    return pltpu.CompilerParams(dimension_semantics=sem, vmem_limit_bytes=VMEM_LIMIT)


def _const_spec(shape):
    zeros = (0,) * len(shape)
    return pl.BlockSpec(shape, lambda *_: zeros)


def _layer_spec(shape, layer, **kw):
    index = (layer,) + (0,) * len(shape)
    return pl.BlockSpec((None,) + tuple(shape), lambda *_: index, **kw)


def _nt_dot(a, b, **kw):
    return lax.dot_general(a, b, (((1,), (1,)), ((), ())), preferred_element_type=F32, **kw)


def _bias_selectors():
    h = jnp.arange(N_HEADS)
    base = LANES * (h // 2) + AUG_STRIDE * (h % 2)
    sq = jnp.zeros((ATT_DIM, LANES), F32)
    sk = jnp.zeros((LANES, ATT_DIM), F32)
    for r in range(BIAS_PIECES):
        sq = sq.at[base + r, N_HEADS * r + h].set(1.0)
        sk = sk.at[N_HEADS * r + h, base + BIAS_PIECES + r].set(-1.0)
    return sq.astype(BF16), sk.astype(BF16)


def _split3(t):
    p1 = t.astype(BF16)
    r1 = t - p1.astype(F32)
    p2 = r1.astype(BF16)
    p3 = (r1 - p2.astype(F32)).astype(BF16)
    return p1, p2, p3


def _inproj_body(*refs, cum_block, prompt, n_prev):
    (x_ref, ln1_ref, w_ref, qnw_ref, knw_ref, bf_ref, mbd_ref, tri_ref, sel_ref, c0_ref, sq_ref, sk_ref) = refs[:12]
    outs = refs[12 + n_prev:]
    carry_ref = outs[-1]
    i = pl.program_id(1)
    x = x_ref[...]
    ms = jnp.mean(x * x, axis=-1, keepdims=True)
    h = (x * lax.rsqrt(ms + RMS_EPS) * ln1_ref[...]).astype(BF16)
    proj = jnp.dot(h, w_ref[...], preferred_element_type=F32)
    tb = x.shape[0]

    def head_norm(t, w):
        ss = jnp.dot((t * t).astype(BF16), mbd_ref[...], preferred_element_type=F32)
        return t * lax.rsqrt(ss * (1.0 / HEAD_DIM) + RMS_EPS) * w

    q = head_norm(proj[:, 0:ATT_DIM], qnw_ref[...]) * (LOG2E * HEAD_DIM ** -0.5)
    k = head_norm(proj[:, ATT_DIM:2 * ATT_DIM], knw_ref[...])
    v = proj[:, 2 * ATT_DIM:3 * ATT_DIM]
    o = 3 * ATT_DIM
    u = proj[:, o:o + SSM_DIM]
    hc = proj[:, o + SSM_DIM:o + SSM_DIM + CONV_DIM]
    gb = proj[:, o + SSM_DIM + CONV_DIM:o + SSM_DIM + 2 * CONV_DIM]
    gc = proj[:, o + SSM_DIM + 2 * CONV_DIM:o + SSM_DIM + 3 * CONV_DIM]
    lf = jax.nn.log_sigmoid(proj[:, PROJ_PAD - LANES:PROJ_PAD] + bf_ref[...])

    @pl.when(i == 0)
    def _():
        carry_ref[...] = c0_ref[...]

    carry = carry_ref[...]
    pieces = []
    for s in range(tb // cum_block):
        blk = jnp.concatenate(_split3(lf[s * cum_block:(s + 1) * cum_block]), axis=-1)
        r = jnp.dot(tri_ref[...], blk, preferred_element_type=F32)
        c = r[:, 0:LANES] + r[:, LANES:2 * LANES] + r[:, 2 * LANES:3 * LANES] + carry
        carry = c[cum_block - 1:cum_block]
        pieces.append(c)
    c_all = pieces[0] if len(pieces) == 1 else jnp.concatenate(pieces, axis=0)
    carry_ref[...] = carry
    c2 = c_all * LOG2E
    lf_t = _nt_dot(sel_ref[...], lf, precision=HIGHEST)

    if prompt:
        qx_ref, kx_ref, vb_ref, kt_ref, vt_ref, lft_ref, u_ref, z_ref, gb_ref = outs[:-1]
        head_lane = lax.broadcasted_iota(jnp.int32, (tb, LANES), 1) < N_HEADS
        p1, p2, p3 = (p.astype(F32) for p in _split3(jnp.where(head_lane, c2, 0.0)))
        pcs = (p1 + pltpu.roll(p2, N_HEADS, 1) + pltpu.roll(p3, 2 * N_HEADS, 1)).astype(BF16)
        row = lax.broadcasted_iota(jnp.int32, (ATT_DIM, tb), 0) % LANES
        q_one = (row < 2 * AUG_STRIDE) & (row % AUG_STRIDE >= BIAS_PIECES) & (row % AUG_STRIDE < 2 * BIAS_PIECES)
        qa_t = _nt_dot(sq_ref[...], pcs) + q_one.astype(F32)
        lane = lax.broadcasted_iota(jnp.int32, (tb, ATT_DIM), 1) % LANES
        k_one = (lane < 2 * AUG_STRIDE) & (lane % AUG_STRIDE < BIAS_PIECES)
        ka = jnp.dot(pcs, sk_ref[...], preferred_element_type=F32) + k_one.astype(F32)
        k_t, v_t = k.T, v.T
        qtb, qatb, kb, kab = q.T.astype(BF16), qa_t.astype(BF16), k.astype(BF16), ka.astype(BF16)
        qx, kx = [], []
        for p in range(HEAD_PAIRS):
            sl = slice(p * LANES, (p + 1) * LANES)
            qx += [qtb[sl], qatb[sl]]
            kx += [kb[:, sl], kab[:, sl]]
        qx_ref[...] = jnp.concatenate(qx, axis=0)
        kx_ref[...] = jnp.concatenate(kx, axis=-1)
        vb_ref[...] = v_t.astype(BF16)
        kt_ref[...] = k_t
        vt_ref[...] = v_t
    else:
        q_ref, kb_ref, vb_ref, kf_ref, vf_ref, lft_ref, cq_ref, ct_ref, u_ref, z_ref, gb_ref = outs[:-1]
        q_ref[...] = q.astype(BF16)
        kb_ref[...] = k.astype(BF16)
        vb_ref[...] = v.astype(BF16)
        kf_ref[...] = k.reshape(tb, N_HEADS, HEAD_DIM)
        vf_ref[...] = v.reshape(tb, N_HEADS, HEAD_DIM)
        cq_ref[...] = c2[:, 0:N_HEADS]
        ct_ref[...] = _nt_dot(sel_ref[...], c2, precision=HIGHEST)
    lft_ref[...] = lf_t
    u_ref[...] = u
    z_ref[...] = gc * hc
    gb_ref[...] = gb


def _inproj(x, c0, lw, tb, layer, depth, prev, prompt):
    bsz, seq, _ = x.shape
    cum_block = min(tb, LANES)
    tri = (lax.broadcasted_iota(jnp.int32, (cum_block, cum_block), 1)
           <= lax.broadcasted_iota(jnp.int32, (cum_block, cum_block), 0)).astype(BF16)
    sel =(lax.broadcasted_iota(jnp.int32, (N_HEADS, LANES), 0)
           == lax.broadcasted_iota(jnp.int32, (N_HEADS, LANES), 1)).astype(F32)
    sq, sk = _bias_selectors()
    tok = lambda w: pl.BlockSpec((None, tb, w), lambda b, i: (b, i, 0))
    head_major = pl.BlockSpec((None, N_HEADS, tb), lambda b, i: (b, 0, i))
    tok_shape = lambda w, dt: jax.ShapeDtypeStruct((bsz, seq, w), dt)
    heads_shape = jax.ShapeDtypeStruct((bsz, N_HEADS, seq), F32)
    if prompt:
        state_shape = jax.ShapeDtypeStruct((depth, bsz, ATT_DIM, seq), F32)
        state_spec = pl.BlockSpec((None, None, ATT_DIM, tb), lambda b, i: (layer, b, 0, i))
        rows_major = lambda r: pl.BlockSpec((None, r, tb), lambda b, i: (b, 0, i))
        out_shapes = (jax.ShapeDtypeStruct((bsz, HEAD_PAIRS * PAIR_EXT, seq), BF16),
                      tok_shape(HEAD_PAIRS * PAIR_EXT, BF16),
                      jax.ShapeDtypeStruct((bsz, ATT_DIM, seq), BF16),
                      state_shape, state_shape, heads_shape,
                      tok_shape(SSM_DIM, F32), tok_shape(CONV_DIM, F32), tok_shape(CONV_DIM, F32))
        out_specs = (rows_major(HEAD_PAIRS * PAIR_EXT), tok(HEAD_PAIRS * PAIR_EXT), rows_major(ATT_DIM),
                     state_spec, state_spec, head_major,
                     tok(SSM_DIM), tok(CONV_DIM), tok(CONV_DIM))
        state_out = (3, 4)
    else:
        state_shape = jax.ShapeDtypeStruct((depth, bsz, seq, N_HEADS, HEAD_DIM), F32)
        state_spec = pl.BlockSpec((None, None, tb, N_HEADS, HEAD_DIM), lambda b, i: (layer, b, i, 0, 0))
        out_shapes = (tok_shape(ATT_DIM, BF16), tok_shape(ATT_DIM, BF16), tok_shape(ATT_DIM, BF16),
                      state_shape, state_shape, heads_shape,
                      tok_shape(N_HEADS, F32), heads_shape,
                      tok_shape(SSM_DIM, F32), tok_shape(CONV_DIM, F32), tok_shape(CONV_DIM, F32))
        out_specs = (tok(ATT_DIM), tok(ATT_DIM), tok(ATT_DIM), state_spec, state_spec, head_major,
                     tok(N_HEADS), head_major, tok(SSM_DIM), tok(CONV_DIM), tok(CONV_DIM))
        state_out = (3, 4)
    in_specs = [tok(D_MODEL), _layer_spec((1, D_MODEL), layer), _layer_spec((D_MODEL, PROJ_PAD), layer),
                _layer_spec((1, ATT_DIM), layer), _layer_spec((1, ATT_DIM), layer), _layer_spec((1, LANES), layer),
                _const_spec((ATT_DIM, ATT_DIM)), _const_spec((cum_block, cum_block)),
                _const_spec((N_HEADS, LANES)),
                pl.BlockSpec((None, 1, LANES), lambda b, i: (b, 0, 0)),
                _const_spec(sq.shape), _const_spec(sk.shape)]
    args = [x, lw["ln1"], lw["w_in"], lw["qn"], lw["kn"], lw["bf"], lw["mbd"], tri, sel, c0, sq, sk]
    aliases = {}
    n_prev = 0
    if prev is not None:
        n_prev = len(prev)
        for n, buf in enumerate(prev):
            aliases[len(args)] = state_out[n]
            args.append(buf)
            in_specs.append(pl.BlockSpec(memory_space=pl.ANY))
    return pl.pallas_call(
        functools.partial(_inproj_body, cum_block=cum_block, prompt=prompt, n_prev=n_prev),
        grid=(bsz, seq // tb),
        in_specs=in_specs, out_specs=out_specs, out_shape=out_shapes,
        scratch_shapes=[pltpu.VMEM((1, LANES), F32)],
        input_output_aliases=aliases,
        compiler_params=_params("arbitrary", "arbitrary"),
        name="inproj_prompt" if prompt else "inproj_sample",
    )(*args)


def _attn_body(qx_ref, kx_ref, vt_ref, o_ref, m_s, mb_s, acc_s, s_s, *, tq, pairs):
    nq = o_ref.shape[0] // tq
    row_x = lax.broadcasted_iota(jnp.int32, (PAIR_EXT, tq), 0)
    row_v = lax.broadcasted_iota(jnp.int32, (LANES, tq), 0)
    head_rows = [(row_v >= HEAD_DIM * a) & (row_v < HEAD_DIM * (a + 1)) for a in range(2)]
    mine = [(((row_x >= HEAD_DIM * a) & (row_x < HEAD_DIM * (a + 1)))
             | ((row_x >= LANES + AUG_STRIDE * a) & (row_x < LANES + AUG_STRIDE * (a + 1)))) for a in range(2)]
    half = tq // 2
    causal_top = (lax.broadcasted_iota(jnp.int32, (half, tq), 0) <= lax.broadcasted_iota(jnp.int32, (half, tq), 1))
    causal_bot = (lax.broadcasted_iota(jnp.int32, (half, half), 0) <= lax.broadcasted_iota(jnp.int32, (half, half), 1))

    def col_max(s):
        return jnp.max(s, axis=0, keepdims=True)

    def scores(qi, j, p, diagonal=False):
        cols = pl.ds(pl.multiple_of(qi * tq, tq), tq)
        qt = qx_ref[p * PAIR_EXT:(p + 1) * PAIR_EXT, cols]
        rows = pl.ds(pl.multiple_of(j * tq, tq), tq)
        kx = kx_ref[rows, p * PAIR_EXT:(p + 1) * PAIR_EXT]
        for a in range(2):
            h = 2 * p + a
            qa = jnp.where(mine[a], qt, jnp.zeros_like(qt))
            if diagonal:
                s_s[h, 0:half, :] = jnp.dot(kx[0:half], qa, preferred_element_type=F32)
                s_s[h, half:tq, half:tq] = jnp.dot(kx[half:tq], qa[:, half:tq], preferred_element_type=F32)
            else:
                s = jnp.dot(kx, qa, preferred_element_type=F32)
                s_s[h] = s
                mb_s[h] = col_max(s)

    def absorb(j, p, diagonal):
        cols = pl.ds(pl.multiple_of(j * tq, tq), tq)
        vt = vt_ref[p * LANES:(p + 1) * LANES, cols]
        for a in range(2):
            h = 2 * p + a
            va = jnp.where(head_rows[a], vt, jnp.ones_like(vt))
            m_old = m_s[h]
            if diagonal:
                top = jnp.where(causal_top, s_s[h, 0:half, :], NEG_INF)
                bot = jnp.where(causal_bot, s_s[h, half:tq, half:tq], NEG_INF)
                m_top = col_max(top)
                m_blk = jnp.concatenate([m_top[:, 0:half], jnp.maximum(m_top[:, half:tq], col_max(bot))], axis=1)
                m_new = jnp.maximum(m_old, m_blk)
                out = jnp.dot(va[:, 0:half], jnp.exp2(top - m_new[0:1]).astype(BF16), preferred_element_type=F32)
                out_late = jnp.dot(va[:, half:tq], jnp.exp2(bot - m_new[0:1, half:tq]).astype(BF16),
                                   preferred_element_type=F32)
                out = jnp.concatenate([out[:, 0:half], out[:, half:tq] + out_late], axis=1)
            else:
                m_new = jnp.maximum(m_old, mb_s[h])
                out = jnp.dot(va, jnp.exp2(s_s[h] - m_new[0:1]).astype(BF16), preferred_element_type=F32)
            acc_s[h] = jnp.exp2(m_old - m_new)[0:1] * acc_s[h] + out
            m_s[h] = m_new

    def reset():
        for h in range(2 * pairs):
            m_s[h] = jnp.full((1, tq), NEG_INF, F32)
            acc_s[h] = jnp.zeros((LANES, tq), F32)

    def key_block(qi, j, diagonal, nxt):
        for p in range(pairs):
            if p + 1 < pairs:
                scores(qi, j, p + 1, diagonal)
            else:
                scores(nxt[0], nxt[1], 0)
            absorb(j, p, diagonal)

    def query_block(qi, carry):
        def body(j, c):
            key_block(qi, j, False, (qi, j + 1))
            return c

        lax.fori_loop(0, qi, body, 0)
        key_block(qi, qi, True, (jnp.minimum(qi + 1, nq - 1), 0))
        rows = pl.ds(pl.multiple_of(qi * tq, tq), tq)
        for p in range(pairs):
            a0, a1 = acc_s[2 * p], acc_s[2 * p + 1]
            out_t = jnp.concatenate([a0[0:HEAD_DIM] / a0[HEAD_DIM:LANES],
                                     a1[HEAD_DIM:LANES] / a1[0:HEAD_DIM]], axis=0)
            o_ref[rows, p * LANES:(p + 1) * LANES] = out_t.T
        reset()
        return carry

    reset()
    scores(0, 0, 0)
    lax.fori_loop(0, nq, query_block, 0)


def _attn_prompt(qx_t, kx, vt, tq, pairs):
    bsz, seq, _ = kx.shape
    return pl.pallas_call(
        functools.partial(_attn_body, tq=tq, pairs=pairs),
        grid=(bsz, HEAD_PAIRS // pairs),
        in_specs=[pl.BlockSpec((None, pairs * PAIR_EXT, seq), lambda b, h: (b, h, 0)),
                  pl.BlockSpec((None, seq, pairs * PAIR_EXT), lambda b, h: (b, 0, h)),
                  pl.BlockSpec((None, pairs * LANES, seq), lambda b, h: (b, h, 0))],
        out_specs=pl.BlockSpec((None, seq, pairs * LANES), lambda b, h: (b, 0, h)),
        out_shape=jax.ShapeDtypeStruct((bsz, seq, ATT_DIM), F32),
        scratch_shapes=[pltpu.VMEM((2 * pairs, 1, tq), F32), pltpu.VMEM((2 * pairs, 1, tq), F32),
                        pltpu.VMEM((2 * pairs, LANES, tq), F32),
                        pltpu.VMEM((2 * pairs, tq, tq), F32)],
        compiler_params=_params("arbitrary", "arbitrary"),
        name="attn_prompt",
    )(qx_t, kx, vt)


def _attn_sample_body(q_ref, ckt_ref, cvt_ref, kn_ref, vn_ref, cq_ref, ctc_ref, ctn_ref, o_ref, *, pairs):
    tq = q_ref.shape[0]
    past = ckt_ref.shape[-1]
    cq8 = cq_ref[...]
    lane = lax.broadcasted_iota(jnp.int32, (tq, LANES), 1)
    lane8 = lax.broadcasted_iota(jnp.int32, (tq, N_HEADS), 1)
    row = lax.broadcasted_iota(jnp.int32, (tq, tq), 0)
    col = lax.broadcasted_iota(jnp.int32, (tq, tq), 1)
    for pp in range(pairs):
        hp = pl.program_id(1) * pairs + pp
        lanes = slice(pp * LANES, (pp + 1) * LANES)
        q = q_ref[:, lanes]
        q2 = jnp.concatenate(
            [jnp.where((lane >= HEAD_DIM * a) & (lane < HEAD_DIM * (a + 1)), q, jnp.zeros_like(q)) for a in range(2)],
            axis=0)
        kt = ckt_ref[2 * pp:2 * pp + 2].reshape(LANES, past).astype(BF16)
        vt = cvt_ref[2 * pp:2 * pp + 2].reshape(LANES, past).astype(BF16)
        s_cache = jnp.dot(q2, kt, preferred_element_type=F32)
        s_new = _nt_dot(q2, kn_ref[:, lanes])
        sc, sn = [], []
        for a in range(2):
            head = 2 * hp + a
            cqa = jnp.sum(jnp.where(lane8 == head, cq8, 0.0), axis=-1, keepdims=True)
            ck_cache = ctc_ref[pl.ds(head, 1), :] * LOG2E
            ck_new = ctn_ref[pl.ds(head, 1), :]
            sc.append(s_cache[a * tq:(a + 1) * tq] + cqa - ck_cache)
            sn.append(jnp.where(col <= row, s_new[a * tq:(a + 1) * tq] + cqa - ck_new, NEG_INF))
        s_cache = jnp.concatenate(sc, axis=0)
        s_new = jnp.concatenate(sn, axis=0)
        m = jnp.maximum(jnp.max(s_cache, axis=-1, keepdims=True), jnp.max(s_new, axis=-1, keepdims=True))
        p_cache = jnp.exp2(s_cache - m)
        p_new = jnp.exp2(s_new - m)
        l = jnp.sum(p_cache, axis=-1, keepdims=True) + jnp.sum(p_new, axis=-1, keepdims=True)
        o = (_nt_dot(p_cache.astype(BF16), vt)
             + jnp.dot(p_new.astype(BF16), vn_ref[:, lanes], preferred_element_type=F32)) / l
        o_ref[:, lanes] = jnp.where(lane < HEAD_DIM, o[0:tq], o[tq:2 * tq])


def _attn_sample(q, kb, vb, cq, ct_new, cache_kt, cache_vt, ct_cache, layer, pairs=2):
    bsz, seq, _ = q.shape
    past = cache_kt.shape[-1]
    cache_blk = pl.BlockSpec((None, None, 2 * pairs, HEAD_DIM, past), lambda b, h: (layer, b, h, 0, 0))
    new_blk = pl.BlockSpec((None, seq, pairs * LANES), lambda b, h: (b, 0, h))
    return pl.pallas_call(
        functools.partial(_attn_sample_body, pairs=pairs),
        grid=(bsz, HEAD_PAIRS // pairs),
        in_specs=[new_blk, cache_blk, cache_blk, new_blk, new_blk,
                  pl.BlockSpec((None, seq, N_HEADS), lambda b, h: (b, 0, 0)),
                  pl.BlockSpec((None, N_HEADS, past), lambda b, h: (b, 0, 0)),
                  pl.BlockSpec((None, N_HEADS, seq), lambda b, h: (b, 0, 0))],
        out_specs=new_blk,
        out_shape=jax.ShapeDtypeStruct((bsz, seq, ATT_DIM), F32),
        compiler_params=_params("arbitrary", "arbitrary"),
        name="attn_sample",
    )(q, cache_kt, cache_vt, kb, vb, cq, ct_cache, ct_new)


def _cache_cumsum_body(x_ref, o_ref):
    rows, past = x_ref.shape
    tri = (lax.broadcasted_iota(jnp.int32, (LANES, LANES), 0)
           <= lax.broadcasted_iota(jnp.int32, (LANES, LANES), 1)).astype(F32)
    carry = jnp.zeros((rows, 1), F32)
    for b in range(past // LANES):
        blk = x_ref[:, b * LANES:(b + 1) * LANES]
        c = jnp.dot(blk, tri, precision=HIGHEST, preferred_element_type=F32) + carry
        o_ref[:, b * LANES:(b + 1) * LANES] = c
        carry = c[:, LANES - 1:LANES]


def _cache_cumsum(x):
    return pl.pallas_call(
        _cache_cumsum_body,
        out_shape=jax.ShapeDtypeStruct(x.shape, F32),
        compiler_params=pltpu.CompilerParams(vmem_limit_bytes=VMEM_LIMIT),
        name="cache_cumsum",
    )(x)


def _s5_prep_body(lr_ref, li_ref, ldt_ref, br_ref, bi_ref, ar_ref, ai_ref, bbr_ref, bbi_ref):
    lr = jnp.minimum(lr_ref[...], -1e-4)
    li = li_ref[...]
    dt = jnp.exp(ldt_ref[...])
    ldr, ldi = lr * dt, li * dt
    mag = jnp.exp(ldr)
    ar, ai = mag * jnp.cos(ldi), mag * jnp.sin(ldi)
    den = lr * lr + li * li
    nr = ar - 1.0
    qr = (nr * lr + ai * li) / den
    qi = (ai * lr - nr * li) / den
    ar_ref[...] = ar
    ai_ref[...] = ai
    br, bi = br_ref[...], bi_ref[...]
    bbr_ref[...] = qr[:, None, :] * br - qi[:, None, :] * bi
    bbi_ref[...] = qr[:, None, :] * bi + qi[:, None, :] * br


def _s5_prep(lam_re, lam_im, log_dt, b_re, b_im):
    g, p = lam_re.shape
    gp = jax.ShapeDtypeStruct((g, p), F32)
    gcp = jax.ShapeDtypeStruct((g, SSM_GROUP, p), F32)
    return pl.pallas_call(
        _s5_prep_body, out_shape=(gp, gp, gcp, gcp), name="s5_prep",
    )(lam_re, lam_im, log_dt.reshape(g, 1), jnp.swapaxes(b_re, 1, 2), jnp.swapaxes(b_im, 1, 2))


def _block_diag(t):
    d, g, c, p = t.shape
    eye = jnp.eye(g, dtype=t.dtype)
    return (t[:, :, :, None, :] * eye[None, :, None, :, None]).reshape(d, g * c, g * p)


def _ssm_body(u_ref, z_ref, gb_ref, h0r_ref, h0i_ref, cprev_ref,
              bre_ref, bim_ref, cre_ref, cim_ref, ar_ref, ai_ref, d_ref, wglu_ref, bglu_ref, cw_ref,
              ssm_ref, conv_ref, hr_out, hi_out, cst_out,
              sre, sim, hr_s, hi_s, zprev_s, *, lb):
    j = pl.program_id(1)
    nb = SUBLANES
    tiles = STATE_LANES // LANES

    @pl.when(j == 0)
    def _():
        hr_s[...] = h0r_ref[...]
        hi_s[...] = h0i_ref[...]
        zprev_s[...] = cprev_ref[...]

    u2 = jnp.swapaxes(u_ref[...], 0, 1).reshape(lb * nb, SSM_DIM)
    ub = u2.astype(BF16)
    bu_re = jnp.dot(ub, bre_ref[...], preferred_element_type=F32)
    bu_im = jnp.dot(ub, bim_ref[...], preferred_element_type=F32)
    for c in range(tiles):
        sre[c] = bu_re[:, c * LANES:(c + 1) * LANES]
        sim[c] = bu_im[:, c * LANES:(c + 1) * LANES]
    ar = [jnp.broadcast_to(ar_ref[:, c * LANES:(c + 1) * LANES], (nb, LANES)) for c in range(tiles)]
    ai = [jnp.broadcast_to(ai_ref[:, c * LANES:(c + 1) * LANES], (nb, LANES)) for c in range(tiles)]

    def step(t, carry):
        hr, hi = carry
        rows = pl.ds(pl.multiple_of(t * nb, nb), nb)
        nr, ni = [], []
        for c in range(tiles):
            r = ar[c] * hr[c] - ai[c] * hi[c] + sre[c, rows, :]
            i = ar[c] * hi[c] + ai[c] * hr[c] + sim[c, rows, :]
            sre[c, rows, :] = r
            sim[c, rows, :] = i
            nr.append(r)
            ni.append(i)
        return tuple(nr), tuple(ni)

    init = (tuple(hr_s[:, c * LANES:(c + 1) * LANES] for c in range(tiles)),
            tuple(hi_s[:, c * LANES:(c + 1) * LANES] for c in range(tiles)))
    hr, hi = lax.fori_loop(0, lb, step, init, unroll=2)
    for c in range(tiles):
        hr_s[:, c * LANES:(c + 1) * LANES] = hr[c]
        hi_s[:, c * LANES:(c + 1) * LANES] = hi[c]
        hr_out[:, c * LANES:(c + 1) * LANES] = hr[c]
        hi_out[:, c * LANES:(c + 1) * LANES] = hi[c]

    h_re = jnp.concatenate([sre[c] for c in range(tiles)], axis=-1).astype(BF16)
    h_im = jnp.concatenate([sim[c] for c in range(tiles)], axis=-1).astype(BF16)
    y = (jnp.dot(h_re, cre_ref[...], preferred_element_type=F32)
         - jnp.dot(h_im, cim_ref[...], preferred_element_type=F32))
    y = y + d_ref[...] * u2
    g = jax.nn.gelu(y)
    gate = jax.nn.sigmoid(jnp.dot(g.astype(BF16), wglu_ref[...], preferred_element_type=F32) + bglu_ref[...])
    ssm_ref[...] = jnp.swapaxes((g * gate).reshape(lb, nb, SSM_DIM), 0, 1)

    tt = lax.broadcasted_iota(jnp.int32, (lb, CONV_DIM), 0)
    w0, w1, w2 = cw_ref[0:1, :], cw_ref[1:2, :], cw_ref[2:3, :]
    for b in range(nb):
        zb = z_ref[b]
        prev = zprev_s[b]
        z1 = jnp.where(tt == 0, prev[1:2], pltpu.roll(zb, 1, 0))
        z2 = jnp.where(tt == 0, prev[0:1], jnp.where(tt == 1, prev[1:2], pltpu.roll(zb, 2, 0)))
        conv_ref[b] = gb_ref[b] * (w0 * z2 + w1 * z1 + w2 * zb)
        last = zb[lb - (CONV_WIDTH - 1):lb]
        zprev_s[b] = last
        cst_out[b] = last


def _ssm_conv(u, z, gb, h0r, h0i, cprev, lw, layer, lb):
    bsz, seq, _ = u.shape
    nb = SUBLANES
    tok = pl.BlockSpec((nb, lb, SSM_DIM), lambda g, j: (g, j, 0))
    st = pl.BlockSpec((nb, STATE_LANES), lambda g, j: (g, 0))
    cs = pl.BlockSpec((nb, CONV_WIDTH - 1, CONV_DIM), lambda g, j: (g, 0, 0))
    par = lambda *shape: _layer_spec(shape, layer)
    return pl.pallas_call(
        functools.partial(_ssm_body, lb=lb),
        grid=(bsz // nb, seq // lb),
        in_specs=[tok, tok, tok, st, st, cs,
                  par(SSM_DIM, STATE_LANES), par(SSM_DIM, STATE_LANES),
                  par(STATE_LANES, SSM_DIM), par(STATE_LANES, SSM_DIM),
                  par(1, STATE_LANES), par(1, STATE_LANES),
                  par(1, SSM_DIM), par(SSM_DIM, SSM_DIM), par(1, SSM_DIM),
                  par(CONV_WIDTH, CONV_DIM)],
        out_specs=(tok, tok, st, st, cs),
        out_shape=(jax.ShapeDtypeStruct((bsz, seq, SSM_DIM), F32),
                   jax.ShapeDtypeStruct((bsz, seq, CONV_DIM), F32),
                   jax.ShapeDtypeStruct((bsz, STATE_LANES), F32),
                   jax.ShapeDtypeStruct((bsz, STATE_LANES), F32),
                   jax.ShapeDtypeStruct((bsz, CONV_WIDTH - 1, CONV_DIM), F32)),
        scratch_shapes=[pltpu.VMEM((STATE_LANES // LANES, nb * lb, LANES), F32),
                        pltpu.VMEM((STATE_LANES // LANES, nb * lb, LANES), F32),
                        pltpu.VMEM((nb, STATE_LANES), F32), pltpu.VMEM((nb, STATE_LANES), F32),
                        pltpu.VMEM((nb, CONV_WIDTH - 1, CONV_DIM), F32)],
        compiler_params=_params("arbitrary", "arbitrary"),
        name="ssm_conv",
    )(u, z, gb, h0r, h0i, cprev, lw["b_re_blk"], lw["b_im_blk"], lw["c_re_blk"], lw["c_im_blk"],
      lw["a_re"], lw["a_im"], lw["d"], lw["w_glu"], lw["b_glu"], lw["conv_w"])


def _rms_rows(t, w):
    return t * lax.rsqrt(jnp.mean(t * t, axis=-1, keepdims=True) + RMS_EPS) * w


def _mix_ffn_body(x_ref, att_ref, ssm_ref, conv_ref, bn_ref, wout_ref, ln2_ref, wup_ref, wdown_ref, o_ref,
                  *, ff_chunk):
    bn = bn_ref[...]
    mix = jnp.concatenate(
        [_rms_rows(att_ref[...], bn[:, 0:ATT_DIM]),
         _rms_rows(ssm_ref[...], bn[:, ATT_DIM:ATT_DIM + SSM_DIM]),
         _rms_rows(conv_ref[...], bn[:, ATT_DIM + SSM_DIM:])], axis=-1).astype(BF16)
    x1 = x_ref[...] + jnp.dot(mix, wout_ref[...], preferred_element_type=F32)
    h2 = _rms_rows(x1, ln2_ref[...]).astype(BF16)
    acc = x1
    for c in range(D_FF // ff_chunk):
        f = jnp.maximum(jnp.dot(h2, wup_ref[:, c * ff_chunk:(c + 1) * ff_chunk], preferred_element_type=F32), 0.0)
        acc = acc + jnp.dot((f * f).astype(BF16), wdown_ref[c * ff_chunk:(c + 1) * ff_chunk, :],
                            preferred_element_type=F32)
    o_ref[...] = acc


def _mix_ffn(x, att, ssm, conv, lw, layer, tb):
    bsz, seq, _ = x.shape
    rows = bsz * seq
    tok = lambda w: pl.BlockSpec((tb, w), lambda i: (i, 0))
    once = lambda *shape: _layer_spec(shape, layer, pipeline_mode=pl.Buffered(1))
    out = pl.pallas_call(
        functools.partial(_mix_ffn_body, ff_chunk=1024),
        grid=(rows // tb,),
        in_specs=[tok(D_MODEL), tok(ATT_DIM), tok(SSM_DIM), tok(CONV_DIM),
                  once(1, D_MODEL), once(D_MODEL, D_MODEL), once(1, D_MODEL),
                  once(D_MODEL, D_FF), once(D_FF, D_MODEL)],
        out_specs=tok(D_MODEL),
        out_shape=jax.ShapeDtypeStruct((rows, D_MODEL), F32),
        compiler_params=_params("arbitrary"),
        name="mix_ffn",
    )(x.reshape(rows, D_MODEL), att.reshape(rows, ATT_DIM), ssm.reshape(rows, SSM_DIM),
      conv.reshape(rows, CONV_DIM), lw["bn"], lw["w_out"], lw["ln2"], lw["w_up"], lw["w_down"])
    return out.reshape(bsz, seq, D_MODEL)


def _prep_weights(ln1_w, w_in, b_forget, q_norm_w, k_norm_w, conv_w, ssm_lam_re, ssm_lam_im, ssm_log_dt,
                  ssm_b_re, ssm_b_im, ssm_c_re, ssm_c_im, ssm_d, w_glu, b_glu, branch_norm_w, w_out,
                  ln2_w, w_up, w_down):
    depth = w_in.shape[0]
    fg0 = 3 * ATT_DIM
    w_re = jnp.concatenate([w_in[:, :, :fg0], w_in[:, :, fg0 + N_HEADS:], w_in[:, :, fg0:fg0 + N_HEADS],
                            jnp.zeros((depth, D_MODEL, LANES - N_HEADS), F32)], axis=2).astype(BF16)
    head_id = jnp.arange(ATT_DIM) // HEAD_DIM
    groups = depth * SSM_GROUPS
    a_re, a_im, bb_re, bb_im = _s5_prep(
        ssm_lam_re.reshape(groups, SSM_STATE), ssm_lam_im.reshape(groups, SSM_STATE), ssm_log_dt.reshape(groups),
        ssm_b_re.reshape(groups, SSM_STATE, SSM_GROUP), ssm_b_im.reshape(groups, SSM_STATE, SSM_GROUP))
    per_layer = lambda t: t.reshape(depth, SSM_GROUPS, SSM_GROUP, SSM_STATE)
    row = lambda t, w: t.reshape(depth, 1, w)
    return dict(
        ln1=row(ln1_w, D_MODEL), w_in=w_re,
        qn=row(jnp.tile(q_norm_w, (1, N_HEADS)), ATT_DIM), kn=row(jnp.tile(k_norm_w, (1, N_HEADS)), ATT_DIM),
        bf=row(jnp.pad(b_forget, ((0, 0), (0, LANES - N_HEADS))), LANES),
        mbd=(head_id[:, None] == head_id[None, :]).astype(BF16),
        a_re=row(a_re, STATE_LANES), a_im=row(a_im, STATE_LANES),
        b_re_blk=_block_diag(per_layer(bb_re)).astype(BF16), b_im_blk=_block_diag(per_layer(bb_im)).astype(BF16),
        c_re_blk=jnp.swapaxes(_block_diag(ssm_c_re), 1, 2).astype(BF16),
        c_im_blk=jnp.swapaxes(_block_diag(ssm_c_im), 1, 2).astype(BF16),
        d=row(ssm_d, SSM_DIM), w_glu=w_glu.astype(BF16), b_glu=row(b_glu, SSM_DIM),
        conv_w=conv_w, bn=row(branch_norm_w, D_MODEL), w_out=w_out.astype(BF16),
        ln2=row(ln2_w, D_MODEL), w_up=w_up.astype(BF16), w_down=w_down.astype(BF16))


def _run_layer(x, lw, layer, depth, prev, past, tiles):
    bsz, seq, _ = x.shape
    if past is None:
        c0 = jnp.zeros((bsz, 1, LANES), F32)
        h0r = jnp.zeros((bsz, STATE_LANES), F32)
        h0i = h0r
        cprev = jnp.zeros((bsz, CONV_WIDTH - 1, CONV_DIM), F32)
        qx, kx, vb, k_all, v_all, lft, u, z, gb = _inproj(x, c0, lw, tiles["tok"], layer, depth, prev, True)
        att = _attn_prompt(qx, kx, vb, tiles["attn"], tiles["attn_pairs"])
    else:
        cache_kt, cache_vt, clf_t, s_re, s_im, s_conv = past
        plen = cache_kt.shape[-1]
        ct_cache = _cache_cumsum(clf_t.reshape(bsz * N_HEADS, plen)).reshape(bsz, N_HEADS, plen)
        c0 = jnp.pad(ct_cache[:, :, plen - 1], ((0, 0), (0, LANES - N_HEADS))).reshape(bsz, 1, LANES)
        h0r = s_re.reshape(bsz, STATE_LANES)
        h0i = s_im.reshape(bsz, STATE_LANES)
        cprev = s_conv
        q, kb, vb, k_all, v_all, lft, cq, ct, u, z, gb = _inproj(x, c0, lw, tiles["tok"], layer, depth, prev, False)
        att = _attn_sample(q, kb, vb, cq, ct, cache_kt, cache_vt, ct_cache, layer)
    ssm, conv, hr, hi, cst = _ssm_conv(u, z, gb, h0r, h0i, cprev, lw, layer, tiles["scan"])
    y = _mix_ffn(x, att, ssm, conv, lw, layer, tiles["ffn"])
    small = (lft, hr.reshape(bsz, SSM_GROUPS, SSM_STATE), hi.reshape(bsz, SSM_GROUPS, SSM_STATE), cst)
    return y, (k_all, v_all), small


def _tiles(bsz, seq):
    tok = min(seq, 512)
    return dict(tok=tok, attn=min(seq, 512), attn_pairs=2, scan=min(seq, 128), ffn=min(bsz * seq, 512))


def _forward(x_prompt, x_sample, cache_k, cache_v, cache_logf, state_ssm_re, state_ssm_im, state_conv, *weights):
    depth = cache_k.shape[0]
    xp, xs = x_prompt, x_sample
    bp, lp = xp.shape[0], xp.shape[1]
    tiles_p = _tiles(bp, lp)
    tiles_s = _tiles(xs.shape[0], xs.shape[1])
    cache_kt = jnp.transpose(cache_k, (0, 1, 3, 4, 2))
    cache_vt = jnp.transpose(cache_v, (0, 1, 3, 4, 2))
    cache_lft = jnp.swapaxes(cache_logf, 2, 3)
    kv_p, kv_s, small_p, small_s = None, None, [], []
    lw = _prep_weights(*weights)
    for l in range(depth):
        xp, kv_p, sp = _run_layer(xp, lw, l, depth, kv_p, None, tiles_p)
        xs, kv_s, ss = _run_layer(xs, lw, l, depth, kv_s,
                                  (cache_kt, cache_vt, cache_lft[l], state_ssm_re[l], state_ssm_im[l], state_conv[l]),
                                  tiles_s)
        small_p.append(sp)
        small_s.append(ss)
    stk = lambda lst, i: jnp.stack([s[i] for s in lst])
    from_t = lambda t: jnp.transpose(t.reshape(depth, bp, N_HEADS, HEAD_DIM, lp), (0, 1, 4, 2, 3))
    return (xp, xs,
            from_t(kv_p[0]), from_t(kv_p[1]), jnp.swapaxes(stk(small_p, 0), 2, 3),
            stk(small_p, 1), stk(small_p, 2), stk(small_p, 3),
            kv_s[0], kv_s[1], jnp.swapaxes(stk(small_s, 0), 2, 3),
            stk(small_s, 1), stk(small_s, 2), stk(small_s, 3))


def kernel(x_prompt, x_sample, cache_k, cache_v, cache_logf, state_ssm_re, state_ssm_im, state_conv, ln1_w, w_in, b_forget, q_norm_w, k_norm_w, conv_w, ssm_lam_re, ssm_lam_im, ssm_log_dt, ssm_b_re, ssm_b_im, ssm_c_re, ssm_c_im, ssm_d, w_glu, b_glu, branch_norm_w, w_out, ln2_w, w_up, w_down):
    return _forward(x_prompt, x_sample, cache_k, cache_v, cache_logf, state_ssm_re, state_ssm_im, state_conv,
                    ln1_w, w_in, b_forget, q_norm_w, k_norm_w, conv_w, ssm_lam_re, ssm_lam_im, ssm_log_dt,
                    ssm_b_re, ssm_b_im, ssm_c_re, ssm_c_im, ssm_d, w_glu, b_glu, branch_norm_w, w_out,
                    ln2_w, w_up, w_down)
```

```python
import functools
import math

import jax
import jax.numpy as jnp
from jax import lax
from jax.experimental import pallas as pl
from jax.experimental.pallas import tpu as pltpu

F32 = jnp.float32
BF16 = jnp.bfloat16
HIGHEST = lax.Precision.HIGHEST

D_MODEL = 1024
N_HEADS = 8
HEAD_DIM = 64
ATT_DIM = N_HEADS * HEAD_DIM
SSM_DIM = 256
SSM_GROUP = 16
SSM_GROUPS = SSM_DIM // SSM_GROUP
SSM_STATE = 64
STATE_LANES = SSM_GROUPS * SSM_STATE
CONV_DIM = 256
CONV_WIDTH = 3
D_FF = 4 * D_MODEL
RMS_EPS = 1e-6
NEG_INF = -1e30
LOG2E = math.log2(math.e)

LANES = 128
SUBLANES = 8
HEAD_PAIRS = ATT_DIM // LANES
PAIR_EXT = 2 * LANES
AUG_STRIDE = 8
BIAS_PIECES = 3
PROJ_PAD = 3 * ATT_DIM + SSM_DIM + 3 * CONV_DIM + LANES
VMEM_LIMIT = 56 * 1024 * 1024


def _params(*sem):
    return pltpu.CompilerParams(dimension_semantics=sem, vmem_limit_bytes=VMEM_LIMIT)


def _const_spec(shape):
    zeros = (0,) * len(shape)
    return pl.BlockSpec(shape, lambda *_: zeros)


def _layer_spec(shape, layer, **kw):
    index = (layer,) + (0,) * len(shape)
    return pl.BlockSpec((None,) + tuple(shape), lambda *_: index, **kw)


def _nt_dot(a, b, **kw):
    return lax.dot_general(a, b, (((1,), (1,)), ((), ())), preferred_element_type=F32, **kw)


def _bias_selectors():
    h = jnp.arange(N_HEADS)
    base = LANES * (h // 2) + AUG_STRIDE * (h % 2)
    sq = jnp.zeros((ATT_DIM, LANES), F32)
    sk = jnp.zeros((LANES, ATT_DIM), F32)
    for r in range(BIAS_PIECES):
        sq = sq.at[base + r, N_HEADS * r + h].set(1.0)
        sk = sk.at[N_HEADS * r + h, base + BIAS_PIECES + r].set(-1.0)
    return sq.astype(BF16), sk.astype(BF16)


def _split3(t):
    p1 = t.astype(BF16)
    r1 = t - p1.astype(F32)
    p2 = r1.astype(BF16)
    p3 = (r1 - p2.astype(F32)).astype(BF16)
    return p1, p2, p3


def _inproj_body(*refs, cum_block, prompt, n_prev):
    (x_ref, ln1_ref, w_ref, qnw_ref, knw_ref, bf_ref, mbd_ref, tri_ref, sel_ref, c0_ref, sq_ref, sk_ref) = refs[:12]
    outs = refs[12 + n_prev:]
    carry_ref = outs[-1]
    i = pl.program_id(1)
    x = x_ref[...]
    ms = jnp.mean(x * x, axis=-1, keepdims=True)
    h = (x * lax.rsqrt(ms + RMS_EPS) * ln1_ref[...]).astype(BF16)
    proj = jnp.dot(h, w_ref[...], preferred_element_type=F32)
    tb = x.shape[0]

    def head_norm(t, w):
        ss = jnp.dot((t * t).astype(BF16), mbd_ref[...], preferred_element_type=F32)
        return t * lax.rsqrt(ss * (1.0 / HEAD_DIM) + RMS_EPS) * w

    q = head_norm(proj[:, 0:ATT_DIM], qnw_ref[...]) * (LOG2E * HEAD_DIM ** -0.5)
    k = head_norm(proj[:, ATT_DIM:2 * ATT_DIM], knw_ref[...])
    v = proj[:, 2 * ATT_DIM:3 * ATT_DIM]
    o = 3 * ATT_DIM
    u = proj[:, o:o + SSM_DIM]
    hc = proj[:, o + SSM_DIM:o + SSM_DIM + CONV_DIM]
    gb = proj[:, o + SSM_DIM + CONV_DIM:o + SSM_DIM + 2 * CONV_DIM]
    gc = proj[:, o + SSM_DIM + 2 * CONV_DIM:o + SSM_DIM + 3 * CONV_DIM]
    lf = jax.nn.log_sigmoid(proj[:, PROJ_PAD - LANES:PROJ_PAD] + bf_ref[...])

    @pl.when(i == 0)
    def _():
        carry_ref[...] = c0_ref[...]

    carry = carry_ref[...]
    pieces = []
    for s in range(tb // cum_block):
        blk = jnp.concatenate(_split3(lf[s * cum_block:(s + 1) * cum_block]), axis=-1)
        r = jnp.dot(tri_ref[...], blk, preferred_element_type=F32)
        c = r[:, 0:LANES] + r[:, LANES:2 * LANES] + r[:, 2 * LANES:3 * LANES] + carry
        carry = c[cum_block - 1:cum_block]
        pieces.append(c)
    c_all = pieces[0] if len(pieces) == 1 else jnp.concatenate(pieces, axis=0)
    carry_ref[...] = carry
    c2 = c_all * LOG2E
    lf_t = _nt_dot(sel_ref[...], lf, precision=HIGHEST)

    if prompt:
        qx_ref, kx_ref, vb_ref, kt_ref, vt_ref, lft_ref, u_ref, z_ref, gb_ref = outs[:-1]
        head_lane = lax.broadcasted_iota(jnp.int32, (tb, LANES), 1) < N_HEADS
        p1, p2, p3 = (p.astype(F32) for p in _split3(jnp.where(head_lane, c2, 0.0)))
        pcs = (p1 + pltpu.roll(p2, N_HEADS, 1) + pltpu.roll(p3, 2 * N_HEADS, 1)).astype(BF16)
        row = lax.broadcasted_iota(jnp.int32, (ATT_DIM, tb), 0) % LANES
        q_one = (row < 2 * AUG_STRIDE) & (row % AUG_STRIDE >= BIAS_PIECES) & (row % AUG_STRIDE < 2 * BIAS_PIECES)
        qa_t = _nt_dot(sq_ref[...], pcs) + q_one.astype(F32)
        lane = lax.broadcasted_iota(jnp.int32, (tb, ATT_DIM), 1) % LANES
        k_one = (lane < 2 * AUG_STRIDE) & (lane % AUG_STRIDE < BIAS_PIECES)
        ka = jnp.dot(pcs, sk_ref[...], preferred_element_type=F32) + k_one.astype(F32)
        k_t, v_t = k.T, v.T
        qtb, qatb, kb, kab = q.T.astype(BF16), qa_t.astype(BF16), k.astype(BF16), ka.astype(BF16)
        qx, kx = [], []
        for p in range(HEAD_PAIRS):
            sl = slice(p * LANES, (p + 1) * LANES)
            qx += [qtb[sl], qatb[sl]]
            kx += [kb[:, sl], kab[:, sl]]
        qx_ref[...] = jnp.concatenate(qx, axis=0)
        kx_ref[...] = jnp.concatenate(kx, axis=-1)
        vb_ref[...] = v_t.astype(BF16)
        kt_ref[...] = k_t
        vt_ref[...] = v_t
    else:
        q_ref, kb_ref, vb_ref, kf_ref, vf_ref, lft_ref, cq_ref, ct_ref, u_ref, z_ref, gb_ref = outs[:-1]
        q_ref[...] = q.astype(BF16)
        kb_ref[...] = k.astype(BF16)
        vb_ref[...] = v.astype(BF16)
        kf_ref[...] = k.reshape(tb, N_HEADS, HEAD_DIM)
        vf_ref[...] = v.reshape(tb, N_HEADS, HEAD_DIM)
        cq_ref[...] = c2[:, 0:N_HEADS]
        ct_ref[...] = _nt_dot(sel_ref[...], c2, precision=HIGHEST)
    lft_ref[...] = lf_t
    u_ref[...] = u
    z_ref[...] = gc * hc
    gb_ref[...] = gb


def _inproj(x, c0, lw, tb, layer, depth, prev, prompt):
    bsz, seq, _ = x.shape
    cum_block = min(tb, LANES)
    tri = (lax.broadcasted_iota(jnp.int32, (cum_block, cum_block), 1)
           <= lax.broadcasted_iota(jnp.int32, (cum_block, cum_block), 0)).astype(BF16)
    sel =(lax.broadcasted_iota(jnp.int32, (N_HEADS, LANES), 0)
           == lax.broadcasted_iota(jnp.int32, (N_HEADS, LANES), 1)).astype(F32)
    sq, sk = _bias_selectors()
    tok = lambda w: pl.BlockSpec((None, tb, w), lambda b, i: (b, i, 0))
    head_major = pl.BlockSpec((None, N_HEADS, tb), lambda b, i: (b, 0, i))
    tok_shape = lambda w, dt: jax.ShapeDtypeStruct((bsz, seq, w), dt)
    heads_shape = jax.ShapeDtypeStruct((bsz, N_HEADS, seq), F32)
    if prompt:
        state_shape = jax.ShapeDtypeStruct((depth, bsz, ATT_DIM, seq), F32)
        state_spec = pl.BlockSpec((None, None, ATT_DIM, tb), lambda b, i: (layer, b, 0, i))
        rows_major = lambda r: pl.BlockSpec((None, r, tb), lambda b, i: (b, 0, i))
        out_shapes = (jax.ShapeDtypeStruct((bsz, HEAD_PAIRS * PAIR_EXT, seq), BF16),
                      tok_shape(HEAD_PAIRS * PAIR_EXT, BF16),
                      jax.ShapeDtypeStruct((bsz, ATT_DIM, seq), BF16),
                      state_shape, state_shape, heads_shape,
                      tok_shape(SSM_DIM, F32), tok_shape(CONV_DIM, F32), tok_shape(CONV_DIM, F32))
        out_specs = (rows_major(HEAD_PAIRS * PAIR_EXT), tok(HEAD_PAIRS * PAIR_EXT), rows_major(ATT_DIM),
                     state_spec, state_spec, head_major,
                     tok(SSM_DIM), tok(CONV_DIM), tok(CONV_DIM))
        state_out = (3, 4)
    else:
        state_shape = jax.ShapeDtypeStruct((depth, bsz, seq, N_HEADS, HEAD_DIM), F32)
        state_spec = pl.BlockSpec((None, None, tb, N_HEADS, HEAD_DIM), lambda b, i: (layer, b, i, 0, 0))
        out_shapes = (tok_shape(ATT_DIM, BF16), tok_shape(ATT_DIM, BF16), tok_shape(ATT_DIM, BF16),
                      state_shape, state_shape, heads_shape,
                      tok_shape(N_HEADS, F32), heads_shape,
                      tok_shape(SSM_DIM, F32), tok_shape(CONV_DIM, F32), tok_shape(CONV_DIM, F32))
        out_specs = (tok(ATT_DIM), tok(ATT_DIM), tok(ATT_DIM), state_spec, state_spec, head_major,
                     tok(N_HEADS), head_major, tok(SSM_DIM), tok(CONV_DIM), tok(CONV_DIM))
        state_out = (3, 4)
    in_specs = [tok(D_MODEL), _layer_spec((1, D_MODEL), layer), _layer_spec((D_MODEL, PROJ_PAD), layer),
                _layer_spec((1, ATT_DIM), layer), _layer_spec((1, ATT_DIM), layer), _layer_spec((1, LANES), layer),
                _const_spec((ATT_DIM, ATT_DIM)), _const_spec((cum_block, cum_block)),
                _const_spec((N_HEADS, LANES)),
                pl.BlockSpec((None, 1, LANES), lambda b, i: (b, 0, 0)),
                _const_spec(sq.shape), _const_spec(sk.shape)]
    args = [x, lw["ln1"], lw["w_in"], lw["qn"], lw["kn"], lw["bf"], lw["mbd"], tri, sel, c0, sq, sk]
    aliases = {}
    n_prev = 0
    if prev is not None:
        n_prev = len(prev)
        for n, buf in enumerate(prev):
            aliases[len(args)] = state_out[n]
            args.append(buf)
            in_specs.append(pl.BlockSpec(memory_space=pl.ANY))
    return pl.pallas_call(
        functools.partial(_inproj_body, cum_block=cum_block, prompt=prompt, n_prev=n_prev),
        grid=(bsz, seq // tb),
        in_specs=in_specs, out_specs=out_specs, out_shape=out_shapes,
        scratch_shapes=[pltpu.VMEM((1, LANES), F32)],
        input_output_aliases=aliases,
        compiler_params=_params("arbitrary", "arbitrary"),
        name="inproj_prompt" if prompt else "inproj_sample",
    )(*args)


def _attn_body(qx_ref, kx_ref, vt_ref, o_ref, m_s, mb_s, acc_s, s_s, *, tq, pairs):
    nq = o_ref.shape[0] // tq
    row_x = lax.broadcasted_iota(jnp.int32, (PAIR_EXT, tq), 0)
    row_v = lax.broadcasted_iota(jnp.int32, (LANES, tq), 0)
    head_rows = [(row_v >= HEAD_DIM * a) & (row_v < HEAD_DIM * (a + 1)) for a in range(2)]
    mine = [(((row_x >= HEAD_DIM * a) & (row_x < HEAD_DIM * (a + 1)))
             | ((row_x >= LANES + AUG_STRIDE * a) & (row_x < LANES + AUG_STRIDE * (a + 1)))) for a in range(2)]
    half = tq // 2
    causal_top = (lax.broadcasted_iota(jnp.int32, (half, tq), 0) <= lax.broadcasted_iota(jnp.int32, (half, tq), 1))
    causal_bot = (lax.broadcasted_iota(jnp.int32, (half, half), 0) <= lax.broadcasted_iota(jnp.int32, (half, half), 1))

    def col_max(s):
        return jnp.max(s, axis=0, keepdims=True)

    def scores(qi, j, p, diagonal=False):
        cols = pl.ds(pl.multiple_of(qi * tq, tq), tq)
        qt = qx_ref[p * PAIR_EXT:(p + 1) * PAIR_EXT, cols]
        rows = pl.ds(pl.multiple_of(j * tq, tq), tq)
        kx = kx_ref[rows, p * PAIR_EXT:(p + 1) * PAIR_EXT]
        for a in range(2):
            h = 2 * p + a
            qa = jnp.where(mine[a], qt, jnp.zeros_like(qt))
            if diagonal:
                s_s[h, 0:half, :] = jnp.dot(kx[0:half], qa, preferred_element_type=F32)
                s_s[h, half:tq, half:tq] = jnp.dot(kx[half:tq], qa[:, half:tq], preferred_element_type=F32)
            else:
                s = jnp.dot(kx, qa, preferred_element_type=F32)
                s_s[h] = s
                mb_s[h] = col_max(s)

    def absorb(j, p, diagonal):
        cols = pl.ds(pl.multiple_of(j * tq, tq), tq)
        vt = vt_ref[p * LANES:(p + 1) * LANES, cols]
        for a in range(2):
            h = 2 * p + a
            va = jnp.where(head_rows[a], vt, jnp.ones_like(vt))
            m_old = m_s[h]
            if diagonal:
                top = jnp.where(causal_top, s_s[h, 0:half, :], NEG_INF)
                bot = jnp.where(causal_bot, s_s[h, half:tq, half:tq], NEG_INF)
                m_top = col_max(top)
                m_blk = jnp.concatenate([m_top[:, 0:half], jnp.maximum(m_top[:, half:tq], col_max(bot))], axis=1)
                m_new = jnp.maximum(m_old, m_blk)
                out = jnp.dot(va[:, 0:half], jnp.exp2(top - m_new[0:1]).astype(BF16), preferred_element_type=F32)
                out_late = jnp.dot(va[:, half:tq], jnp.exp2(bot - m_new[0:1, half:tq]).astype(BF16),
                                   preferred_element_type=F32)
                out = jnp.concatenate([out[:, 0:half], out[:, half:tq] + out_late], axis=1)
            else:
                m_new = jnp.maximum(m_old, mb_s[h])
                out = jnp.dot(va, jnp.exp2(s_s[h] - m_new[0:1]).astype(BF16), preferred_element_type=F32)
            acc_s[h] = jnp.exp2(m_old - m_new)[0:1] * acc_s[h] + out
            m_s[h] = m_new

    def reset():
        for h in range(2 * pairs):
            m_s[h] = jnp.full((1, tq), NEG_INF, F32)
            acc_s[h] = jnp.zeros((LANES, tq), F32)

    def key_block(qi, j, diagonal, nxt):
        for p in range(pairs):
            if p + 1 < pairs:
                scores(qi, j, p + 1, diagonal)
            else:
                scores(nxt[0], nxt[1], 0)
            absorb(j, p, diagonal)

    def query_block(qi, carry):
        def body(j, c):
            key_block(qi, j, False, (qi, j + 1))
            return c

        lax.fori_loop(0, qi, body, 0)
        key_block(qi, qi, True, (jnp.minimum(qi + 1, nq - 1), 0))
        rows = pl.ds(pl.multiple_of(qi * tq, tq), tq)
        for p in range(pairs):
            a0, a1 = acc_s[2 * p], acc_s[2 * p + 1]
            out_t = jnp.concatenate([a0[0:HEAD_DIM] / a0[HEAD_DIM:LANES],
                                     a1[HEAD_DIM:LANES] / a1[0:HEAD_DIM]], axis=0)
            o_ref[rows, p * LANES:(p + 1) * LANES] = out_t.T
        reset()
        return carry

    reset()
    scores(0, 0, 0)
    lax.fori_loop(0, nq, query_block, 0)


def _attn_prompt(qx_t, kx, vt, tq, pairs):
    bsz, seq, _ = kx.shape
    return pl.pallas_call(
        functools.partial(_attn_body, tq=tq, pairs=pairs),
        grid=(bsz, HEAD_PAIRS // pairs),
        in_specs=[pl.BlockSpec((None, pairs * PAIR_EXT, seq), lambda b, h: (b, h, 0)),
                  pl.BlockSpec((None, seq, pairs * PAIR_EXT), lambda b, h: (b, 0, h)),
                  pl.BlockSpec((None, pairs * LANES, seq), lambda b, h: (b, h, 0))],
        out_specs=pl.BlockSpec((None, seq, pairs * LANES), lambda b, h: (b, 0, h)),
        out_shape=jax.ShapeDtypeStruct((bsz, seq, ATT_DIM), F32),
        scratch_shapes=[pltpu.VMEM((2 * pairs, 1, tq), F32), pltpu.VMEM((2 * pairs, 1, tq), F32),
                        pltpu.VMEM((2 * pairs, LANES, tq), F32),
                        pltpu.VMEM((2 * pairs, tq, tq), F32)],
        compiler_params=_params("arbitrary", "arbitrary"),
        name="attn_prompt",
    )(qx_t, kx, vt)


def _attn_sample_body(q_ref, ckt_ref, cvt_ref, kn_ref, vn_ref, cq_ref, ctc_ref, ctn_ref, o_ref, *, pairs):
    tq = q_ref.shape[0]
    past = ckt_ref.shape[-1]
    cq8 = cq_ref[...]
    lane = lax.broadcasted_iota(jnp.int32, (tq, LANES), 1)
    lane8 = lax.broadcasted_iota(jnp.int32, (tq, N_HEADS), 1)
    row = lax.broadcasted_iota(jnp.int32, (tq, tq), 0)
    col = lax.broadcasted_iota(jnp.int32, (tq, tq), 1)
    for pp in range(pairs):
        hp = pl.program_id(1) * pairs + pp
        lanes = slice(pp * LANES, (pp + 1) * LANES)
        q = q_ref[:, lanes]
        q2 = jnp.concatenate(
            [jnp.where((lane >= HEAD_DIM * a) & (lane < HEAD_DIM * (a + 1)), q, jnp.zeros_like(q)) for a in range(2)],
            axis=0)
        kt = ckt_ref[2 * pp:2 * pp + 2].reshape(LANES, past).astype(BF16)
        vt = cvt_ref[2 * pp:2 * pp + 2].reshape(LANES, past).astype(BF16)
        s_cache = jnp.dot(q2, kt, preferred_element_type=F32)
        s_new = _nt_dot(q2, kn_ref[:, lanes])
        sc, sn = [], []
        for a in range(2):
            head = 2 * hp + a
            cqa = jnp.sum(jnp.where(lane8 == head, cq8, 0.0), axis=-1, keepdims=True)
            ck_cache = ctc_ref[pl.ds(head, 1), :] * LOG2E
            ck_new = ctn_ref[pl.ds(head, 1), :]
            sc.append(s_cache[a * tq:(a + 1) * tq] + cqa - ck_cache)
            sn.append(jnp.where(col <= row, s_new[a * tq:(a + 1) * tq] + cqa - ck_new, NEG_INF))
        s_cache = jnp.concatenate(sc, axis=0)
        s_new = jnp.concatenate(sn, axis=0)
        m = jnp.maximum(jnp.max(s_cache, axis=-1, keepdims=True), jnp.max(s_new, axis=-1, keepdims=True))
        p_cache = jnp.exp2(s_cache - m)
        p_new = jnp.exp2(s_new - m)
        l = jnp.sum(p_cache, axis=-1, keepdims=True) + jnp.sum(p_new, axis=-1, keepdims=True)
        o = (_nt_dot(p_cache.astype(BF16), vt)
             + jnp.dot(p_new.astype(BF16), vn_ref[:, lanes], preferred_element_type=F32)) / l
        o_ref[:, lanes] = jnp.where(lane < HEAD_DIM, o[0:tq], o[tq:2 * tq])


def _attn_sample(q, kb, vb, cq, ct_new, cache_kt, cache_vt, ct_cache, layer, pairs=4):
    bsz, seq, _ = q.shape
    past = cache_kt.shape[-1]
    cache_blk = pl.BlockSpec((None, None, 2 * pairs, HEAD_DIM, past), lambda b, h: (layer, b, h, 0, 0))
    new_blk = pl.BlockSpec((None, seq, pairs * LANES), lambda b, h: (b, 0, h))
    return pl.pallas_call(
        functools.partial(_attn_sample_body, pairs=pairs),
        grid=(bsz, HEAD_PAIRS // pairs),
        in_specs=[new_blk, cache_blk, cache_blk, new_blk, new_blk,
                  pl.BlockSpec((None, seq, N_HEADS), lambda b, h: (b, 0, 0)),
                  pl.BlockSpec((None, N_HEADS, past), lambda b, h: (b, 0, 0)),
                  pl.BlockSpec((None, N_HEADS, seq), lambda b, h: (b, 0, 0))],
        out_specs=new_blk,
        out_shape=jax.ShapeDtypeStruct((bsz, seq, ATT_DIM), F32),
        compiler_params=_params("arbitrary", "arbitrary"),
        name="attn_sample",
    )(q, cache_kt, cache_vt, kb, vb, cq, ct_cache, ct_new)


def _cache_cumsum_body(x_ref, o_ref):
    rows, past = x_ref.shape
    tri = (lax.broadcasted_iota(jnp.int32, (LANES, LANES), 0)
           <= lax.broadcasted_iota(jnp.int32, (LANES, LANES), 1)).astype(F32)
    carry = jnp.zeros((rows, 1), F32)
    for b in range(past // LANES):
        blk = x_ref[:, b * LANES:(b + 1) * LANES]
        c = jnp.dot(blk, tri, precision=HIGHEST, preferred_element_type=F32) + carry
        o_ref[:, b * LANES:(b + 1) * LANES] = c
        carry = c[:, LANES - 1:LANES]


def _cache_cumsum(x):
    return pl.pallas_call(
        _cache_cumsum_body,
        out_shape=jax.ShapeDtypeStruct(x.shape, F32),
        compiler_params=pltpu.CompilerParams(vmem_limit_bytes=VMEM_LIMIT),
        name="cache_cumsum",
    )(x)


def _s5_prep_body(lr_ref, li_ref, ldt_ref, br_ref, bi_ref, ar_ref, ai_ref, bbr_ref, bbi_ref):
    lr = jnp.minimum(lr_ref[...], -1e-4)
    li = li_ref[...]
    dt = jnp.exp(ldt_ref[...])
    ldr, ldi = lr * dt, li * dt
    mag = jnp.exp(ldr)
    ar, ai = mag * jnp.cos(ldi), mag * jnp.sin(ldi)
    den = lr * lr + li * li
    nr = ar - 1.0
    qr = (nr * lr + ai * li) / den
    qi = (ai * lr - nr * li) / den
    ar_ref[...] = ar
    ai_ref[...] = ai
    br, bi = br_ref[...], bi_ref[...]
    bbr_ref[...] = qr[:, None, :] * br - qi[:, None, :] * bi
    bbi_ref[...] = qr[:, None, :] * bi + qi[:, None, :] * br


def _s5_prep(lam_re, lam_im, log_dt, b_re, b_im):
    g, p = lam_re.shape
    gp = jax.ShapeDtypeStruct((g, p), F32)
    gcp = jax.ShapeDtypeStruct((g, SSM_GROUP, p), F32)
    return pl.pallas_call(
        _s5_prep_body, out_shape=(gp, gp, gcp, gcp), name="s5_prep",
    )(lam_re, lam_im, log_dt.reshape(g, 1), jnp.swapaxes(b_re, 1, 2), jnp.swapaxes(b_im, 1, 2))


def _block_diag(t):
    d, g, c, p = t.shape
    eye = jnp.eye(g, dtype=t.dtype)
    return (t[:, :, :, None, :] * eye[None, :, None, :, None]).reshape(d, g * c, g * p)


def _ssm_body(u_ref, z_ref, gb_ref, h0r_ref, h0i_ref, cprev_ref,
              bre_ref, bim_ref, cre_ref, cim_ref, ar_ref, ai_ref, d_ref, wglu_ref, bglu_ref, cw_ref,
              ssm_ref, conv_ref, hr_out, hi_out, cst_out,
              sre, sim, hr_s, hi_s, zprev_s, *, lb):
    j = pl.program_id(1)
    nb = SUBLANES
    tiles = STATE_LANES // LANES

    @pl.when(j == 0)
    def _():
        hr_s[...] = h0r_ref[...]
        hi_s[...] = h0i_ref[...]
        zprev_s[...] = cprev_ref[...]

    u2 = jnp.swapaxes(u_ref[...], 0, 1).reshape(lb * nb, SSM_DIM)
    ub = u2.astype(BF16)
    bu_re = jnp.dot(ub, bre_ref[...], preferred_element_type=F32)
    bu_im = jnp.dot(ub, bim_ref[...], preferred_element_type=F32)
    for c in range(tiles):
        sre[c] = bu_re[:, c * LANES:(c + 1) * LANES]
        sim[c] = bu_im[:, c * LANES:(c + 1) * LANES]
    ar = [jnp.broadcast_to(ar_ref[:, c * LANES:(c + 1) * LANES], (nb, LANES)) for c in range(tiles)]
    ai = [jnp.broadcast_to(ai_ref[:, c * LANES:(c + 1) * LANES], (nb, LANES)) for c in range(tiles)]

    def step(t, carry):
        hr, hi = carry
        rows = pl.ds(pl.multiple_of(t * nb, nb), nb)
        nr, ni = [], []
        for c in range(tiles):
            r = ar[c] * hr[c] - ai[c] * hi[c] + sre[c, rows, :]
            i = ar[c] * hi[c] + ai[c] * hr[c] + sim[c, rows, :]
            sre[c, rows, :] = r
            sim[c, rows, :] = i
            nr.append(r)
            ni.append(i)
        return tuple(nr), tuple(ni)

    init = (tuple(hr_s[:, c * LANES:(c + 1) * LANES] for c in range(tiles)),
            tuple(hi_s[:, c * LANES:(c + 1) * LANES] for c in range(tiles)))
    hr, hi = lax.fori_loop(0, lb, step, init, unroll=2)
    for c in range(tiles):
        hr_s[:, c * LANES:(c + 1) * LANES] = hr[c]
        hi_s[:, c * LANES:(c + 1) * LANES] = hi[c]
        hr_out[:, c * LANES:(c + 1) * LANES] = hr[c]
        hi_out[:, c * LANES:(c + 1) * LANES] = hi[c]

    h_re = jnp.concatenate([sre[c] for c in range(tiles)], axis=-1).astype(BF16)
    h_im = jnp.concatenate([sim[c] for c in range(tiles)], axis=-1).astype(BF16)
    y = (jnp.dot(h_re, cre_ref[...], preferred_element_type=F32)
         - jnp.dot(h_im, cim_ref[...], preferred_element_type=F32))
    y = y + d_ref[...] * u2
    g = jax.nn.gelu(y)
    gate = jax.nn.sigmoid(jnp.dot(g.astype(BF16), wglu_ref[...], preferred_element_type=F32) + bglu_ref[...])
    ssm_ref[...] = jnp.swapaxes((g * gate).reshape(lb, nb, SSM_DIM), 0, 1)

    tt = lax.broadcasted_iota(jnp.int32, (lb, CONV_DIM), 0)
    w0, w1, w2 = cw_ref[0:1, :], cw_ref[1:2, :], cw_ref[2:3, :]
    for b in range(nb):
        zb = z_ref[b]
        prev = zprev_s[b]
        z1 = jnp.where(tt == 0, prev[1:2], pltpu.roll(zb, 1, 0))
        z2 = jnp.where(tt == 0, prev[0:1], jnp.where(tt == 1, prev[1:2], pltpu.roll(zb, 2, 0)))
        conv_ref[b] = gb_ref[b] * (w0 * z2 + w1 * z1 + w2 * zb)
        last = zb[lb - (CONV_WIDTH - 1):lb]
        zprev_s[b] = last
        cst_out[b] = last


def _ssm_conv(u, z, gb, h0r, h0i, cprev, lw, layer, lb):
    bsz, seq, _ = u.shape
    nb = SUBLANES
    tok = pl.BlockSpec((nb, lb, SSM_DIM), lambda g, j: (g, j, 0))
    st = pl.BlockSpec((nb, STATE_LANES), lambda g, j: (g, 0))
    cs = pl.BlockSpec((nb, CONV_WIDTH - 1, CONV_DIM), lambda g, j: (g, 0, 0))
    par = lambda *shape: _layer_spec(shape, layer)
    return pl.pallas_call(
        functools.partial(_ssm_body, lb=lb),
        grid=(bsz // nb, seq // lb),
        in_specs=[tok, tok, tok, st, st, cs,
                  par(SSM_DIM, STATE_LANES), par(SSM_DIM, STATE_LANES),
                  par(STATE_LANES, SSM_DIM), par(STATE_LANES, SSM_DIM),
                  par(1, STATE_LANES), par(1, STATE_LANES),
                  par(1, SSM_DIM), par(SSM_DIM, SSM_DIM), par(1, SSM_DIM),
                  par(CONV_WIDTH, CONV_DIM)],
        out_specs=(tok, tok, st, st, cs),
        out_shape=(jax.ShapeDtypeStruct((bsz, seq, SSM_DIM), F32),
                   jax.ShapeDtypeStruct((bsz, seq, CONV_DIM), F32),
                   jax.ShapeDtypeStruct((bsz, STATE_LANES), F32),
                   jax.ShapeDtypeStruct((bsz, STATE_LANES), F32),
                   jax.ShapeDtypeStruct((bsz, CONV_WIDTH - 1, CONV_DIM), F32)),
        scratch_shapes=[pltpu.VMEM((STATE_LANES // LANES, nb * lb, LANES), F32),
                        pltpu.VMEM((STATE_LANES // LANES, nb * lb, LANES), F32),
                        pltpu.VMEM((nb, STATE_LANES), F32), pltpu.VMEM((nb, STATE_LANES), F32),
                        pltpu.VMEM((nb, CONV_WIDTH - 1, CONV_DIM), F32)],
        compiler_params=_params("arbitrary", "arbitrary"),
        name="ssm_conv",
    )(u, z, gb, h0r, h0i, cprev, lw["b_re_blk"], lw["b_im_blk"], lw["c_re_blk"], lw["c_im_blk"],
      lw["a_re"], lw["a_im"], lw["d"], lw["w_glu"], lw["b_glu"], lw["conv_w"])


def _rms_rows(t, w):
    return t * lax.rsqrt(jnp.mean(t * t, axis=-1, keepdims=True) + RMS_EPS) * w


def _mix_ffn_body(x_ref, att_ref, ssm_ref, conv_ref, bn_ref, wout_ref, ln2_ref, wup_ref, wdown_ref, o_ref,
                  *, ff_chunk):
    bn = bn_ref[...]
    mix = jnp.concatenate(
        [_rms_rows(att_ref[...], bn[:, 0:ATT_DIM]),
         _rms_rows(ssm_ref[...], bn[:, ATT_DIM:ATT_DIM + SSM_DIM]),
         _rms_rows(conv_ref[...], bn[:, ATT_DIM + SSM_DIM:])], axis=-1).astype(BF16)
    x1 = x_ref[...] + jnp.dot(mix, wout_ref[...], preferred_element_type=F32)
    h2 = _rms_rows(x1, ln2_ref[...]).astype(BF16)
    acc = x1
    for c in range(D_FF // ff_chunk):
        f = jnp.maximum(jnp.dot(h2, wup_ref[:, c * ff_chunk:(c + 1) * ff_chunk], preferred_element_type=F32), 0.0)
        acc = acc + jnp.dot((f * f).astype(BF16), wdown_ref[c * ff_chunk:(c + 1) * ff_chunk, :],
                            preferred_element_type=F32)
    o_ref[...] = acc


def _mix_ffn(x, att, ssm, conv, lw, layer, tb):
    bsz, seq, _ = x.shape
    rows = bsz * seq
    tok = lambda w: pl.BlockSpec((tb, w), lambda i: (i, 0))
    once = lambda *shape: _layer_spec(shape, layer, pipeline_mode=pl.Buffered(1))
    out = pl.pallas_call(
        functools.partial(_mix_ffn_body, ff_chunk=1024),
        grid=(rows // tb,),
        in_specs=[tok(D_MODEL), tok(ATT_DIM), tok(SSM_DIM), tok(CONV_DIM),
                  once(1, D_MODEL), once(D_MODEL, D_MODEL), once(1, D_MODEL),
                  once(D_MODEL, D_FF), once(D_FF, D_MODEL)],
        out_specs=tok(D_MODEL),
        out_shape=jax.ShapeDtypeStruct((rows, D_MODEL), F32),
        compiler_params=_params("arbitrary"),
        name="mix_ffn",
    )(x.reshape(rows, D_MODEL), att.reshape(rows, ATT_DIM), ssm.reshape(rows, SSM_DIM),
      conv.reshape(rows, CONV_DIM), lw["bn"], lw["w_out"], lw["ln2"], lw["w_up"], lw["w_down"])
    return out.reshape(bsz, seq, D_MODEL)


def _prep_weights(ln1_w, w_in, b_forget, q_norm_w, k_norm_w, conv_w, ssm_lam_re, ssm_lam_im, ssm_log_dt,
                  ssm_b_re, ssm_b_im, ssm_c_re, ssm_c_im, ssm_d, w_glu, b_glu, branch_norm_w, w_out,
                  ln2_w, w_up, w_down):
    depth = w_in.shape[0]
    fg0 = 3 * ATT_DIM
    w_re = jnp.concatenate([w_in[:, :, :fg0], w_in[:, :, fg0 + N_HEADS:], w_in[:, :, fg0:fg0 + N_HEADS],
                            jnp.zeros((depth, D_MODEL, LANES - N_HEADS), F32)], axis=2).astype(BF16)
    head_id = jnp.arange(ATT_DIM) // HEAD_DIM
    groups = depth * SSM_GROUPS
    a_re, a_im, bb_re, bb_im = _s5_prep(
        ssm_lam_re.reshape(groups, SSM_STATE), ssm_lam_im.reshape(groups, SSM_STATE), ssm_log_dt.reshape(groups),
        ssm_b_re.reshape(groups, SSM_STATE, SSM_GROUP), ssm_b_im.reshape(groups, SSM_STATE, SSM_GROUP))
    per_layer = lambda t: t.reshape(depth, SSM_GROUPS, SSM_GROUP, SSM_STATE)
    row = lambda t, w: t.reshape(depth, 1, w)
    return dict(
        ln1=row(ln1_w, D_MODEL), w_in=w_re,
        qn=row(jnp.tile(q_norm_w, (1, N_HEADS)), ATT_DIM), kn=row(jnp.tile(k_norm_w, (1, N_HEADS)), ATT_DIM),
        bf=row(jnp.pad(b_forget, ((0, 0), (0, LANES - N_HEADS))), LANES),
        mbd=(head_id[:, None] == head_id[None, :]).astype(BF16),
        a_re=row(a_re, STATE_LANES), a_im=row(a_im, STATE_LANES),
        b_re_blk=_block_diag(per_layer(bb_re)).astype(BF16), b_im_blk=_block_diag(per_layer(bb_im)).astype(BF16),
        c_re_blk=jnp.swapaxes(_block_diag(ssm_c_re), 1, 2).astype(BF16),
        c_im_blk=jnp.swapaxes(_block_diag(ssm_c_im), 1, 2).astype(BF16),
        d=row(ssm_d, SSM_DIM), w_glu=w_glu.astype(BF16), b_glu=row(b_glu, SSM_DIM),
        conv_w=conv_w, bn=row(branch_norm_w, D_MODEL), w_out=w_out.astype(BF16),
        ln2=row(ln2_w, D_MODEL), w_up=w_up.astype(BF16), w_down=w_down.astype(BF16))


def _run_layer(x, lw, layer, depth, prev, past, tiles):
    bsz, seq, _ = x.shape
    if past is None:
        c0 = jnp.zeros((bsz, 1, LANES), F32)
        h0r = jnp.zeros((bsz, STATE_LANES), F32)
        h0i = h0r
        cprev = jnp.zeros((bsz, CONV_WIDTH - 1, CONV_DIM), F32)
        qx, kx, vb, k_all, v_all, lft, u, z, gb = _inproj(x, c0, lw, tiles["tok"], layer, depth, prev, True)
        att = _attn_prompt(qx, kx, vb, tiles["attn"], tiles["attn_pairs"])
    else:
        cache_kt, cache_vt, clf_t, s_re, s_im, s_conv = past
        plen = cache_kt.shape[-1]
        ct_cache = _cache_cumsum(clf_t.reshape(bsz * N_HEADS, plen)).reshape(bsz, N_HEADS, plen)
        c0 = jnp.pad(ct_cache[:, :, plen - 1], ((0, 0), (0, LANES - N_HEADS))).reshape(bsz, 1, LANES)
        h0r = s_re.reshape(bsz, STATE_LANES)
        h0i = s_im.reshape(bsz, STATE_LANES)
        cprev = s_conv
        q, kb, vb, k_all, v_all, lft, cq, ct, u, z, gb = _inproj(x, c0, lw, tiles["tok"], layer, depth, prev, False)
        att = _attn_sample(q, kb, vb, cq, ct, cache_kt, cache_vt, ct_cache, layer)
    ssm, conv, hr, hi, cst = _ssm_conv(u, z, gb, h0r, h0i, cprev, lw, layer, tiles["scan"])
    y = _mix_ffn(x, att, ssm, conv, lw, layer, tiles["ffn"])
    small = (lft, hr.reshape(bsz, SSM_GROUPS, SSM_STATE), hi.reshape(bsz, SSM_GROUPS, SSM_STATE), cst)
    return y, (k_all, v_all), small


def _tiles(bsz, seq):
    tok = min(seq, 1024)
    return dict(tok=tok, attn=min(seq, 512), attn_pairs=2, scan=min(seq, 128), ffn=min(bsz * seq, 1024))


def _forward(x_prompt, x_sample, cache_k, cache_v, cache_logf, state_ssm_re, state_ssm_im, state_conv, *weights):
    depth = cache_k.shape[0]
    xp, xs = x_prompt, x_sample
    bp, lp = xp.shape[0], xp.shape[1]
    tiles_p = _tiles(bp, lp)
    tiles_s = _tiles(xs.shape[0], xs.shape[1])
    cache_kt = jnp.transpose(cache_k, (0, 1, 3, 4, 2))
    cache_vt = jnp.transpose(cache_v, (0, 1, 3, 4, 2))
    cache_lft = jnp.swapaxes(cache_logf, 2, 3)
    kv_p, kv_s, small_p, small_s = None, None, [], []
    lw = _prep_weights(*weights)
    for l in range(depth):
        xp, kv_p, sp = _run_layer(xp, lw, l, depth, kv_p, None, tiles_p)
        xs, kv_s, ss = _run_layer(xs, lw, l, depth, kv_s,
                                  (cache_kt, cache_vt, cache_lft[l], state_ssm_re[l], state_ssm_im[l], state_conv[l]),
                                  tiles_s)
        small_p.append(sp)
        small_s.append(ss)
    stk = lambda lst, i: jnp.stack([s[i] for s in lst])
    from_t = lambda t: jnp.transpose(t.reshape(depth, bp, N_HEADS, HEAD_DIM, lp), (0, 1, 4, 2, 3))
    return (xp, xs,
            from_t(kv_p[0]), from_t(kv_p[1]), jnp.swapaxes(stk(small_p, 0), 2, 3),
            stk(small_p, 1), stk(small_p, 2), stk(small_p, 3),
            kv_s[0], kv_s[1], jnp.swapaxes(stk(small_s, 0), 2, 3),
            stk(small_s, 1), stk(small_s, 2), stk(small_s, 3))


def kernel(x_prompt, x_sample, cache_k, cache_v, cache_logf, state_ssm_re, state_ssm_im, state_conv, ln1_w, w_in, b_forget, q_norm_w, k_norm_w, conv_w, ssm_lam_re, ssm_lam_im, ssm_log_dt, ssm_b_re, ssm_b_im, ssm_c_re, ssm_c_im, ssm_d, w_glu, b_glu, branch_norm_w, w_out, ln2_w, w_up, w_down):
    return _forward(x_prompt, x_sample, cache_k, cache_v, cache_logf, state_ssm_re, state_ssm_im, state_conv,
                    ln1_w, w_in, b_forget, q_norm_w, k_norm_w, conv_w, ssm_lam_re, ssm_lam_im, ssm_log_dt,
                    ssm_b_re, ssm_b_im, ssm_c_re, ssm_c_im, ssm_d, w_glu, b_glu, branch_norm_w, w_out,
                    ln2_w, w_up, w_down)
```

```python
import functools
import math

import jax
import jax.numpy as jnp
from jax import lax
from jax.experimental import pallas as pl
from jax.experimental.pallas import tpu as pltpu

F32 = jnp.float32
BF16 = jnp.bfloat16
HIGHEST = lax.Precision.HIGHEST

D_MODEL = 1024
N_HEADS = 8
HEAD_DIM = 64
ATT_DIM = N_HEADS * HEAD_DIM
SSM_DIM = 256
SSM_GROUP = 16
SSM_GROUPS = SSM_DIM // SSM_GROUP
SSM_STATE = 64
STATE_LANES = SSM_GROUPS * SSM_STATE
CONV_DIM = 256
CONV_WIDTH = 3
D_FF = 4 * D_MODEL
RMS_EPS = 1e-6
NEG_INF = -1e30
LOG2E = math.log2(math.e)

LANES = 128
SUBLANES = 8
HEAD_PAIRS = ATT_DIM // LANES
PAIR_EXT = 2 * LANES
AUG_STRIDE = 8
BIAS_PIECES = 3
PROJ_PAD = 3 * ATT_DIM + SSM_DIM + 3 * CONV_DIM + LANES
VMEM_LIMIT = 56 * 1024 * 1024


def _params(*sem):
    return pltpu.CompilerParams(dimension_semantics=sem, vmem_limit_bytes=VMEM_LIMIT)


def _const_spec(shape):
    zeros = (0,) * len(shape)
    return pl.BlockSpec(shape, lambda *_: zeros)


def _layer_spec(shape, layer, **kw):
    index = (layer,) + (0,) * len(shape)
    return pl.BlockSpec((None,) + tuple(shape), lambda *_: index, **kw)


def _nt_dot(a, b, **kw):
    return lax.dot_general(a, b, (((1,), (1,)), ((), ())), preferred_element_type=F32, **kw)


def _bias_selectors():
    h = jnp.arange(N_HEADS)
    base = LANES * (h // 2) + AUG_STRIDE * (h % 2)
    sq = jnp.zeros((ATT_DIM, LANES), F32)
    sk = jnp.zeros((LANES, ATT_DIM), F32)
    for r in range(BIAS_PIECES):
        sq = sq.at[base + r, N_HEADS * r + h].set(1.0)
        sk = sk.at[N_HEADS * r + h, base + BIAS_PIECES + r].set(-1.0)
    return sq.astype(BF16), sk.astype(BF16)


def _split3(t):
    p1 = t.astype(BF16)
    r1 = t - p1.astype(F32)
    p2 = r1.astype(BF16)
    p3 = (r1 - p2.astype(F32)).astype(BF16)
    return p1, p2, p3


def _inproj_body(*refs, cum_block, prompt, n_prev):
    (x_ref, ln1_ref, w_ref, qnw_ref, knw_ref, bf_ref, mbd_ref, tri_ref, sel_ref, c0_ref, sq_ref, sk_ref) = refs[:12]
    outs = refs[12 + n_prev:]
    carry_ref = outs[-1]
    i = pl.program_id(1)
    x = x_ref[...]
    ms = jnp.mean(x * x, axis=-1, keepdims=True)
    h = (x * lax.rsqrt(ms + RMS_EPS) * ln1_ref[...]).astype(BF16)
    proj = jnp.dot(h, w_ref[...], preferred_element_type=F32)
    tb = x.shape[0]

    def head_norm(t, w):
        ss = jnp.dot((t * t).astype(BF16), mbd_ref[...], preferred_element_type=F32)
        return t * lax.rsqrt(ss * (1.0 / HEAD_DIM) + RMS_EPS) * w

    v = proj[:, 2 * ATT_DIM:3 * ATT_DIM]
    o = 3 * ATT_DIM
    u_ref, z_ref, gb_ref = outs[-4:-1]
    u_ref[...] = proj[:, o:o + SSM_DIM]
    gb_ref[...] = proj[:, o + SSM_DIM + CONV_DIM:o + SSM_DIM + 2 * CONV_DIM]
    z_ref[...] = (proj[:, o + SSM_DIM + 2 * CONV_DIM:o + SSM_DIM + 3 * CONV_DIM]
                  * proj[:, o + SSM_DIM:o + SSM_DIM + CONV_DIM])
    if prompt:
        v_t = v.T
        outs[2][...] = v_t.astype(BF16)
        outs[4][...] = v_t
    k = head_norm(proj[:, ATT_DIM:2 * ATT_DIM], knw_ref[...])
    if prompt:
        outs[3][...] = k.T
    q = head_norm(proj[:, 0:ATT_DIM], qnw_ref[...]) * (LOG2E * HEAD_DIM ** -0.5)
    lf = jax.nn.log_sigmoid(proj[:, PROJ_PAD - LANES:PROJ_PAD] + bf_ref[...])

    @pl.when(i == 0)
    def _():
        carry_ref[...] = c0_ref[...]

    carry = carry_ref[...]
    pieces = []
    for s in range(tb // cum_block):
        blk = jnp.concatenate(_split3(lf[s * cum_block:(s + 1) * cum_block]), axis=-1)
        r = jnp.dot(tri_ref[...], blk, preferred_element_type=F32)
        c = r[:, 0:LANES] + r[:, LANES:2 * LANES] + r[:, 2 * LANES:3 * LANES] + carry
        carry = c[cum_block - 1:cum_block]
        pieces.append(c)
    c_all = pieces[0] if len(pieces) == 1 else jnp.concatenate(pieces, axis=0)
    carry_ref[...] = carry
    c2 = c_all * LOG2E
    lf_t = _nt_dot(sel_ref[...], lf, precision=HIGHEST)

    if prompt:
        qx_ref, kx_ref, vb_ref, kt_ref, vt_ref, lft_ref, u_ref, z_ref, gb_ref = outs[:-1]
        head_lane = lax.broadcasted_iota(jnp.int32, (tb, LANES), 1) < N_HEADS
        p1, p2, p3 = (p.astype(F32) for p in _split3(jnp.where(head_lane, c2, 0.0)))
        pcs = (p1 + pltpu.roll(p2, N_HEADS, 1) + pltpu.roll(p3, 2 * N_HEADS, 1)).astype(BF16)
        row = lax.broadcasted_iota(jnp.int32, (ATT_DIM, tb), 0) % LANES
        q_one = (row < 2 * AUG_STRIDE) & (row % AUG_STRIDE >= BIAS_PIECES) & (row % AUG_STRIDE < 2 * BIAS_PIECES)
        qa_t = _nt_dot(sq_ref[...], pcs) + q_one.astype(F32)
        lane = lax.broadcasted_iota(jnp.int32, (tb, ATT_DIM), 1) % LANES
        k_one = (lane < 2 * AUG_STRIDE) & (lane % AUG_STRIDE < BIAS_PIECES)
        ka = jnp.dot(pcs, sk_ref[...], preferred_element_type=F32) + k_one.astype(F32)
        qtb, qatb, kb, kab = q.T.astype(BF16), qa_t.astype(BF16), k.astype(BF16), ka.astype(BF16)
        qx, kx = [], []
        for p in range(HEAD_PAIRS):
            sl = slice(p * LANES, (p + 1) * LANES)
            qx += [qtb[sl], qatb[sl]]
            kx += [kb[:, sl], kab[:, sl]]
        qx_ref[...] = jnp.concatenate(qx, axis=0)
        kx_ref[...] = jnp.concatenate(kx, axis=-1)
    else:
        q_ref, kb_ref, vb_ref, kf_ref, vf_ref, lft_ref, cq_ref, ct_ref, u_ref, z_ref, gb_ref = outs[:-1]
        q_ref[...] = q.astype(BF16)
        kb_ref[...] = k.astype(BF16)
        vb_ref[...] = v.astype(BF16)
        kf_ref[...] = k.reshape(tb, N_HEADS, HEAD_DIM)
        vf_ref[...] = v.reshape(tb, N_HEADS, HEAD_DIM)
        cq_ref[...] = c2[:, 0:N_HEADS]
        ct_ref[...] = _nt_dot(sel_ref[...], c2, precision=HIGHEST)
    lft_ref[...] = lf_t


def _inproj(x, c0, lw, tb, layer, depth, prev, prompt):
    bsz, seq, _ = x.shape
    cum_block = min(tb, LANES)
    tri = (lax.broadcasted_iota(jnp.int32, (cum_block, cum_block), 1)
           <= lax.broadcasted_iota(jnp.int32, (cum_block, cum_block), 0)).astype(BF16)
    sel =(lax.broadcasted_iota(jnp.int32, (N_HEADS, LANES), 0)
           == lax.broadcasted_iota(jnp.int32, (N_HEADS, LANES), 1)).astype(F32)
    sq, sk = _bias_selectors()
    tok = lambda w: pl.BlockSpec((None, tb, w), lambda b, i: (b, i, 0))
    head_major = pl.BlockSpec((None, N_HEADS, tb), lambda b, i: (b, 0, i))
    tok_shape = lambda w, dt: jax.ShapeDtypeStruct((bsz, seq, w), dt)
    heads_shape = jax.ShapeDtypeStruct((bsz, N_HEADS, seq), F32)
    if prompt:
        state_shape = jax.ShapeDtypeStruct((depth, bsz, ATT_DIM, seq), F32)
        state_spec = pl.BlockSpec((None, None, ATT_DIM, tb), lambda b, i: (layer, b, 0, i))
        rows_major = lambda r: pl.BlockSpec((None, r, tb), lambda b, i: (b, 0, i))
        out_shapes = (jax.ShapeDtypeStruct((bsz, HEAD_PAIRS * PAIR_EXT, seq), BF16),
                      tok_shape(HEAD_PAIRS * PAIR_EXT, BF16),
                      jax.ShapeDtypeStruct((bsz, ATT_DIM, seq), BF16),
                      state_shape, state_shape, heads_shape,
                      tok_shape(SSM_DIM, F32), tok_shape(CONV_DIM, F32), tok_shape(CONV_DIM, F32))
        out_specs = (rows_major(HEAD_PAIRS * PAIR_EXT), tok(HEAD_PAIRS * PAIR_EXT), rows_major(ATT_DIM),
                     state_spec, state_spec, head_major,
                     tok(SSM_DIM), tok(CONV_DIM), tok(CONV_DIM))
        state_out = (3, 4)
    else:
        state_shape = jax.ShapeDtypeStruct((depth, bsz, seq, N_HEADS, HEAD_DIM), F32)
        state_spec = pl.BlockSpec((None, None, tb, N_HEADS, HEAD_DIM), lambda b, i: (layer, b, i, 0, 0))
        out_shapes = (tok_shape(ATT_DIM, BF16), tok_shape(ATT_DIM, BF16), tok_shape(ATT_DIM, BF16),
                      state_shape, state_shape, heads_shape,
                      tok_shape(N_HEADS, F32), heads_shape,
                      tok_shape(SSM_DIM, F32), tok_shape(CONV_DIM, F32), tok_shape(CONV_DIM, F32))
        out_specs = (tok(ATT_DIM), tok(ATT_DIM), tok(ATT_DIM), state_spec, state_spec, head_major,
                     tok(N_HEADS), head_major, tok(SSM_DIM), tok(CONV_DIM), tok(CONV_DIM))
        state_out = (3, 4)
    in_specs = [tok(D_MODEL), _layer_spec((1, D_MODEL), layer), _layer_spec((D_MODEL, PROJ_PAD), layer),
                _layer_spec((1, ATT_DIM), layer), _layer_spec((1, ATT_DIM), layer), _layer_spec((1, LANES), layer),
                _const_spec((ATT_DIM, ATT_DIM)), _const_spec((cum_block, cum_block)),
                _const_spec((N_HEADS, LANES)),
                pl.BlockSpec((None, 1, LANES), lambda b, i: (b, 0, 0)),
                _const_spec(sq.shape), _const_spec(sk.shape)]
    args = [x, lw["ln1"], lw["w_in"], lw["qn"], lw["kn"], lw["bf"], lw["mbd"], tri, sel, c0, sq, sk]
    aliases = {}
    n_prev = 0
    if prev is not None:
        n_prev = len(prev)
        for n, buf in enumerate(prev):
            aliases[len(args)] = state_out[n]
            args.append(buf)
            in_specs.append(pl.BlockSpec(memory_space=pl.ANY))
    return pl.pallas_call(
        functools.partial(_inproj_body, cum_block=cum_block, prompt=prompt, n_prev=n_prev),
        grid=(bsz, seq // tb),
        in_specs=in_specs, out_specs=out_specs, out_shape=out_shapes,
        scratch_shapes=[pltpu.VMEM((1, LANES), F32)],
        input_output_aliases=aliases,
        compiler_params=_params("arbitrary", "arbitrary"),
        name="inproj_prompt" if prompt else "inproj_sample",
    )(*args)


def _attn_body(qx_ref, kx_ref, vt_ref, o_ref, m_s, mb_s, acc_s, s_s, *, tq, pairs):
    nq = o_ref.shape[0] // tq
    row_x = lax.broadcasted_iota(jnp.int32, (PAIR_EXT, tq), 0)
    row_v = lax.broadcasted_iota(jnp.int32, (LANES, tq), 0)
    head_rows = [(row_v >= HEAD_DIM * a) & (row_v < HEAD_DIM * (a + 1)) for a in range(2)]
    mine = [(((row_x >= HEAD_DIM * a) & (row_x < HEAD_DIM * (a + 1)))
             | ((row_x >= LANES + AUG_STRIDE * a) & (row_x < LANES + AUG_STRIDE * (a + 1)))) for a in range(2)]
    half = tq // 2
    causal_top = (lax.broadcasted_iota(jnp.int32, (half, tq), 0) <= lax.broadcasted_iota(jnp.int32, (half, tq), 1))
    causal_bot = (lax.broadcasted_iota(jnp.int32, (half, half), 0) <= lax.broadcasted_iota(jnp.int32, (half, half), 1))

    def col_max(s):
        return jnp.max(s, axis=0, keepdims=True)

    def scores(qi, j, p, diagonal=False):
        cols = pl.ds(pl.multiple_of(qi * tq, tq), tq)
        qt = qx_ref[p * PAIR_EXT:(p + 1) * PAIR_EXT, cols]
        rows = pl.ds(pl.multiple_of(j * tq, tq), tq)
        kx = kx_ref[rows, p * PAIR_EXT:(p + 1) * PAIR_EXT]
        for a in range(2):
            h = 2 * p + a
            qa = jnp.where(mine[a], qt, jnp.zeros_like(qt))
            if diagonal:
                s_s[h, 0:half, :] = jnp.dot(kx[0:half], qa, preferred_element_type=F32)
                s_s[h, half:tq, half:tq] = jnp.dot(kx[half:tq], qa[:, half:tq], preferred_element_type=F32)
            else:
                s = jnp.dot(kx, qa, preferred_element_type=F32)
                s_s[h] = s
                mb_s[h] = col_max(s)

    def absorb(j, p, diagonal):
        cols = pl.ds(pl.multiple_of(j * tq, tq), tq)
        vt = vt_ref[p * LANES:(p + 1) * LANES, cols]
        for a in range(2):
            h = 2 * p + a
            va = jnp.where(head_rows[a], vt, jnp.ones_like(vt))
            m_old = m_s[h]
            if diagonal:
                top = jnp.where(causal_top, s_s[h, 0:half, :], NEG_INF)
                bot = jnp.where(causal_bot, s_s[h, half:tq, half:tq], NEG_INF)
                m_top = col_max(top)
                m_blk = jnp.concatenate([m_top[:, 0:half], jnp.maximum(m_top[:, half:tq], col_max(bot))], axis=1)
                m_new = jnp.maximum(m_old, m_blk)
                out = jnp.dot(va[:, 0:half], jnp.exp2(top - m_new[0:1]).astype(BF16), preferred_element_type=F32)
                out_late = jnp.dot(va[:, half:tq], jnp.exp2(bot - m_new[0:1, half:tq]).astype(BF16),
                                   preferred_element_type=F32)
                out = jnp.concatenate([out[:, 0:half], out[:, half:tq] + out_late], axis=1)
            else:
                m_new = jnp.maximum(m_old, mb_s[h])
                out = jnp.dot(va, jnp.exp2(s_s[h] - m_new[0:1]).astype(BF16), preferred_element_type=F32)
            acc_s[h] = jnp.exp2(m_old - m_new)[0:1] * acc_s[h] + out
            m_s[h] = m_new

    def reset():
        for h in range(2 * pairs):
            m_s[h] = jnp.full((1, tq), NEG_INF, F32)
            acc_s[h] = jnp.zeros((LANES, tq), F32)

    def key_block(qi, j, diagonal, nxt):
        for p in range(pairs):
            if p + 1 < pairs:
                scores(qi, j, p + 1, diagonal)
            else:
                scores(nxt[0], nxt[1], 0)
            absorb(j, p, diagonal)

    def query_block(qi, carry):
        def body(j, c):
            key_block(qi, j, False, (qi, j + 1))
            return c

        lax.fori_loop(0, qi, body, 0)
        key_block(qi, qi, True, (jnp.minimum(qi + 1, nq - 1), 0))
        rows = pl.ds(pl.multiple_of(qi * tq, tq), tq)
        for p in range(pairs):
            a0, a1 = acc_s[2 * p], acc_s[2 * p + 1]
            out_t = jnp.concatenate([a0[0:HEAD_DIM] / a0[HEAD_DIM:LANES],
                                     a1[HEAD_DIM:LANES] / a1[0:HEAD_DIM]], axis=0)
            o_ref[rows, p * LANES:(p + 1) * LANES] = out_t.T
        reset()
        return carry

    reset()
    scores(0, 0, 0)
    lax.fori_loop(0, nq, query_block, 0)


def _attn_prompt(qx_t, kx, vt, tq, pairs):
    bsz, seq, _ = kx.shape
    return pl.pallas_call(
        functools.partial(_attn_body, tq=tq, pairs=pairs),
        grid=(bsz, HEAD_PAIRS // pairs),
        in_specs=[pl.BlockSpec((None, pairs * PAIR_EXT, seq), lambda b, h: (b, h, 0)),
                  pl.BlockSpec((None, seq, pairs * PAIR_EXT), lambda b, h: (b, 0, h)),
                  pl.BlockSpec((None, pairs * LANES, seq), lambda b, h: (b, h, 0))],
        out_specs=pl.BlockSpec((None, seq, pairs * LANES), lambda b, h: (b, 0, h)),
        out_shape=jax.ShapeDtypeStruct((bsz, seq, ATT_DIM), F32),
        scratch_shapes=[pltpu.VMEM((2 * pairs, 1, tq), F32), pltpu.VMEM((2 * pairs, 1, tq), F32),
                        pltpu.VMEM((2 * pairs, LANES, tq), F32),
                        pltpu.VMEM((2 * pairs, tq, tq), F32)],
        compiler_params=_params("arbitrary", "arbitrary"),
        name="attn_prompt",
    )(qx_t, kx, vt)


def _attn_sample_body(q_ref, ckt_ref, cvt_ref, kn_ref, vn_ref, cq_ref, ctc_ref, ctn_ref, o_ref, *, pairs):
    tq = q_ref.shape[0]
    past = ckt_ref.shape[-1]
    cq8 = cq_ref[...]
    lane = lax.broadcasted_iota(jnp.int32, (tq, LANES), 1)
    lane8 = lax.broadcasted_iota(jnp.int32, (tq, N_HEADS), 1)
    row = lax.broadcasted_iota(jnp.int32, (tq, tq), 0)
    col = lax.broadcasted_iota(jnp.int32, (tq, tq), 1)
    for pp in range(pairs):
        hp = pl.program_id(1) * pairs + pp
        lanes = slice(pp * LANES, (pp + 1) * LANES)
        q = q_ref[:, lanes]
        q2 = jnp.concatenate(
            [jnp.where((lane >= HEAD_DIM * a) & (lane < HEAD_DIM * (a + 1)), q, jnp.zeros_like(q)) for a in range(2)],
            axis=0)
        kt = ckt_ref[2 * pp:2 * pp + 2].reshape(LANES, past).astype(BF16)
        vt = cvt_ref[2 * pp:2 * pp + 2].reshape(LANES, past).astype(BF16)
        s_cache = jnp.dot(q2, kt, preferred_element_type=F32)
        s_new = _nt_dot(q2, kn_ref[:, lanes])
        sc, sn = [], []
        for a in range(2):
            head = 2 * hp + a
            cqa = jnp.sum(jnp.where(lane8 == head, cq8, 0.0), axis=-1, keepdims=True)
            ck_cache = ctc_ref[pl.ds(head, 1), :] * LOG2E
            ck_new = ctn_ref[pl.ds(head, 1), :]
            sc.append(s_cache[a * tq:(a + 1) * tq] + cqa - ck_cache)
            sn.append(jnp.where(col <= row, s_new[a * tq:(a + 1) * tq] + cqa - ck_new, NEG_INF))
        s_cache = jnp.concatenate(sc, axis=0)
        s_new = jnp.concatenate(sn, axis=0)
        m = jnp.maximum(jnp.max(s_cache, axis=-1, keepdims=True), jnp.max(s_new, axis=-1, keepdims=True))
        p_cache = jnp.exp2(s_cache - m)
        p_new = jnp.exp2(s_new - m)
        l = jnp.sum(p_cache, axis=-1, keepdims=True) + jnp.sum(p_new, axis=-1, keepdims=True)
        o = (_nt_dot(p_cache.astype(BF16), vt)
             + jnp.dot(p_new.astype(BF16), vn_ref[:, lanes], preferred_element_type=F32)) / l
        o_ref[:, lanes] = jnp.where(lane < HEAD_DIM, o[0:tq], o[tq:2 * tq])


def _attn_sample(q, kb, vb, cq, ct_new, cache_kt, cache_vt, ct_cache, layer, pairs=4):
    bsz, seq, _ = q.shape
    past = cache_kt.shape[-1]
    cache_blk = pl.BlockSpec((None, None, 2 * pairs, HEAD_DIM, past), lambda b, h: (layer, b, h, 0, 0))
    new_blk = pl.BlockSpec((None, seq, pairs * LANES), lambda b, h: (b, 0, h))
    return pl.pallas_call(
        functools.partial(_attn_sample_body, pairs=pairs),
        grid=(bsz, HEAD_PAIRS // pairs),
        in_specs=[new_blk, cache_blk, cache_blk, new_blk, new_blk,
                  pl.BlockSpec((None, seq, N_HEADS), lambda b, h: (b, 0, 0)),
                  pl.BlockSpec((None, N_HEADS, past), lambda b, h: (b, 0, 0)),
                  pl.BlockSpec((None, N_HEADS, seq), lambda b, h: (b, 0, 0))],
        out_specs=new_blk,
        out_shape=jax.ShapeDtypeStruct((bsz, seq, ATT_DIM), F32),
        compiler_params=_params("arbitrary", "arbitrary"),
        name="attn_sample",
    )(q, cache_kt, cache_vt, kb, vb, cq, ct_cache, ct_new)


def _cache_cumsum_body(x_ref, o_ref):
    rows, past = x_ref.shape
    tri = (lax.broadcasted_iota(jnp.int32, (LANES, LANES), 0)
           <= lax.broadcasted_iota(jnp.int32, (LANES, LANES), 1)).astype(F32)
    carry = jnp.zeros((rows, 1), F32)
    for b in range(past // LANES):
        blk = x_ref[:, b * LANES:(b + 1) * LANES]
        c = jnp.dot(blk, tri, precision=HIGHEST, preferred_element_type=F32) + carry
        o_ref[:, b * LANES:(b + 1) * LANES] = c
        carry = c[:, LANES - 1:LANES]


def _cache_cumsum(x):
    return pl.pallas_call(
        _cache_cumsum_body,
        out_shape=jax.ShapeDtypeStruct(x.shape, F32),
        compiler_params=pltpu.CompilerParams(vmem_limit_bytes=VMEM_LIMIT),
        name="cache_cumsum",
    )(x)


def _s5_prep_body(lr_ref, li_ref, ldt_ref, br_ref, bi_ref, ar_ref, ai_ref, bbr_ref, bbi_ref):
    lr = jnp.minimum(lr_ref[...], -1e-4)
    li = li_ref[...]
    dt = jnp.exp(ldt_ref[...])
    ldr, ldi = lr * dt, li * dt
    mag = jnp.exp(ldr)
    ar, ai = mag * jnp.cos(ldi), mag * jnp.sin(ldi)
    den = lr * lr + li * li
    nr = ar - 1.0
    qr = (nr * lr + ai * li) / den
    qi = (ai * lr - nr * li) / den
    ar_ref[...] = ar
    ai_ref[...] = ai
    br, bi = br_ref[...], bi_ref[...]
    bbr_ref[...] = qr[:, None, :] * br - qi[:, None, :] * bi
    bbi_ref[...] = qr[:, None, :] * bi + qi[:, None, :] * br


def _s5_prep(lam_re, lam_im, log_dt, b_re, b_im):
    g, p = lam_re.shape
    gp = jax.ShapeDtypeStruct((g, p), F32)
    gcp = jax.ShapeDtypeStruct((g, SSM_GROUP, p), F32)
    return pl.pallas_call(
        _s5_prep_body, out_shape=(gp, gp, gcp, gcp), name="s5_prep",
    )(lam_re, lam_im, log_dt.reshape(g, 1), jnp.swapaxes(b_re, 1, 2), jnp.swapaxes(b_im, 1, 2))


def _block_diag(t):
    d, g, c, p = t.shape
    eye = jnp.eye(g, dtype=t.dtype)
    return (t[:, :, :, None, :] * eye[None, :, None, :, None]).reshape(d, g * c, g * p)


def _ssm_body(u_ref, z_ref, gb_ref, h0r_ref, h0i_ref, cprev_ref,
              bre_ref, bim_ref, cre_ref, cim_ref, ar_ref, ai_ref, d_ref, wglu_ref, bglu_ref, cw_ref,
              ssm_ref, conv_ref, hr_out, hi_out, cst_out,
              sre, sim, hr_s, hi_s, zprev_s, *, lb):
    j = pl.program_id(1)
    nb = SUBLANES
    tiles = STATE_LANES // LANES

    @pl.when(j == 0)
    def _():
        hr_s[...] = h0r_ref[...]
        hi_s[...] = h0i_ref[...]
        zprev_s[...] = cprev_ref[...]

    u2 = jnp.swapaxes(u_ref[...], 0, 1).reshape(lb * nb, SSM_DIM)
    ub = u2.astype(BF16)
    bu_re = jnp.dot(ub, bre_ref[...], preferred_element_type=F32)
    bu_im = jnp.dot(ub, bim_ref[...], preferred_element_type=F32)
    for c in range(tiles):
        sre[c] = bu_re[:, c * LANES:(c + 1) * LANES]
        sim[c] = bu_im[:, c * LANES:(c + 1) * LANES]
    ar = [jnp.broadcast_to(ar_ref[:, c * LANES:(c + 1) * LANES], (nb, LANES)) for c in range(tiles)]
    ai = [jnp.broadcast_to(ai_ref[:, c * LANES:(c + 1) * LANES], (nb, LANES)) for c in range(tiles)]

    def step(t, carry):
        hr, hi = carry
        rows = pl.ds(pl.multiple_of(t * nb, nb), nb)
        nr, ni = [], []
        for c in range(tiles):
            r = ar[c] * hr[c] - ai[c] * hi[c] + sre[c, rows, :]
            i = ar[c] * hi[c] + ai[c] * hr[c] + sim[c, rows, :]
            sre[c, rows, :] = r
            sim[c, rows, :] = i
            nr.append(r)
            ni.append(i)
        return tuple(nr), tuple(ni)

    init = (tuple(hr_s[:, c * LANES:(c + 1) * LANES] for c in range(tiles)),
            tuple(hi_s[:, c * LANES:(c + 1) * LANES] for c in range(tiles)))
    hr, hi = lax.fori_loop(0, lb, step, init, unroll=2)
    for c in range(tiles):
        hr_s[:, c * LANES:(c + 1) * LANES] = hr[c]
        hi_s[:, c * LANES:(c + 1) * LANES] = hi[c]
        hr_out[:, c * LANES:(c + 1) * LANES] = hr[c]
        hi_out[:, c * LANES:(c + 1) * LANES] = hi[c]

    tt = lax.broadcasted_iota(jnp.int32, (lb, CONV_DIM), 0)
    w0, w1, w2 = cw_ref[0:1, :], cw_ref[1:2, :], cw_ref[2:3, :]
    for b in range(nb):
        zb = z_ref[b]
        prev = zprev_s[b]
        z1 = jnp.where(tt == 0, prev[1:2], pltpu.roll(zb, 1, 0))
        z2 = jnp.where(tt == 0, prev[0:1], jnp.where(tt == 1, prev[1:2], pltpu.roll(zb, 2, 0)))
        conv_ref[b] = gb_ref[b] * (w0 * z2 + w1 * z1 + w2 * zb)
        last = zb[lb - (CONV_WIDTH - 1):lb]
        zprev_s[b] = last
        cst_out[b] = last

    h_re = jnp.concatenate([sre[c] for c in range(tiles)], axis=-1).astype(BF16)
    h_im = jnp.concatenate([sim[c] for c in range(tiles)], axis=-1).astype(BF16)
    y = (jnp.dot(h_re, cre_ref[...], preferred_element_type=F32)
         - jnp.dot(h_im, cim_ref[...], preferred_element_type=F32))
    y = y + d_ref[...] * u2
    g = jax.nn.gelu(y)
    gate = jax.nn.sigmoid(jnp.dot(g.astype(BF16), wglu_ref[...], preferred_element_type=F32) + bglu_ref[...])
    ssm_ref[...] = jnp.swapaxes((g * gate).reshape(lb, nb, SSM_DIM), 0, 1)


def _ssm_conv(u, z, gb, h0r, h0i, cprev, lw, layer, lb):
    bsz, seq, _ = u.shape
    nb = SUBLANES
    tok = pl.BlockSpec((nb, lb, SSM_DIM), lambda g, j: (g, j, 0))
    st = pl.BlockSpec((nb, STATE_LANES), lambda g, j: (g, 0))
    cs = pl.BlockSpec((nb, CONV_WIDTH - 1, CONV_DIM), lambda g, j: (g, 0, 0))
    par = lambda *shape: _layer_spec(shape, layer)
    return pl.pallas_call(
        functools.partial(_ssm_body, lb=lb),
        grid=(bsz // nb, seq // lb),
        in_specs=[tok, tok, tok, st, st, cs,
                  par(SSM_DIM, STATE_LANES), par(SSM_DIM, STATE_LANES),
                  par(STATE_LANES, SSM_DIM), par(STATE_LANES, SSM_DIM),
                  par(1, STATE_LANES), par(1, STATE_LANES),
                  par(1, SSM_DIM), par(SSM_DIM, SSM_DIM), par(1, SSM_DIM),
                  par(CONV_WIDTH, CONV_DIM)],
        out_specs=(tok, tok, st, st, cs),
        out_shape=(jax.ShapeDtypeStruct((bsz, seq, SSM_DIM), F32),
                   jax.ShapeDtypeStruct((bsz, seq, CONV_DIM), F32),
                   jax.ShapeDtypeStruct((bsz, STATE_LANES), F32),
                   jax.ShapeDtypeStruct((bsz, STATE_LANES), F32),
                   jax.ShapeDtypeStruct((bsz, CONV_WIDTH - 1, CONV_DIM), F32)),
        scratch_shapes=[pltpu.VMEM((STATE_LANES // LANES, nb * lb, LANES), F32),
                        pltpu.VMEM((STATE_LANES // LANES, nb * lb, LANES), F32),
                        pltpu.VMEM((nb, STATE_LANES), F32), pltpu.VMEM((nb, STATE_LANES), F32),
                        pltpu.VMEM((nb, CONV_WIDTH - 1, CONV_DIM), F32)],
        compiler_params=_params("arbitrary", "arbitrary"),
        name="ssm_conv",
    )(u, z, gb, h0r, h0i, cprev, lw["b_re_blk"], lw["b_im_blk"], lw["c_re_blk"], lw["c_im_blk"],
      lw["a_re"], lw["a_im"], lw["d"], lw["w_glu"], lw["b_glu"], lw["conv_w"])


def _rms_rows(t, w):
    return t * lax.rsqrt(jnp.mean(t * t, axis=-1, keepdims=True) + RMS_EPS) * w


def _mix_ffn_body(x_ref, att_ref, ssm_ref, conv_ref, bn_ref, wout_ref, ln2_ref, wup_ref, wdown_ref, o_ref,
                  *, ff_chunk):
    bn = bn_ref[...]
    mix = jnp.concatenate(
        [_rms_rows(att_ref[...], bn[:, 0:ATT_DIM]),
         _rms_rows(ssm_ref[...], bn[:, ATT_DIM:ATT_DIM + SSM_DIM]),
         _rms_rows(conv_ref[...], bn[:, ATT_DIM + SSM_DIM:])], axis=-1).astype(BF16)
    x1 = x_ref[...] + jnp.dot(mix, wout_ref[...], preferred_element_type=F32)
    h2 = _rms_rows(x1, ln2_ref[...]).astype(BF16)
    acc = x1
    for c in range(D_FF // ff_chunk):
        f = jnp.maximum(jnp.dot(h2, wup_ref[:, c * ff_chunk:(c + 1) * ff_chunk], preferred_element_type=F32), 0.0)
        acc = acc + jnp.dot((f * f).astype(BF16), wdown_ref[c * ff_chunk:(c + 1) * ff_chunk, :],
                            preferred_element_type=F32)
    o_ref[...] = acc


def _mix_ffn(x, att, ssm, conv, lw, layer, tb):
    bsz, seq, _ = x.shape
    rows = bsz * seq
    tok = lambda w: pl.BlockSpec((tb, w), lambda i: (i, 0))
    once = lambda *shape: _layer_spec(shape, layer, pipeline_mode=pl.Buffered(1))
    out = pl.pallas_call(
        functools.partial(_mix_ffn_body, ff_chunk=1024),
        grid=(rows // tb,),
        in_specs=[tok(D_MODEL), tok(ATT_DIM), tok(SSM_DIM), tok(CONV_DIM),
                  once(1, D_MODEL), once(D_MODEL, D_MODEL), once(1, D_MODEL),
                  once(D_MODEL, D_FF), once(D_FF, D_MODEL)],
        out_specs=tok(D_MODEL),
        out_shape=jax.ShapeDtypeStruct((rows, D_MODEL), F32),
        compiler_params=_params("arbitrary"),
        name="mix_ffn",
    )(x.reshape(rows, D_MODEL), att.reshape(rows, ATT_DIM), ssm.reshape(rows, SSM_DIM),
      conv.reshape(rows, CONV_DIM), lw["bn"], lw["w_out"], lw["ln2"], lw["w_up"], lw["w_down"])
    return out.reshape(bsz, seq, D_MODEL)


def _prep_weights(ln1_w, w_in, b_forget, q_norm_w, k_norm_w, conv_w, ssm_lam_re, ssm_lam_im, ssm_log_dt,
                  ssm_b_re, ssm_b_im, ssm_c_re, ssm_c_im, ssm_d, w_glu, b_glu, branch_norm_w, w_out,
                  ln2_w, w_up, w_down):
    depth = w_in.shape[0]
    fg0 = 3 * ATT_DIM
    w_re = jnp.concatenate([w_in[:, :, :fg0], w_in[:, :, fg0 + N_HEADS:], w_in[:, :, fg0:fg0 + N_HEADS],
                            jnp.zeros((depth, D_MODEL, LANES - N_HEADS), F32)], axis=2).astype(BF16)
    head_id = jnp.arange(ATT_DIM) // HEAD_DIM
    groups = depth * SSM_GROUPS
    a_re, a_im, bb_re, bb_im = _s5_prep(
        ssm_lam_re.reshape(groups, SSM_STATE), ssm_lam_im.reshape(groups, SSM_STATE), ssm_log_dt.reshape(groups),
        ssm_b_re.reshape(groups, SSM_STATE, SSM_GROUP), ssm_b_im.reshape(groups, SSM_STATE, SSM_GROUP))
    per_layer = lambda t: t.reshape(depth, SSM_GROUPS, SSM_GROUP, SSM_STATE)
    row = lambda t, w: t.reshape(depth, 1, w)
    return dict(
        ln1=row(ln1_w, D_MODEL), w_in=w_re,
        qn=row(jnp.tile(q_norm_w, (1, N_HEADS)), ATT_DIM), kn=row(jnp.tile(k_norm_w, (1, N_HEADS)), ATT_DIM),
        bf=row(jnp.pad(b_forget, ((0, 0), (0, LANES - N_HEADS))), LANES),
        mbd=(head_id[:, None] == head_id[None, :]).astype(BF16),
        a_re=row(a_re, STATE_LANES), a_im=row(a_im, STATE_LANES),
        b_re_blk=_block_diag(per_layer(bb_re)).astype(BF16), b_im_blk=_block_diag(per_layer(bb_im)).astype(BF16),
        c_re_blk=jnp.swapaxes(_block_diag(ssm_c_re), 1, 2).astype(BF16),
        c_im_blk=jnp.swapaxes(_block_diag(ssm_c_im), 1, 2).astype(BF16),
        d=row(ssm_d, SSM_DIM), w_glu=w_glu.astype(BF16), b_glu=row(b_glu, SSM_DIM),
        conv_w=conv_w, bn=row(branch_norm_w, D_MODEL), w_out=w_out.astype(BF16),
        ln2=row(ln2_w, D_MODEL), w_up=w_up.astype(BF16), w_down=w_down.astype(BF16))


def _run_layer(x, lw, layer, depth, prev, past, tiles):
    bsz, seq, _ = x.shape
    if past is None:
        c0 = jnp.zeros((bsz, 1, LANES), F32)
        h0r = jnp.zeros((bsz, STATE_LANES), F32)
        h0i = h0r
        cprev = jnp.zeros((bsz, CONV_WIDTH - 1, CONV_DIM), F32)
        qx, kx, vb, k_all, v_all, lft, u, z, gb = _inproj(x, c0, lw, tiles["tok"], layer, depth, prev, True)
        att = _attn_prompt(qx, kx, vb, tiles["attn"], tiles["attn_pairs"])
    else:
        cache_kt, cache_vt, clf_t, s_re, s_im, s_conv = past
        plen = cache_kt.shape[-1]
        ct_cache = _cache_cumsum(clf_t.reshape(bsz * N_HEADS, plen)).reshape(bsz, N_HEADS, plen)
        c0 = jnp.pad(ct_cache[:, :, plen - 1], ((0, 0), (0, LANES - N_HEADS))).reshape(bsz, 1, LANES)
        h0r = s_re.reshape(bsz, STATE_LANES)
        h0i = s_im.reshape(bsz, STATE_LANES)
        cprev = s_conv
        q, kb, vb, k_all, v_all, lft, cq, ct, u, z, gb = _inproj(x, c0, lw, tiles["tok"], layer, depth, prev, False)
        att = _attn_sample(q, kb, vb, cq, ct, cache_kt, cache_vt, ct_cache, layer)
    ssm, conv, hr, hi, cst = _ssm_conv(u, z, gb, h0r, h0i, cprev, lw, layer, tiles["scan"])
    y = _mix_ffn(x, att, ssm, conv, lw, layer, tiles["ffn"])
    small = (lft, hr.reshape(bsz, SSM_GROUPS, SSM_STATE), hi.reshape(bsz, SSM_GROUPS, SSM_STATE), cst)
    return y, (k_all, v_all), small


def _tiles(bsz, seq):
    tok = min(seq, 1024)
    return dict(tok=tok, attn=min(seq, 512), attn_pairs=2, scan=min(seq, 128), ffn=min(bsz * seq, 1024))


def _forward(x_prompt, x_sample, cache_k, cache_v, cache_logf, state_ssm_re, state_ssm_im, state_conv, *weights):
    depth = cache_k.shape[0]
    xp, xs = x_prompt, x_sample
    bp, lp = xp.shape[0], xp.shape[1]
    tiles_p = _tiles(bp, lp)
    tiles_s = _tiles(xs.shape[0], xs.shape[1])
    cache_kt = jnp.transpose(cache_k, (0, 1, 3, 4, 2))
    cache_vt = jnp.transpose(cache_v, (0, 1, 3, 4, 2))
    cache_lft = jnp.swapaxes(cache_logf, 2, 3)
    kv_p, kv_s, small_p, small_s = None, None, [], []
    lw = _prep_weights(*weights)
    for l in range(depth):
        xp, kv_p, sp = _run_layer(xp, lw, l, depth, kv_p, None, tiles_p)
        xs, kv_s, ss = _run_layer(xs, lw, l, depth, kv_s,
                                  (cache_kt, cache_vt, cache_lft[l], state_ssm_re[l], state_ssm_im[l], state_conv[l]),
                                  tiles_s)
        small_p.append(sp)
        small_s.append(ss)
    stk = lambda lst, i: jnp.stack([s[i] for s in lst])
    from_t = lambda t: jnp.transpose(t.reshape(depth, bp, N_HEADS, HEAD_DIM, lp), (0, 1, 4, 2, 3))
    return (xp, xs,
            from_t(kv_p[0]), from_t(kv_p[1]), jnp.swapaxes(stk(small_p, 0), 2, 3),
            stk(small_p, 1), stk(small_p, 2), stk(small_p, 3),
            kv_s[0], kv_s[1], jnp.swapaxes(stk(small_s, 0), 2, 3),
            stk(small_s, 1), stk(small_s, 2), stk(small_s, 3))


def kernel(x_prompt, x_sample, cache_k, cache_v, cache_logf, state_ssm_re, state_ssm_im, state_conv, ln1_w, w_in, b_forget, q_norm_w, k_norm_w, conv_w, ssm_lam_re, ssm_lam_im, ssm_log_dt, ssm_b_re, ssm_b_im, ssm_c_re, ssm_c_im, ssm_d, w_glu, b_glu, branch_norm_w, w_out, ln2_w, w_up, w_down):
    return _forward(x_prompt, x_sample, cache_k, cache_v, cache_logf, state_ssm_re, state_ssm_im, state_conv,
                    ln1_w, w_in, b_forget, q_norm_w, k_norm_w, conv_w, ssm_lam_re, ssm_lam_im, ssm_log_dt,
                    ssm_b_re, ssm_b_im, ssm_c_re, ssm_c_im, ssm_d, w_glu, b_glu, branch_norm_w, w_out,
                    ln2_w, w_up, w_down)
```

```python
import functools
import math

import jax
import jax.numpy as jnp
from jax import lax
from jax.experimental import pallas as pl
from jax.experimental.pallas import tpu as pltpu

F32 = jnp.float32
BF16 = jnp.bfloat16
HIGHEST = lax.Precision.HIGHEST

D_MODEL = 1024
N_HEADS = 8
HEAD_DIM = 64
ATT_DIM = N_HEADS * HEAD_DIM
SSM_DIM = 256
SSM_GROUP = 16
SSM_GROUPS = SSM_DIM // SSM_GROUP
SSM_STATE = 64
STATE_LANES = SSM_GROUPS * SSM_STATE
CONV_DIM = 256
CONV_WIDTH = 3
D_FF = 4 * D_MODEL
RMS_EPS = 1e-6
NEG_INF = -1e30
LOG2E = math.log2(math.e)

LANES = 128
SUBLANES = 8
HEAD_PAIRS = ATT_DIM // LANES
PAIR_EXT = 2 * LANES
AUG_STRIDE = 8
BIAS_PIECES = 3
PROJ_PAD = 3 * ATT_DIM + SSM_DIM + 3 * CONV_DIM + LANES
VMEM_LIMIT = 56 * 1024 * 1024


def _params(*sem):
    return pltpu.CompilerParams(dimension_semantics=sem, vmem_limit_bytes=VMEM_LIMIT)


def _const_spec(shape):
    zeros = (0,) * len(shape)
    return pl.BlockSpec(shape, lambda *_: zeros)


def _layer_spec(shape, layer, **kw):
    index = (layer,) + (0,) * len(shape)
    return pl.BlockSpec((None,) + tuple(shape), lambda *_: index, **kw)


def _nt_dot(a, b, **kw):
    return lax.dot_general(a, b, (((1,), (1,)), ((), ())), preferred_element_type=F32, **kw)


def _bias_selectors():
    h = jnp.arange(N_HEADS)
    base = LANES * (h // 2) + AUG_STRIDE * (h % 2)
    sq = jnp.zeros((ATT_DIM, LANES), F32)
    sk = jnp.zeros((LANES, ATT_DIM), F32)
    for r in range(BIAS_PIECES):
        sq = sq.at[base + r, N_HEADS * r + h].set(1.0)
        sk = sk.at[N_HEADS * r + h, base + BIAS_PIECES + r].set(-1.0)
    return sq.astype(BF16), sk.astype(BF16)


def _split3(t):
    p1 = t.astype(BF16)
    r1 = t - p1.astype(F32)
    p2 = r1.astype(BF16)
    p3 = (r1 - p2.astype(F32)).astype(BF16)
    return p1, p2, p3


def _inproj_body(*refs, cum_block, prompt, n_prev):
    (x_ref, ln1_ref, w_ref, qnw_ref, knw_ref, bf_ref, mbd_ref, tri_ref, sel_ref, c0_ref, sq_ref, sk_ref) = refs[:12]
    outs = refs[12 + n_prev:]
    carry_ref = outs[-1]
    i = pl.program_id(1)
    x = x_ref[...]
    ms = jnp.mean(x * x, axis=-1, keepdims=True)
    h = (x * lax.rsqrt(ms + RMS_EPS) * ln1_ref[...]).astype(BF16)
    proj = jnp.dot(h, w_ref[...], preferred_element_type=F32)
    tb = x.shape[0]

    def head_norm(t, w):
        ss = jnp.dot((t * t).astype(BF16), mbd_ref[...], preferred_element_type=F32)
        return t * lax.rsqrt(ss * (1.0 / HEAD_DIM) + RMS_EPS) * w

    v = proj[:, 2 * ATT_DIM:3 * ATT_DIM]
    o = 3 * ATT_DIM
    u_ref, z_ref, gb_ref = outs[-4:-1]
    u_ref[...] = proj[:, o:o + SSM_DIM]
    gb_ref[...] = proj[:, o + SSM_DIM + CONV_DIM:o + SSM_DIM + 2 * CONV_DIM]
    z_ref[...] = (proj[:, o + SSM_DIM + 2 * CONV_DIM:o + SSM_DIM + 3 * CONV_DIM]
                  * proj[:, o + SSM_DIM:o + SSM_DIM + CONV_DIM])
    if prompt:
        v_t = v.T
        outs[2][...] = v_t.astype(BF16)
        outs[4][...] = v_t
    k = head_norm(proj[:, ATT_DIM:2 * ATT_DIM], knw_ref[...])
    if prompt:
        outs[3][...] = k.T
    q = head_norm(proj[:, 0:ATT_DIM], qnw_ref[...]) * (LOG2E * HEAD_DIM ** -0.5)
    lf = jax.nn.log_sigmoid(proj[:, PROJ_PAD - LANES:PROJ_PAD] + bf_ref[...])

    @pl.when(i == 0)
    def _():
        carry_ref[...] = c0_ref[...]

    carry = carry_ref[...]
    pieces = []
    for s in range(tb // cum_block):
        blk = jnp.concatenate(_split3(lf[s * cum_block:(s + 1) * cum_block]), axis=-1)
        r = jnp.dot(tri_ref[...], blk, preferred_element_type=F32)
        c = r[:, 0:LANES] + r[:, LANES:2 * LANES] + r[:, 2 * LANES:3 * LANES] + carry
        carry = c[cum_block - 1:cum_block]
        pieces.append(c)
    c_all = pieces[0] if len(pieces) == 1 else jnp.concatenate(pieces, axis=0)
    carry_ref[...] = carry
    c2 = c_all * LOG2E
    lf_t = _nt_dot(sel_ref[...], lf, precision=HIGHEST)

    if prompt:
        qx_ref, kx_ref, vb_ref, kt_ref, vt_ref, lft_ref, u_ref, z_ref, gb_ref = outs[:-1]
        head_lane = lax.broadcasted_iota(jnp.int32, (tb, LANES), 1) < N_HEADS
        p1, p2, p3 = (p.astype(F32) for p in _split3(jnp.where(head_lane, c2, 0.0)))
        pcs = (p1 + pltpu.roll(p2, N_HEADS, 1) + pltpu.roll(p3, 2 * N_HEADS, 1)).astype(BF16)
        row = lax.broadcasted_iota(jnp.int32, (ATT_DIM, tb), 0) % LANES
        q_one = (row < 2 * AUG_STRIDE) & (row % AUG_STRIDE >= BIAS_PIECES) & (row % AUG_STRIDE < 2 * BIAS_PIECES)
        qa_t = _nt_dot(sq_ref[...], pcs) + q_one.astype(F32)
        lane = lax.broadcasted_iota(jnp.int32, (tb, ATT_DIM), 1) % LANES
        k_one = (lane < 2 * AUG_STRIDE) & (lane % AUG_STRIDE < BIAS_PIECES)
        ka = jnp.dot(pcs, sk_ref[...], preferred_element_type=F32) + k_one.astype(F32)
        qtb, qatb, kb, kab = q.T.astype(BF16), qa_t.astype(BF16), k.astype(BF16), ka.astype(BF16)
        qx, kx = [], []
        for p in range(HEAD_PAIRS):
            sl = slice(p * LANES, (p + 1) * LANES)
            qx += [qtb[sl], qatb[sl]]
            kx += [kb[:, sl], kab[:, sl]]
        qx_ref[...] = jnp.concatenate(qx, axis=0)
        kx_ref[...] = jnp.concatenate(kx, axis=-1)
    else:
        q_ref, kb_ref, vb_ref, kf_ref, vf_ref, lft_ref, cq_ref, ct_ref, u_ref, z_ref, gb_ref = outs[:-1]
        q_ref[...] = q.astype(BF16)
        kb_ref[...] = k.astype(BF16)
        vb_ref[...] = v.astype(BF16)
        kf_ref[...] = k.reshape(tb, N_HEADS, HEAD_DIM)
        vf_ref[...] = v.reshape(tb, N_HEADS, HEAD_DIM)
        cq_ref[...] = c2[:, 0:N_HEADS]
        ct_ref[...] = _nt_dot(sel_ref[...], c2, precision=HIGHEST)
    lft_ref[...] = lf_t


def _inproj(x, c0, lw, tb, layer, depth, prev, prompt):
    bsz, seq, _ = x.shape
    cum_block = min(tb, LANES)
    tri = (lax.broadcasted_iota(jnp.int32, (cum_block, cum_block), 1)
           <= lax.broadcasted_iota(jnp.int32, (cum_block, cum_block), 0)).astype(BF16)
    sel =(lax.broadcasted_iota(jnp.int32, (N_HEADS, LANES), 0)
           == lax.broadcasted_iota(jnp.int32, (N_HEADS, LANES), 1)).astype(F32)
    sq, sk = _bias_selectors()
    tok = lambda w: pl.BlockSpec((None, tb, w), lambda b, i: (b, i, 0))
    head_major = pl.BlockSpec((None, N_HEADS, tb), lambda b, i: (b, 0, i))
    tok_shape = lambda w, dt: jax.ShapeDtypeStruct((bsz, seq, w), dt)
    heads_shape = jax.ShapeDtypeStruct((bsz, N_HEADS, seq), F32)
    if prompt:
        state_shape = jax.ShapeDtypeStruct((depth, bsz, ATT_DIM, seq), F32)
        state_spec = pl.BlockSpec((None, None, ATT_DIM, tb), lambda b, i: (layer, b, 0, i))
        rows_major = lambda r: pl.BlockSpec((None, r, tb), lambda b, i: (b, 0, i))
        out_shapes = (jax.ShapeDtypeStruct((bsz, HEAD_PAIRS * PAIR_EXT, seq), BF16),
                      tok_shape(HEAD_PAIRS * PAIR_EXT, BF16),
                      jax.ShapeDtypeStruct((bsz, ATT_DIM, seq), BF16),
                      state_shape, state_shape, heads_shape,
                      tok_shape(SSM_DIM, F32), tok_shape(CONV_DIM, F32), tok_shape(CONV_DIM, F32))
        out_specs = (rows_major(HEAD_PAIRS * PAIR_EXT), tok(HEAD_PAIRS * PAIR_EXT), rows_major(ATT_DIM),
                     state_spec, state_spec, head_major,
                     tok(SSM_DIM), tok(CONV_DIM), tok(CONV_DIM))
        state_out = (3, 4)
    else:
        state_shape = jax.ShapeDtypeStruct((depth, bsz, seq, N_HEADS, HEAD_DIM), F32)
        state_spec = pl.BlockSpec((None, None, tb, N_HEADS, HEAD_DIM), lambda b, i: (layer, b, i, 0, 0))
        out_shapes = (tok_shape(ATT_DIM, BF16), tok_shape(ATT_DIM, BF16), tok_shape(ATT_DIM, BF16),
                      state_shape, state_shape, heads_shape,
                      tok_shape(N_HEADS, F32), heads_shape,
                      tok_shape(SSM_DIM, F32), tok_shape(CONV_DIM, F32), tok_shape(CONV_DIM, F32))
        out_specs = (tok(ATT_DIM), tok(ATT_DIM), tok(ATT_DIM), state_spec, state_spec, head_major,
                     tok(N_HEADS), head_major, tok(SSM_DIM), tok(CONV_DIM), tok(CONV_DIM))
        state_out = (3, 4)
    in_specs = [tok(D_MODEL), _layer_spec((1, D_MODEL), layer), _layer_spec((D_MODEL, PROJ_PAD), layer),
                _layer_spec((1, ATT_DIM), layer), _layer_spec((1, ATT_DIM), layer), _layer_spec((1, LANES), layer),
                _const_spec((ATT_DIM, ATT_DIM)), _const_spec((cum_block, cum_block)),
                _const_spec((N_HEADS, LANES)),
                pl.BlockSpec((None, 1, LANES), lambda b, i: (b, 0, 0)),
                _const_spec(sq.shape), _const_spec(sk.shape)]
    args = [x, lw["ln1"], lw["w_in"], lw["qn"], lw["kn"], lw["bf"], lw["mbd"], tri, sel, c0, sq, sk]
    aliases = {}
    n_prev = 0
    if prev is not None:
        n_prev = len(prev)
        for n, buf in enumerate(prev):
            aliases[len(args)] = state_out[n]
            args.append(buf)
            in_specs.append(pl.BlockSpec(memory_space=pl.ANY))
    return pl.pallas_call(
        functools.partial(_inproj_body, cum_block=cum_block, prompt=prompt, n_prev=n_prev),
        grid=(bsz, seq // tb),
        in_specs=in_specs, out_specs=out_specs, out_shape=out_shapes,
        scratch_shapes=[pltpu.VMEM((1, LANES), F32)],
        input_output_aliases=aliases,
        compiler_params=_params("arbitrary", "arbitrary"),
        name="inproj_prompt" if prompt else "inproj_sample",
    )(*args)


def _attn_body(qx_ref, kx_ref, vt_ref, o_ref, m_s, mb_s, acc_s, s_s, *, tq, pairs):
    nq = o_ref.shape[0] // tq
    row_x = lax.broadcasted_iota(jnp.int32, (PAIR_EXT, tq), 0)
    row_v = lax.broadcasted_iota(jnp.int32, (LANES, tq), 0)
    head_rows = [(row_v >= HEAD_DIM * a) & (row_v < HEAD_DIM * (a + 1)) for a in range(2)]
    mine = [(((row_x >= HEAD_DIM * a) & (row_x < HEAD_DIM * (a + 1)))
             | ((row_x >= LANES + AUG_STRIDE * a) & (row_x < LANES + AUG_STRIDE * (a + 1)))) for a in range(2)]
    half = tq // 2
    causal_top = (lax.broadcasted_iota(jnp.int32, (half, tq), 0) <= lax.broadcasted_iota(jnp.int32, (half, tq), 1))
    causal_bot = (lax.broadcasted_iota(jnp.int32, (half, half), 0) <= lax.broadcasted_iota(jnp.int32, (half, half), 1))

    def col_max(s):
        return jnp.max(s, axis=0, keepdims=True)

    def scores(qi, j, p, diagonal=False, heads=(0, 1)):
        cols = pl.ds(pl.multiple_of(qi * tq, tq), tq)
        qt = qx_ref[p * PAIR_EXT:(p + 1) * PAIR_EXT, cols]
        rows = pl.ds(pl.multiple_of(j * tq, tq), tq)
        kx = kx_ref[rows, p * PAIR_EXT:(p + 1) * PAIR_EXT]
        for a in heads:
            h = 2 * p + a
            qa = jnp.where(mine[a], qt, jnp.zeros_like(qt))
            if diagonal:
                s_s[h, 0:half, :] = jnp.dot(kx[0:half], qa, preferred_element_type=F32)
                s_s[h, half:tq, half:tq] = jnp.dot(kx[half:tq], qa[:, half:tq], preferred_element_type=F32)
            else:
                s = jnp.dot(kx, qa, preferred_element_type=F32)
                s_s[h] = s
                mb_s[h] = col_max(s)

    def absorb(j, p, diagonal, heads=(0, 1)):
        cols = pl.ds(pl.multiple_of(j * tq, tq), tq)
        vt = vt_ref[p * LANES:(p + 1) * LANES, cols]
        for a in heads:
            h = 2 * p + a
            va = jnp.where(head_rows[a], vt, jnp.ones_like(vt))
            m_old = m_s[h]
            if diagonal:
                top = jnp.where(causal_top, s_s[h, 0:half, :], NEG_INF)
                bot = jnp.where(causal_bot, s_s[h, half:tq, half:tq], NEG_INF)
                m_top = col_max(top)
                m_blk = jnp.concatenate([m_top[:, 0:half], jnp.maximum(m_top[:, half:tq], col_max(bot))], axis=1)
                m_new = jnp.maximum(m_old, m_blk)
                out = jnp.dot(va[:, 0:half], jnp.exp2(top - m_new[0:1]).astype(BF16), preferred_element_type=F32)
                out_late = jnp.dot(va[:, half:tq], jnp.exp2(bot - m_new[0:1, half:tq]).astype(BF16),
                                   preferred_element_type=F32)
                out = jnp.concatenate([out[:, 0:half], out[:, half:tq] + out_late], axis=1)
            else:
                m_new = jnp.maximum(m_old, mb_s[h])
                out = jnp.dot(va, jnp.exp2(s_s[h] - m_new[0:1]).astype(BF16), preferred_element_type=F32)
            acc_s[h] = jnp.exp2(m_old - m_new)[0:1] * acc_s[h] + out
            m_s[h] = m_new

    def reset():
        for h in range(2 * pairs):
            m_s[h] = jnp.full((1, tq), NEG_INF, F32)
            acc_s[h] = jnp.zeros((LANES, tq), F32)

    def key_block(qi, j, diagonal, nxt):
        for p in range(pairs):
            for a in range(2):
                if p + 1 < pairs:
                    scores(qi, j, p + 1, diagonal, heads=(a,))
                else:
                    scores(nxt[0], nxt[1], 0, heads=(a,))
                absorb(j, p, diagonal, heads=(a,))

    def query_block(qi, carry):
        def body(j, c):
            key_block(qi, j, False, (qi, j + 1))
            return c

        lax.fori_loop(0, qi, body, 0)
        key_block(qi, qi, True, (jnp.minimum(qi + 1, nq - 1), 0))
        rows = pl.ds(pl.multiple_of(qi * tq, tq), tq)
        for p in range(pairs):
            a0, a1 = acc_s[2 * p], acc_s[2 * p + 1]
            out_t = jnp.concatenate([a0[0:HEAD_DIM] / a0[HEAD_DIM:LANES],
                                     a1[HEAD_DIM:LANES] / a1[0:HEAD_DIM]], axis=0)
            o_ref[rows, p * LANES:(p + 1) * LANES] = out_t.T
        reset()
        return carry

    reset()
    scores(0, 0, 0)
    lax.fori_loop(0, nq, query_block, 0)


def _attn_prompt(qx_t, kx, vt, tq, pairs):
    bsz, seq, _ = kx.shape
    return pl.pallas_call(
        functools.partial(_attn_body, tq=tq, pairs=pairs),
        grid=(bsz, HEAD_PAIRS // pairs),
        in_specs=[pl.BlockSpec((None, pairs * PAIR_EXT, seq), lambda b, h: (b, h, 0)),
                  pl.BlockSpec((None, seq, pairs * PAIR_EXT), lambda b, h: (b, 0, h)),
                  pl.BlockSpec((None, pairs * LANES, seq), lambda b, h: (b, h, 0))],
        out_specs=pl.BlockSpec((None, seq, pairs * LANES), lambda b, h: (b, 0, h)),
        out_shape=jax.ShapeDtypeStruct((bsz, seq, ATT_DIM), F32),
        scratch_shapes=[pltpu.VMEM((2 * pairs, 1, tq), F32), pltpu.VMEM((2 * pairs, 1, tq), F32),
                        pltpu.VMEM((2 * pairs, LANES, tq), F32),
                        pltpu.VMEM((2 * pairs, tq, tq), F32)],
        compiler_params=_params("arbitrary", "arbitrary"),
        name="attn_prompt",
    )(qx_t, kx, vt)


def _attn_sample_body(q_ref, ckt_ref, cvt_ref, kn_ref, vn_ref, cq_ref, ctc_ref, ctn_ref, o_ref, *, pairs):
    tq = q_ref.shape[0]
    past = ckt_ref.shape[-1]
    cq8 = cq_ref[...]
    lane = lax.broadcasted_iota(jnp.int32, (tq, LANES), 1)
    lane8 = lax.broadcasted_iota(jnp.int32, (tq, N_HEADS), 1)
    row = lax.broadcasted_iota(jnp.int32, (tq, tq), 0)
    col = lax.broadcasted_iota(jnp.int32, (tq, tq), 1)
    for pp in range(pairs):
        hp = pl.program_id(1) * pairs + pp
        lanes = slice(pp * LANES, (pp + 1) * LANES)
        q = q_ref[:, lanes]
        q2 = jnp.concatenate(
            [jnp.where((lane >= HEAD_DIM * a) & (lane < HEAD_DIM * (a + 1)), q, jnp.zeros_like(q)) for a in range(2)],
            axis=0)
        kt = ckt_ref[2 * pp:2 * pp + 2].reshape(LANES, past).astype(BF16)
        vt = cvt_ref[2 * pp:2 * pp + 2].reshape(LANES, past).astype(BF16)
        s_cache = jnp.dot(q2, kt, preferred_element_type=F32)
        s_new = _nt_dot(q2, kn_ref[:, lanes])
        sc, sn = [], []
        for a in range(2):
            head = 2 * hp + a
            cqa = jnp.sum(jnp.where(lane8 == head, cq8, 0.0), axis=-1, keepdims=True)
            ck_cache = ctc_ref[pl.ds(head, 1), :] * LOG2E
            ck_new = ctn_ref[pl.ds(head, 1), :]
            sc.append(s_cache[a * tq:(a + 1) * tq] + cqa - ck_cache)
            sn.append(jnp.where(col <= row, s_new[a * tq:(a + 1) * tq] + cqa - ck_new, NEG_INF))
        s_cache = jnp.concatenate(sc, axis=0)
        s_new = jnp.concatenate(sn, axis=0)
        m = jnp.maximum(jnp.max(s_cache, axis=-1, keepdims=True), jnp.max(s_new, axis=-1, keepdims=True))
        p_cache = jnp.exp2(s_cache - m)
        p_new = jnp.exp2(s_new - m)
        l = jnp.sum(p_cache, axis=-1, keepdims=True) + jnp.sum(p_new, axis=-1, keepdims=True)
        o = (_nt_dot(p_cache.astype(BF16), vt)
             + jnp.dot(p_new.astype(BF16), vn_ref[:, lanes], preferred_element_type=F32)) / l
        o_ref[:, lanes] = jnp.where(lane < HEAD_DIM, o[0:tq], o[tq:2 * tq])


def _attn_sample(q, kb, vb, cq, ct_new, cache_kt, cache_vt, ct_cache, layer, pairs=4):
    bsz, seq, _ = q.shape
    past = cache_kt.shape[-1]
    cache_blk = pl.BlockSpec((None, None, 2 * pairs, HEAD_DIM, past), lambda b, h: (layer, b, h, 0, 0))
    new_blk = pl.BlockSpec((None, seq, pairs * LANES), lambda b, h: (b, 0, h))
    return pl.pallas_call(
        functools.partial(_attn_sample_body, pairs=pairs),
        grid=(bsz, HEAD_PAIRS // pairs),
        in_specs=[new_blk, cache_blk, cache_blk, new_blk, new_blk,
                  pl.BlockSpec((None, seq, N_HEADS), lambda b, h: (b, 0, 0)),
                  pl.BlockSpec((None, N_HEADS, past), lambda b, h: (b, 0, 0)),
                  pl.BlockSpec((None, N_HEADS, seq), lambda b, h: (b, 0, 0))],
        out_specs=new_blk,
        out_shape=jax.ShapeDtypeStruct((bsz, seq, ATT_DIM), F32),
        compiler_params=_params("arbitrary", "arbitrary"),
        name="attn_sample",
    )(q, cache_kt, cache_vt, kb, vb, cq, ct_cache, ct_new)


def _cache_cumsum_body(x_ref, o_ref):
    rows, past = x_ref.shape
    tri = (lax.broadcasted_iota(jnp.int32, (LANES, LANES), 0)
           <= lax.broadcasted_iota(jnp.int32, (LANES, LANES), 1)).astype(F32)
    carry = jnp.zeros((rows, 1), F32)
    for b in range(past // LANES):
        blk = x_ref[:, b * LANES:(b + 1) * LANES]
        c = jnp.dot(blk, tri, precision=HIGHEST, preferred_element_type=F32) + carry
        o_ref[:, b * LANES:(b + 1) * LANES] = c
        carry = c[:, LANES - 1:LANES]


def _cache_cumsum(x):
    return pl.pallas_call(
        _cache_cumsum_body,
        out_shape=jax.ShapeDtypeStruct(x.shape, F32),
        compiler_params=pltpu.CompilerParams(vmem_limit_bytes=VMEM_LIMIT),
        name="cache_cumsum",
    )(x)


def _s5_prep_body(lr_ref, li_ref, ldt_ref, br_ref, bi_ref, ar_ref, ai_ref, bbr_ref, bbi_ref):
    lr = jnp.minimum(lr_ref[...], -1e-4)
    li = li_ref[...]
    dt = jnp.exp(ldt_ref[...])
    ldr, ldi = lr * dt, li * dt
    mag = jnp.exp(ldr)
    ar, ai = mag * jnp.cos(ldi), mag * jnp.sin(ldi)
    den = lr * lr + li * li
    nr = ar - 1.0
    qr = (nr * lr + ai * li) / den
    qi = (ai * lr - nr * li) / den
    ar_ref[...] = ar
    ai_ref[...] = ai
    br, bi = br_ref[...], bi_ref[...]
    bbr_ref[...] = qr[:, None, :] * br - qi[:, None, :] * bi
    bbi_ref[...] = qr[:, None, :] * bi + qi[:, None, :] * br


def _s5_prep(lam_re, lam_im, log_dt, b_re, b_im):
    g, p = lam_re.shape
    gp = jax.ShapeDtypeStruct((g, p), F32)
    gcp = jax.ShapeDtypeStruct((g, SSM_GROUP, p), F32)
    return pl.pallas_call(
        _s5_prep_body, out_shape=(gp, gp, gcp, gcp), name="s5_prep",
    )(lam_re, lam_im, log_dt.reshape(g, 1), jnp.swapaxes(b_re, 1, 2), jnp.swapaxes(b_im, 1, 2))


def _block_diag(t):
    d, g, c, p = t.shape
    eye = jnp.eye(g, dtype=t.dtype)
    return (t[:, :, :, None, :] * eye[None, :, None, :, None]).reshape(d, g * c, g * p)


def _ssm_body(u_ref, z_ref, gb_ref, h0r_ref, h0i_ref, cprev_ref,
              bre_ref, bim_ref, cre_ref, cim_ref, ar_ref, ai_ref, d_ref, wglu_ref, bglu_ref, cw_ref,
              ssm_ref, conv_ref, hr_out, hi_out, cst_out,
              sre, sim, hr_s, hi_s, zprev_s, *, lb):
    j = pl.program_id(1)
    nb = SUBLANES
    tiles = STATE_LANES // LANES

    @pl.when(j == 0)
    def _():
        hr_s[...] = h0r_ref[...]
        hi_s[...] = h0i_ref[...]
        zprev_s[...] = cprev_ref[...]

    u2 = jnp.swapaxes(u_ref[...], 0, 1).reshape(lb * nb, SSM_DIM)
    ub = u2.astype(BF16)
    bu_re = jnp.dot(ub, bre_ref[...], preferred_element_type=F32)
    bu_im = jnp.dot(ub, bim_ref[...], preferred_element_type=F32)
    for c in range(tiles):
        sre[c] = bu_re[:, c * LANES:(c + 1) * LANES]
        sim[c] = bu_im[:, c * LANES:(c + 1) * LANES]
    ar = [jnp.broadcast_to(ar_ref[:, c * LANES:(c + 1) * LANES], (nb, LANES)) for c in range(tiles)]
    ai = [jnp.broadcast_to(ai_ref[:, c * LANES:(c + 1) * LANES], (nb, LANES)) for c in range(tiles)]

    def step(t, carry):
        hr, hi = carry
        rows = pl.ds(pl.multiple_of(t * nb, nb), nb)
        nr, ni = [], []
        for c in range(tiles):
            r = ar[c] * hr[c] - ai[c] * hi[c] + sre[c, rows, :]
            i = ar[c] * hi[c] + ai[c] * hr[c] + sim[c, rows, :]
            sre[c, rows, :] = r
            sim[c, rows, :] = i
            nr.append(r)
            ni.append(i)
        return tuple(nr), tuple(ni)

    init = (tuple(hr_s[:, c * LANES:(c + 1) * LANES] for c in range(tiles)),
            tuple(hi_s[:, c * LANES:(c + 1) * LANES] for c in range(tiles)))
    hr, hi = lax.fori_loop(0, lb, step, init, unroll=2)
    for c in range(tiles):
        hr_s[:, c * LANES:(c + 1) * LANES] = hr[c]
        hi_s[:, c * LANES:(c + 1) * LANES] = hi[c]
        hr_out[:, c * LANES:(c + 1) * LANES] = hr[c]
        hi_out[:, c * LANES:(c + 1) * LANES] = hi[c]

    tt = lax.broadcasted_iota(jnp.int32, (lb, CONV_DIM), 0)
    w0, w1, w2 = cw_ref[0:1, :], cw_ref[1:2, :], cw_ref[2:3, :]
    for b in range(nb):
        zb = z_ref[b]
        prev = zprev_s[b]
        z1 = jnp.where(tt == 0, prev[1:2], pltpu.roll(zb, 1, 0))
        z2 = jnp.where(tt == 0, prev[0:1], jnp.where(tt == 1, prev[1:2], pltpu.roll(zb, 2, 0)))
        conv_ref[b] = gb_ref[b] * (w0 * z2 + w1 * z1 + w2 * zb)
        last = zb[lb - (CONV_WIDTH - 1):lb]
        zprev_s[b] = last
        cst_out[b] = last

    h_re = jnp.concatenate([sre[c] for c in range(tiles)], axis=-1).astype(BF16)
    h_im = jnp.concatenate([sim[c] for c in range(tiles)], axis=-1).astype(BF16)
    y = (jnp.dot(h_re, cre_ref[...], preferred_element_type=F32)
         - jnp.dot(h_im, cim_ref[...], preferred_element_type=F32))
    y = y + d_ref[...] * u2
    g = jax.nn.gelu(y)
    gate = jax.nn.sigmoid(jnp.dot(g.astype(BF16), wglu_ref[...], preferred_element_type=F32) + bglu_ref[...])
    ssm_ref[...] = jnp.swapaxes((g * gate).reshape(lb, nb, SSM_DIM), 0, 1)


def _ssm_conv(u, z, gb, h0r, h0i, cprev, lw, layer, lb):
    bsz, seq, _ = u.shape
    nb = SUBLANES
    tok = pl.BlockSpec((nb, lb, SSM_DIM), lambda g, j: (g, j, 0))
    st = pl.BlockSpec((nb, STATE_LANES), lambda g, j: (g, 0))
    cs = pl.BlockSpec((nb, CONV_WIDTH - 1, CONV_DIM), lambda g, j: (g, 0, 0))
    par = lambda *shape: _layer_spec(shape, layer)
    return pl.pallas_call(
        functools.partial(_ssm_body, lb=lb),
        grid=(bsz // nb, seq // lb),
        in_specs=[tok, tok, tok, st, st, cs,
                  par(SSM_DIM, STATE_LANES), par(SSM_DIM, STATE_LANES),
                  par(STATE_LANES, SSM_DIM), par(STATE_LANES, SSM_DIM),
                  par(1, STATE_LANES), par(1, STATE_LANES),
                  par(1, SSM_DIM), par(SSM_DIM, SSM_DIM), par(1, SSM_DIM),
                  par(CONV_WIDTH, CONV_DIM)],
        out_specs=(tok, tok, st, st, cs),
        out_shape=(jax.ShapeDtypeStruct((bsz, seq, SSM_DIM), F32),
                   jax.ShapeDtypeStruct((bsz, seq, CONV_DIM), F32),
                   jax.ShapeDtypeStruct((bsz, STATE_LANES), F32),
                   jax.ShapeDtypeStruct((bsz, STATE_LANES), F32),
                   jax.ShapeDtypeStruct((bsz, CONV_WIDTH - 1, CONV_DIM), F32)),
        scratch_shapes=[pltpu.VMEM((STATE_LANES // LANES, nb * lb, LANES), F32),
                        pltpu.VMEM((STATE_LANES // LANES, nb * lb, LANES), F32),
                        pltpu.VMEM((nb, STATE_LANES), F32), pltpu.VMEM((nb, STATE_LANES), F32),
                        pltpu.VMEM((nb, CONV_WIDTH - 1, CONV_DIM), F32)],
        compiler_params=_params("arbitrary", "arbitrary"),
        name="ssm_conv",
    )(u, z, gb, h0r, h0i, cprev, lw["b_re_blk"], lw["b_im_blk"], lw["c_re_blk"], lw["c_im_blk"],
      lw["a_re"], lw["a_im"], lw["d"], lw["w_glu"], lw["b_glu"], lw["conv_w"])


def _rms_rows(t, w):
    return t * lax.rsqrt(jnp.mean(t * t, axis=-1, keepdims=True) + RMS_EPS) * w


def _mix_ffn_body(x_ref, att_ref, ssm_ref, conv_ref, bn_ref, wout_ref, ln2_ref, wup_ref, wdown_ref, o_ref,
                  *, ff_chunk):
    bn = bn_ref[...]
    mix = jnp.concatenate(
        [_rms_rows(att_ref[...], bn[:, 0:ATT_DIM]),
         _rms_rows(ssm_ref[...], bn[:, ATT_DIM:ATT_DIM + SSM_DIM]),
         _rms_rows(conv_ref[...], bn[:, ATT_DIM + SSM_DIM:])], axis=-1).astype(BF16)
    x1 = x_ref[...] + jnp.dot(mix, wout_ref[...], preferred_element_type=F32)
    h2 = _rms_rows(x1, ln2_ref[...]).astype(BF16)
    acc = x1
    for c in range(D_FF // ff_chunk):
        f = jnp.maximum(jnp.dot(h2, wup_ref[:, c * ff_chunk:(c + 1) * ff_chunk], preferred_element_type=F32), 0.0)
        acc = acc + jnp.dot((f * f).astype(BF16), wdown_ref[c * ff_chunk:(c + 1) * ff_chunk, :],
                            preferred_element_type=F32)
    o_ref[...] = acc


def _mix_ffn(x, att, ssm, conv, lw, layer, tb):
    bsz, seq, _ = x.shape
    rows = bsz * seq
    tok = lambda w: pl.BlockSpec((tb, w), lambda i: (i, 0))
    once = lambda *shape: _layer_spec(shape, layer, pipeline_mode=pl.Buffered(1))
    out = pl.pallas_call(
        functools.partial(_mix_ffn_body, ff_chunk=1024),
        grid=(rows // tb,),
        in_specs=[tok(D_MODEL), tok(ATT_DIM), tok(SSM_DIM), tok(CONV_DIM),
                  once(1, D_MODEL), once(D_MODEL, D_MODEL), once(1, D_MODEL),
                  once(D_MODEL, D_FF), once(D_FF, D_MODEL)],
        out_specs=tok(D_MODEL),
        out_shape=jax.ShapeDtypeStruct((rows, D_MODEL), F32),
        compiler_params=_params("arbitrary"),
        name="mix_ffn",
    )(x.reshape(rows, D_MODEL), att.reshape(rows, ATT_DIM), ssm.reshape(rows, SSM_DIM),
      conv.reshape(rows, CONV_DIM), lw["bn"], lw["w_out"], lw["ln2"], lw["w_up"], lw["w_down"])
    return out.reshape(bsz, seq, D_MODEL)


def _prep_weights(ln1_w, w_in, b_forget, q_norm_w, k_norm_w, conv_w, ssm_lam_re, ssm_lam_im, ssm_log_dt,
                  ssm_b_re, ssm_b_im, ssm_c_re, ssm_c_im, ssm_d, w_glu, b_glu, branch_norm_w, w_out,
                  ln2_w, w_up, w_down):
    depth = w_in.shape[0]
    fg0 = 3 * ATT_DIM
    w_re = jnp.concatenate([w_in[:, :, :fg0], w_in[:, :, fg0 + N_HEADS:], w_in[:, :, fg0:fg0 + N_HEADS],
                            jnp.zeros((depth, D_MODEL, LANES - N_HEADS), F32)], axis=2).astype(BF16)
    head_id = jnp.arange(ATT_DIM) // HEAD_DIM
    groups = depth * SSM_GROUPS
    a_re, a_im, bb_re, bb_im = _s5_prep(
        ssm_lam_re.reshape(groups, SSM_STATE), ssm_lam_im.reshape(groups, SSM_STATE), ssm_log_dt.reshape(groups),
        ssm_b_re.reshape(groups, SSM_STATE, SSM_GROUP), ssm_b_im.reshape(groups, SSM_STATE, SSM_GROUP))
    per_layer = lambda t: t.reshape(depth, SSM_GROUPS, SSM_GROUP, SSM_STATE)
    row = lambda t, w: t.reshape(depth, 1, w)
    return dict(
        ln1=row(ln1_w, D_MODEL), w_in=w_re,
        qn=row(jnp.tile(q_norm_w, (1, N_HEADS)), ATT_DIM), kn=row(jnp.tile(k_norm_w, (1, N_HEADS)), ATT_DIM),
        bf=row(jnp.pad(b_forget, ((0, 0), (0, LANES - N_HEADS))), LANES),
        mbd=(head_id[:, None] == head_id[None, :]).astype(BF16),
        a_re=row(a_re, STATE_LANES), a_im=row(a_im, STATE_LANES),
        b_re_blk=_block_diag(per_layer(bb_re)).astype(BF16), b_im_blk=_block_diag(per_layer(bb_im)).astype(BF16),
        c_re_blk=jnp.swapaxes(_block_diag(ssm_c_re), 1, 2).astype(BF16),
        c_im_blk=jnp.swapaxes(_block_diag(ssm_c_im), 1, 2).astype(BF16),
        d=row(ssm_d, SSM_DIM), w_glu=w_glu.astype(BF16), b_glu=row(b_glu, SSM_DIM),
        conv_w=conv_w, bn=row(branch_norm_w, D_MODEL), w_out=w_out.astype(BF16),
        ln2=row(ln2_w, D_MODEL), w_up=w_up.astype(BF16), w_down=w_down.astype(BF16))


def _run_layer(x, lw, layer, depth, prev, past, tiles):
    bsz, seq, _ = x.shape
    if past is None:
        c0 = jnp.zeros((bsz, 1, LANES), F32)
        h0r = jnp.zeros((bsz, STATE_LANES), F32)
        h0i = h0r
        cprev = jnp.zeros((bsz, CONV_WIDTH - 1, CONV_DIM), F32)
        qx, kx, vb, k_all, v_all, lft, u, z, gb = _inproj(x, c0, lw, tiles["tok"], layer, depth, prev, True)
        att = _attn_prompt(qx, kx, vb, tiles["attn"], tiles["attn_pairs"])
    else:
        cache_kt, cache_vt, clf_t, s_re, s_im, s_conv = past
        plen = cache_kt.shape[-1]
        ct_cache = _cache_cumsum(clf_t.reshape(bsz * N_HEADS, plen)).reshape(bsz, N_HEADS, plen)
        c0 = jnp.pad(ct_cache[:, :, plen - 1], ((0, 0), (0, LANES - N_HEADS))).reshape(bsz, 1, LANES)
        h0r = s_re.reshape(bsz, STATE_LANES)
        h0i = s_im.reshape(bsz, STATE_LANES)
        cprev = s_conv
        q, kb, vb, k_all, v_all, lft, cq, ct, u, z, gb = _inproj(x, c0, lw, tiles["tok"], layer, depth, prev, False)
        att = _attn_sample(q, kb, vb, cq, ct, cache_kt, cache_vt, ct_cache, layer)
    ssm, conv, hr, hi, cst = _ssm_conv(u, z, gb, h0r, h0i, cprev, lw, layer, tiles["scan"])
    y = _mix_ffn(x, att, ssm, conv, lw, layer, tiles["ffn"])
    small = (lft, hr.reshape(bsz, SSM_GROUPS, SSM_STATE), hi.reshape(bsz, SSM_GROUPS, SSM_STATE), cst)
    return y, (k_all, v_all), small


def _tiles(bsz, seq):
    tok = min(seq, 1024)
    return dict(tok=tok, attn=min(seq, 512), attn_pairs=2, scan=min(seq, 128), ffn=min(bsz * seq, 1024))


def _forward(x_prompt, x_sample, cache_k, cache_v, cache_logf, state_ssm_re, state_ssm_im, state_conv, *weights):
    depth = cache_k.shape[0]
    xp, xs = x_prompt, x_sample
    bp, lp = xp.shape[0], xp.shape[1]
    tiles_p = _tiles(bp, lp)
    tiles_s = _tiles(xs.shape[0], xs.shape[1])
    cache_kt = jnp.transpose(cache_k, (0, 1, 3, 4, 2))
    cache_vt = jnp.transpose(cache_v, (0, 1, 3, 4, 2))
    cache_lft = jnp.swapaxes(cache_logf, 2, 3)
    kv_p, kv_s, small_p, small_s = None, None, [], []
    lw = _prep_weights(*weights)
    for l in range(depth):
        xp, kv_p, sp = _run_layer(xp, lw, l, depth, kv_p, None, tiles_p)
        xs, kv_s, ss = _run_layer(xs, lw, l, depth, kv_s,
                                  (cache_kt, cache_vt, cache_lft[l], state_ssm_re[l], state_ssm_im[l], state_conv[l]),
                                  tiles_s)
        small_p.append(sp)
        small_s.append(ss)
    stk = lambda lst, i: jnp.stack([s[i] for s in lst])
    from_t = lambda t: jnp.transpose(t.reshape(depth, bp, N_HEADS, HEAD_DIM, lp), (0, 1, 4, 2, 3))
    return (xp, xs,
            from_t(kv_p[0]), from_t(kv_p[1]), jnp.swapaxes(stk(small_p, 0), 2, 3),
            stk(small_p, 1), stk(small_p, 2), stk(small_p, 3),
            kv_s[0], kv_s[1], jnp.swapaxes(stk(small_s, 0), 2, 3),
            stk(small_s, 1), stk(small_s, 2), stk(small_s, 3))


def kernel(x_prompt, x_sample, cache_k, cache_v, cache_logf, state_ssm_re, state_ssm_im, state_conv, ln1_w, w_in, b_forget, q_norm_w, k_norm_w, conv_w, ssm_lam_re, ssm_lam_im, ssm_log_dt, ssm_b_re, ssm_b_im, ssm_c_re, ssm_c_im, ssm_d, w_glu, b_glu, branch_norm_w, w_out, ln2_w, w_up, w_down):
    return _forward(x_prompt, x_sample, cache_k, cache_v, cache_logf, state_ssm_re, state_ssm_im, state_conv,
                    ln1_w, w_in, b_forget, q_norm_w, k_norm_w, conv_w, ssm_lam_re, ssm_lam_im, ssm_log_dt,
                    ssm_b_re, ssm_b_im, ssm_c_re, ssm_c_im, ssm_d, w_glu, b_glu, branch_norm_w, w_out,
                    ln2_w, w_up, w_down)
```

```python
import functools
import math

import jax
import jax.numpy as jnp
from jax import lax
from jax.experimental import pallas as pl
from jax.experimental.pallas import tpu as pltpu

F32 = jnp.float32
BF16 = jnp.bfloat16
HIGHEST = lax.Precision.HIGHEST

D_MODEL = 1024
N_HEADS = 8
HEAD_DIM = 64
ATT_DIM = N_HEADS * HEAD_DIM
SSM_DIM = 256
SSM_GROUP = 16
SSM_GROUPS = SSM_DIM // SSM_GROUP
SSM_STATE = 64
STATE_LANES = SSM_GROUPS * SSM_STATE
CONV_DIM = 256
CONV_WIDTH = 3
D_FF = 4 * D_MODEL
RMS_EPS = 1e-6
NEG_INF = -1e30
LOG2E = math.log2(math.e)

LANES = 128
SUBLANES = 8
HEAD_PAIRS = ATT_DIM // LANES
PAIR_EXT = 2 * LANES
AUG_STRIDE = 8
BIAS_PIECES = 3
PROJ_PAD = 3 * ATT_DIM + SSM_DIM + 3 * CONV_DIM + LANES
VMEM_LIMIT = 56 * 1024 * 1024


def _params(*sem):
    return pltpu.CompilerParams(dimension_semantics=sem, vmem_limit_bytes=VMEM_LIMIT)


def _const_spec(shape):
    zeros = (0,) * len(shape)
    return pl.BlockSpec(shape, lambda *_: zeros)


def _layer_spec(shape, layer, **kw):
    index = (layer,) + (0,) * len(shape)
    return pl.BlockSpec((None,) + tuple(shape), lambda *_: index, **kw)


def _nt_dot(a, b, **kw):
    return lax.dot_general(a, b, (((1,), (1,)), ((), ())), preferred_element_type=F32, **kw)


def _bias_selectors():
    h = jnp.arange(N_HEADS)
    base = LANES * (h // 2) + AUG_STRIDE * (h % 2)
    sq = jnp.zeros((ATT_DIM, LANES), F32)
    sk = jnp.zeros((LANES, ATT_DIM), F32)
    for r in range(BIAS_PIECES):
        sq = sq.at[base + r, N_HEADS * r + h].set(1.0)
        sk = sk.at[N_HEADS * r + h, base + BIAS_PIECES + r].set(-1.0)
    return sq.astype(BF16), sk.astype(BF16)


def _split3(t):
    p1 = t.astype(BF16)
    r1 = t - p1.astype(F32)
    p2 = r1.astype(BF16)
    p3 = (r1 - p2.astype(F32)).astype(BF16)
    return p1, p2, p3


def _inproj_body(*refs, cum_block, prompt, n_prev):
    (x_ref, ln1_ref, w_ref, qnw_ref, knw_ref, bf_ref, mbd_ref, tri_ref, sel_ref, c0_ref, sq_ref, sk_ref) = refs[:12]
    outs = refs[12 + n_prev:]
    carry_ref = outs[-1]
    i = pl.program_id(1)
    x = x_ref[...]
    ms = jnp.mean(x * x, axis=-1, keepdims=True)
    h = (x * lax.rsqrt(ms + RMS_EPS) * ln1_ref[...]).astype(BF16)
    proj = jnp.dot(h, w_ref[...], preferred_element_type=F32)
    tb = x.shape[0]

    def head_norm(t, w):
        ss = jnp.dot((t * t).astype(BF16), mbd_ref[...], preferred_element_type=F32)
        return t * lax.rsqrt(ss * (1.0 / HEAD_DIM) + RMS_EPS) * w

    v = proj[:, 2 * ATT_DIM:3 * ATT_DIM]
    o = 3 * ATT_DIM
    u_ref, z_ref, gb_ref = outs[-4:-1]
    u_ref[...] = proj[:, o:o + SSM_DIM]
    gb_ref[...] = proj[:, o + SSM_DIM + CONV_DIM:o + SSM_DIM + 2 * CONV_DIM]
    z_ref[...] = (proj[:, o + SSM_DIM + 2 * CONV_DIM:o + SSM_DIM + 3 * CONV_DIM]
                  * proj[:, o + SSM_DIM:o + SSM_DIM + CONV_DIM])
    if prompt:
        v_t = v.T
        outs[2][...] = v_t.astype(BF16)
        outs[4][...] = v_t
    k = head_norm(proj[:, ATT_DIM:2 * ATT_DIM], knw_ref[...])
    if prompt:
        outs[3][...] = k.T
    q = head_norm(proj[:, 0:ATT_DIM], qnw_ref[...]) * (LOG2E * HEAD_DIM ** -0.5)
    lf = jax.nn.log_sigmoid(proj[:, PROJ_PAD - LANES:PROJ_PAD] + bf_ref[...])

    @pl.when(i == 0)
    def _():
        carry_ref[...] = c0_ref[...]

    carry = carry_ref[...]
    pieces = []
    for s in range(tb // cum_block):
        blk = jnp.concatenate(_split3(lf[s * cum_block:(s + 1) * cum_block]), axis=-1)
        r = jnp.dot(tri_ref[...], blk, preferred_element_type=F32)
        c = r[:, 0:LANES] + r[:, LANES:2 * LANES] + r[:, 2 * LANES:3 * LANES] + carry
        carry = c[cum_block - 1:cum_block]
        pieces.append(c)
    c_all = pieces[0] if len(pieces) == 1 else jnp.concatenate(pieces, axis=0)
    carry_ref[...] = carry
    c2 = c_all * LOG2E
    lf_t = _nt_dot(sel_ref[...], lf, precision=HIGHEST)

    if prompt:
        qx_ref, kx_ref, vb_ref, kt_ref, vt_ref, lft_ref, u_ref, z_ref, gb_ref = outs[:-1]
        head_lane = lax.broadcasted_iota(jnp.int32, (tb, LANES), 1) < N_HEADS
        p1, p2, p3 = (p.astype(F32) for p in _split3(jnp.where(head_lane, c2, 0.0)))
        pcs = (p1 + pltpu.roll(p2, N_HEADS, 1) + pltpu.roll(p3, 2 * N_HEADS, 1)).astype(BF16)
        row = lax.broadcasted_iota(jnp.int32, (ATT_DIM, tb), 0) % LANES
        q_one = (row < 2 * AUG_STRIDE) & (row % AUG_STRIDE >= BIAS_PIECES) & (row % AUG_STRIDE < 2 * BIAS_PIECES)
        qa_t = _nt_dot(sq_ref[...], pcs) + q_one.astype(F32)
        lane = lax.broadcasted_iota(jnp.int32, (tb, ATT_DIM), 1) % LANES
        k_one = (lane < 2 * AUG_STRIDE) & (lane % AUG_STRIDE < BIAS_PIECES)
        ka = jnp.dot(pcs, sk_ref[...], preferred_element_type=F32) + k_one.astype(F32)
        qtb, qatb, kb, kab = q.T.astype(BF16), qa_t.astype(BF16), k.astype(BF16), ka.astype(BF16)
        qx, kx = [], []
        for p in range(HEAD_PAIRS):
            sl = slice(p * LANES, (p + 1) * LANES)
            qx += [qtb[sl], qatb[sl]]
            kx += [kb[:, sl], kab[:, sl]]
        qx_ref[...] = jnp.concatenate(qx, axis=0)
        kx_ref[...] = jnp.concatenate(kx, axis=-1)
    else:
        q_ref, kb_ref, vb_ref, kf_ref, vf_ref, lft_ref, cq_ref, ct_ref, u_ref, z_ref, gb_ref = outs[:-1]
        q_ref[...] = q.astype(BF16)
        kb_ref[...] = k.astype(BF16)
        vb_ref[...] = v.astype(BF16)
        kf_ref[...] = k.reshape(tb, N_HEADS, HEAD_DIM)
        vf_ref[...] = v.reshape(tb, N_HEADS, HEAD_DIM)
        cq_ref[...] = c2[:, 0:N_HEADS]
        ct_ref[...] = _nt_dot(sel_ref[...], c2, precision=HIGHEST)
    lft_ref[...] = lf_t


def _inproj(x, c0, lw, tb, layer, depth, prev, prompt):
    bsz, seq, _ = x.shape
    cum_block = min(tb, LANES)
    tri = (lax.broadcasted_iota(jnp.int32, (cum_block, cum_block), 1)
           <= lax.broadcasted_iota(jnp.int32, (cum_block, cum_block), 0)).astype(BF16)
    sel =(lax.broadcasted_iota(jnp.int32, (N_HEADS, LANES), 0)
           == lax.broadcasted_iota(jnp.int32, (N_HEADS, LANES), 1)).astype(F32)
    sq, sk = _bias_selectors()
    tok = lambda w: pl.BlockSpec((None, tb, w), lambda b, i: (b, i, 0))
    head_major = pl.BlockSpec((None, N_HEADS, tb), lambda b, i: (b, 0, i))
    tok_shape = lambda w, dt: jax.ShapeDtypeStruct((bsz, seq, w), dt)
    heads_shape = jax.ShapeDtypeStruct((bsz, N_HEADS, seq), F32)
    if prompt:
        state_shape = jax.ShapeDtypeStruct((depth, bsz, ATT_DIM, seq), F32)
        state_spec = pl.BlockSpec((None, None, ATT_DIM, tb), lambda b, i: (layer, b, 0, i))
        rows_major = lambda r: pl.BlockSpec((None, r, tb), lambda b, i: (b, 0, i))
        out_shapes = (jax.ShapeDtypeStruct((bsz, HEAD_PAIRS * PAIR_EXT, seq), BF16),
                      tok_shape(HEAD_PAIRS * PAIR_EXT, BF16),
                      jax.ShapeDtypeStruct((bsz, ATT_DIM, seq), BF16),
                      state_shape, state_shape, heads_shape,
                      tok_shape(SSM_DIM, F32), tok_shape(CONV_DIM, F32), tok_shape(CONV_DIM, F32))
        out_specs = (rows_major(HEAD_PAIRS * PAIR_EXT), tok(HEAD_PAIRS * PAIR_EXT), rows_major(ATT_DIM),
                     state_spec, state_spec, head_major,
                     tok(SSM_DIM), tok(CONV_DIM), tok(CONV_DIM))
        state_out = (3, 4)
    else:
        state_shape = jax.ShapeDtypeStruct((depth, bsz, seq, N_HEADS, HEAD_DIM), F32)
        state_spec = pl.BlockSpec((None, None, tb, N_HEADS, HEAD_DIM), lambda b, i: (layer, b, i, 0, 0))
        out_shapes = (tok_shape(ATT_DIM, BF16), tok_shape(ATT_DIM, BF16), tok_shape(ATT_DIM, BF16),
                      state_shape, state_shape, heads_shape,
                      tok_shape(N_HEADS, F32), heads_shape,
                      tok_shape(SSM_DIM, F32), tok_shape(CONV_DIM, F32), tok_shape(CONV_DIM, F32))
        out_specs = (tok(ATT_DIM), tok(ATT_DIM), tok(ATT_DIM), state_spec, state_spec, head_major,
                     tok(N_HEADS), head_major, tok(SSM_DIM), tok(CONV_DIM), tok(CONV_DIM))
        state_out = (3, 4)
    in_specs = [tok(D_MODEL), _layer_spec((1, D_MODEL), layer), _layer_spec((D_MODEL, PROJ_PAD), layer),
                _layer_spec((1, ATT_DIM), layer), _layer_spec((1, ATT_DIM), layer), _layer_spec((1, LANES), layer),
                _const_spec((ATT_DIM, ATT_DIM)), _const_spec((cum_block, cum_block)),
                _const_spec((N_HEADS, LANES)),
                pl.BlockSpec((None, 1, LANES), lambda b, i: (b, 0, 0)),
                _const_spec(sq.shape), _const_spec(sk.shape)]
    args = [x, lw["ln1"], lw["w_in"], lw["qn"], lw["kn"], lw["bf"], lw["mbd"], tri, sel, c0, sq, sk]
    aliases = {}
    n_prev = 0
    if prev is not None:
        n_prev = len(prev)
        for n, buf in enumerate(prev):
            aliases[len(args)] = state_out[n]
            args.append(buf)
            in_specs.append(pl.BlockSpec(memory_space=pl.ANY))
    return pl.pallas_call(
        functools.partial(_inproj_body, cum_block=cum_block, prompt=prompt, n_prev=n_prev),
        grid=(bsz, seq // tb),
        in_specs=in_specs, out_specs=out_specs, out_shape=out_shapes,
        scratch_shapes=[pltpu.VMEM((1, LANES), F32)],
        input_output_aliases=aliases,
        compiler_params=_params("arbitrary", "arbitrary"),
        name="inproj_prompt" if prompt else "inproj_sample",
    )(*args)


def _attn_body(qx_ref, kx_ref, vt_ref, o_ref, m_s, mb_s, acc_s, s_s, *, tq, pairs):
    nq = o_ref.shape[0] // tq
    row_x = lax.broadcasted_iota(jnp.int32, (PAIR_EXT, tq), 0)
    row_v = lax.broadcasted_iota(jnp.int32, (LANES, tq), 0)
    head_rows = [(row_v >= HEAD_DIM * a) & (row_v < HEAD_DIM * (a + 1)) for a in range(2)]
    mine = [(((row_x >= HEAD_DIM * a) & (row_x < HEAD_DIM * (a + 1)))
             | ((row_x >= LANES + AUG_STRIDE * a) & (row_x < LANES + AUG_STRIDE * (a + 1)))) for a in range(2)]
    half = tq // 2
    causal_top = (lax.broadcasted_iota(jnp.int32, (half, tq), 0) <= lax.broadcasted_iota(jnp.int32, (half, tq), 1))
    causal_bot = (lax.broadcasted_iota(jnp.int32, (half, half), 0) <= lax.broadcasted_iota(jnp.int32, (half, half), 1))

    def col_max(s):
        return jnp.max(s, axis=0, keepdims=True)

    def scores(qi, j, p, diagonal=False, heads=(0, 1)):
        cols = pl.ds(pl.multiple_of(qi * tq, tq), tq)
        qt = qx_ref[p * PAIR_EXT:(p + 1) * PAIR_EXT, cols]
        rows = pl.ds(pl.multiple_of(j * tq, tq), tq)
        kx = kx_ref[rows, p * PAIR_EXT:(p + 1) * PAIR_EXT]
        for a in heads:
            h = 2 * p + a
            qa = jnp.where(mine[a], qt, jnp.zeros_like(qt))
            if diagonal:
                s_s[h, 0:half, :] = jnp.dot(kx[0:half], qa, preferred_element_type=F32)
                s_s[h, half:tq, half:tq] = jnp.dot(kx[half:tq], qa[:, half:tq], preferred_element_type=F32)
            else:
                s = jnp.dot(kx, qa, preferred_element_type=F32)
                s_s[h] = s
                mb_s[h] = col_max(s)

    def absorb(j, p, diagonal, heads=(0, 1)):
        cols = pl.ds(pl.multiple_of(j * tq, tq), tq)
        vt = vt_ref[p * LANES:(p + 1) * LANES, cols]
        for a in heads:
            h = 2 * p + a
            va = jnp.where(head_rows[a], vt, jnp.ones_like(vt))
            m_old = m_s[h]
            if diagonal:
                top = jnp.where(causal_top, s_s[h, 0:half, :], NEG_INF)
                bot = jnp.where(causal_bot, s_s[h, half:tq, half:tq], NEG_INF)
                m_top = col_max(top)
                m_blk = jnp.concatenate([m_top[:, 0:half], jnp.maximum(m_top[:, half:tq], col_max(bot))], axis=1)
                m_new = jnp.maximum(m_old, m_blk)
                out = jnp.dot(va[:, 0:half], jnp.exp2(top - m_new[0:1]).astype(BF16), preferred_element_type=F32)
                out_late = jnp.dot(va[:, half:tq], jnp.exp2(bot - m_new[0:1, half:tq]).astype(BF16),
                                   preferred_element_type=F32)
                out = jnp.concatenate([out[:, 0:half], out[:, half:tq] + out_late], axis=1)
            else:
                m_new = jnp.maximum(m_old, mb_s[h])
                out = jnp.dot(va, jnp.exp2(s_s[h] - m_new[0:1]).astype(BF16), preferred_element_type=F32)
            acc_s[h] = jnp.exp2(m_old - m_new)[0:1] * acc_s[h] + out
            m_s[h] = m_new

    def reset():
        for h in range(2 * pairs):
            m_s[h] = jnp.full((1, tq), NEG_INF, F32)
            acc_s[h] = jnp.zeros((LANES, tq), F32)

    def key_block(qi, j, diagonal, nxt):
        for p in range(pairs):
            for a in range(2):
                if p + 1 < pairs:
                    scores(qi, j, p + 1, diagonal, heads=(a,))
                else:
                    scores(nxt[0], nxt[1], 0, heads=(a,))
                absorb(j, p, diagonal, heads=(a,))

    def query_block(qi, carry):
        def body(j, c):
            key_block(qi, j, False, (qi, j + 1))
            return c

        lax.fori_loop(0, qi, body, 0)
        key_block(qi, qi, True, (jnp.minimum(qi + 1, nq - 1), 0))
        rows = pl.ds(pl.multiple_of(qi * tq, tq), tq)
        for p in range(pairs):
            a0, a1 = acc_s[2 * p], acc_s[2 * p + 1]
            out_t = jnp.concatenate([a0[0:HEAD_DIM] / a0[HEAD_DIM:LANES],
                                     a1[HEAD_DIM:LANES] / a1[0:HEAD_DIM]], axis=0)
            o_ref[rows, p * LANES:(p + 1) * LANES] = out_t.T
        reset()
        return carry

    reset()
    scores(0, 0, 0)
    lax.fori_loop(0, nq, query_block, 0)


def _attn_prompt(qx_t, kx, vt, tq, pairs):
    bsz, seq, _ = kx.shape
    return pl.pallas_call(
        functools.partial(_attn_body, tq=tq, pairs=pairs),
        grid=(bsz, HEAD_PAIRS // pairs),
        in_specs=[pl.BlockSpec((None, pairs * PAIR_EXT, seq), lambda b, h: (b, h, 0)),
                  pl.BlockSpec((None, seq, pairs * PAIR_EXT), lambda b, h: (b, 0, h)),
                  pl.BlockSpec((None, pairs * LANES, seq), lambda b, h: (b, h, 0))],
        out_specs=pl.BlockSpec((None, seq, pairs * LANES), lambda b, h: (b, 0, h)),
        out_shape=jax.ShapeDtypeStruct((bsz, seq, ATT_DIM), F32),
        scratch_shapes=[pltpu.VMEM((2 * pairs, 1, tq), F32), pltpu.VMEM((2 * pairs, 1, tq), F32),
                        pltpu.VMEM((2 * pairs, LANES, tq), F32),
                        pltpu.VMEM((2 * pairs, tq, tq), F32)],
        compiler_params=_params("arbitrary", "arbitrary"),
        name="attn_prompt",
    )(qx_t, kx, vt)


def _attn_sample_body(q_ref, ckt_ref, cvt_ref, kn_ref, vn_ref, cq_ref, ctc_ref, ctn_ref, o_ref, *, pairs):
    tq = q_ref.shape[0]
    past = ckt_ref.shape[-1]
    cq8 = cq_ref[...]
    lane = lax.broadcasted_iota(jnp.int32, (tq, LANES), 1)
    lane8 = lax.broadcasted_iota(jnp.int32, (tq, N_HEADS), 1)
    row = lax.broadcasted_iota(jnp.int32, (tq, tq), 0)
    col = lax.broadcasted_iota(jnp.int32, (tq, tq), 1)
    for pp in range(pairs):
        hp = pl.program_id(1) * pairs + pp
        lanes = slice(pp * LANES, (pp + 1) * LANES)
        q = q_ref[:, lanes]
        q2 = jnp.concatenate(
            [jnp.where((lane >= HEAD_DIM * a) & (lane < HEAD_DIM * (a + 1)), q, jnp.zeros_like(q)) for a in range(2)],
            axis=0)
        kt = ckt_ref[2 * pp:2 * pp + 2].reshape(LANES, past).astype(BF16)
        vt = cvt_ref[2 * pp:2 * pp + 2].reshape(LANES, past).astype(BF16)
        s_cache = jnp.dot(q2, kt, preferred_element_type=F32)
        s_new = _nt_dot(q2, kn_ref[:, lanes])
        sc, sn = [], []
        for a in range(2):
            head = 2 * hp + a
            cqa = jnp.sum(jnp.where(lane8 == head, cq8, 0.0), axis=-1, keepdims=True)
            ck_cache = ctc_ref[pl.ds(head, 1), :] * LOG2E
            ck_new = ctn_ref[pl.ds(head, 1), :]
            sc.append(s_cache[a * tq:(a + 1) * tq] + cqa - ck_cache)
            sn.append(jnp.where(col <= row, s_new[a * tq:(a + 1) * tq] + cqa - ck_new, NEG_INF))
        s_cache = jnp.concatenate(sc, axis=0)
        s_new = jnp.concatenate(sn, axis=0)
        m = jnp.maximum(jnp.max(s_cache, axis=-1, keepdims=True), jnp.max(s_new, axis=-1, keepdims=True))
        p_cache = jnp.exp2(s_cache - m)
        p_new = jnp.exp2(s_new - m)
        l = jnp.sum(p_cache, axis=-1, keepdims=True) + jnp.sum(p_new, axis=-1, keepdims=True)
        o = (_nt_dot(p_cache.astype(BF16), vt)
             + jnp.dot(p_new.astype(BF16), vn_ref[:, lanes], preferred_element_type=F32)) / l
        o_ref[:, lanes] = jnp.where(lane < HEAD_DIM, o[0:tq], o[tq:2 * tq])


def _attn_sample(q, kb, vb, cq, ct_new, cache_kt, cache_vt, ct_cache, layer, pairs=4):
    bsz, seq, _ = q.shape
    past = cache_kt.shape[-1]
    cache_blk = pl.BlockSpec((None, None, 2 * pairs, HEAD_DIM, past), lambda b, h: (layer, b, h, 0, 0))
    new_blk = pl.BlockSpec((None, seq, pairs * LANES), lambda b, h: (b, 0, h))
    return pl.pallas_call(
        functools.partial(_attn_sample_body, pairs=pairs),
        grid=(bsz, HEAD_PAIRS // pairs),
        in_specs=[new_blk, cache_blk, cache_blk, new_blk, new_blk,
                  pl.BlockSpec((None, seq, N_HEADS), lambda b, h: (b, 0, 0)),
                  pl.BlockSpec((None, N_HEADS, past), lambda b, h: (b, 0, 0)),
                  pl.BlockSpec((None, N_HEADS, seq), lambda b, h: (b, 0, 0))],
        out_specs=new_blk,
        out_shape=jax.ShapeDtypeStruct((bsz, seq, ATT_DIM), F32),
        compiler_params=_params("arbitrary", "arbitrary"),
        name="attn_sample",
    )(q, cache_kt, cache_vt, kb, vb, cq, ct_cache, ct_new)


def _cache_cumsum_body(x_ref, o_ref):
    rows, past = x_ref.shape
    tri = (lax.broadcasted_iota(jnp.int32, (LANES, LANES), 0)
           <= lax.broadcasted_iota(jnp.int32, (LANES, LANES), 1)).astype(F32)
    carry = jnp.zeros((rows, 1), F32)
    for b in range(past // LANES):
        blk = x_ref[:, b * LANES:(b + 1) * LANES]
        c = jnp.dot(blk, tri, precision=HIGHEST, preferred_element_type=F32) + carry
        o_ref[:, b * LANES:(b + 1) * LANES] = c
        carry = c[:, LANES - 1:LANES]


def _cache_cumsum(x):
    return pl.pallas_call(
        _cache_cumsum_body,
        out_shape=jax.ShapeDtypeStruct(x.shape, F32),
        compiler_params=pltpu.CompilerParams(vmem_limit_bytes=VMEM_LIMIT),
        name="cache_cumsum",
    )(x)


def _s5_prep_body(lr_ref, li_ref, ldt_ref, br_ref, bi_ref, ar_ref, ai_ref, bbr_ref, bbi_ref):
    lr = jnp.minimum(lr_ref[...], -1e-4)
    li = li_ref[...]
    dt = jnp.exp(ldt_ref[...])
    ldr, ldi = lr * dt, li * dt
    mag = jnp.exp(ldr)
    ar, ai = mag * jnp.cos(ldi), mag * jnp.sin(ldi)
    den = lr * lr + li * li
    nr = ar - 1.0
    qr = (nr * lr + ai * li) / den
    qi = (ai * lr - nr * li) / den
    ar_ref[...] = ar
    ai_ref[...] = ai
    br, bi = br_ref[...], bi_ref[...]
    bbr_ref[...] = qr[:, None, :] * br - qi[:, None, :] * bi
    bbi_ref[...] = qr[:, None, :] * bi + qi[:, None, :] * br


def _s5_prep(lam_re, lam_im, log_dt, b_re, b_im):
    g, p = lam_re.shape
    gp = jax.ShapeDtypeStruct((g, p), F32)
    gcp = jax.ShapeDtypeStruct((g, SSM_GROUP, p), F32)
    return pl.pallas_call(
        _s5_prep_body, out_shape=(gp, gp, gcp, gcp), name="s5_prep",
    )(lam_re, lam_im, log_dt.reshape(g, 1), jnp.swapaxes(b_re, 1, 2), jnp.swapaxes(b_im, 1, 2))


def _block_diag(t):
    d, g, c, p = t.shape
    eye = jnp.eye(g, dtype=t.dtype)
    return (t[:, :, :, None, :] * eye[None, :, None, :, None]).reshape(d, g * c, g * p)


def _ssm_body(u_ref, z_ref, gb_ref, h0r_ref, h0i_ref, cprev_ref,
              bre_ref, bim_ref, cre_ref, cim_ref, ar_ref, ai_ref, d_ref, wglu_ref, bglu_ref, cw_ref,
              ssm_ref, conv_ref, hr_out, hi_out, cst_out,
              sre, sim, hr_s, hi_s, zprev_s, *, lb):
    j = pl.program_id(1)
    nb = SUBLANES
    tiles = STATE_LANES // LANES

    @pl.when(j == 0)
    def _():
        hr_s[...] = h0r_ref[...]
        hi_s[...] = h0i_ref[...]
        zprev_s[...] = cprev_ref[...]

    u2 = jnp.swapaxes(u_ref[...], 0, 1).reshape(lb * nb, SSM_DIM)
    ub = u2.astype(BF16)
    bu_re = jnp.dot(ub, bre_ref[...], preferred_element_type=F32)
    bu_im = jnp.dot(ub, bim_ref[...], preferred_element_type=F32)
    for c in range(tiles):
        sre[c] = bu_re[:, c * LANES:(c + 1) * LANES]
        sim[c] = bu_im[:, c * LANES:(c + 1) * LANES]
    ar = [jnp.broadcast_to(ar_ref[:, c * LANES:(c + 1) * LANES], (nb, LANES)) for c in range(tiles)]
    ai = [jnp.broadcast_to(ai_ref[:, c * LANES:(c + 1) * LANES], (nb, LANES)) for c in range(tiles)]

    def step(t, carry):
        hr, hi = carry
        rows = pl.ds(pl.multiple_of(t * nb, nb), nb)
        nr, ni = [], []
        for c in range(tiles):
            r = ar[c] * hr[c] - ai[c] * hi[c] + sre[c, rows, :]
            i = ar[c] * hi[c] + ai[c] * hr[c] + sim[c, rows, :]
            sre[c, rows, :] = r
            sim[c, rows, :] = i
            nr.append(r)
            ni.append(i)
        return tuple(nr), tuple(ni)

    init = (tuple(hr_s[:, c * LANES:(c + 1) * LANES] for c in range(tiles)),
            tuple(hi_s[:, c * LANES:(c + 1) * LANES] for c in range(tiles)))
    hr, hi = lax.fori_loop(0, lb, step, init, unroll=2)
    for c in range(tiles):
        hr_s[:, c * LANES:(c + 1) * LANES] = hr[c]
        hi_s[:, c * LANES:(c + 1) * LANES] = hi[c]
        hr_out[:, c * LANES:(c + 1) * LANES] = hr[c]
        hi_out[:, c * LANES:(c + 1) * LANES] = hi[c]

    tt = lax.broadcasted_iota(jnp.int32, (lb, CONV_DIM), 0)
    w0, w1, w2 = cw_ref[0:1, :], cw_ref[1:2, :], cw_ref[2:3, :]
    for b in range(nb):
        zb = z_ref[b]
        prev = zprev_s[b]
        z1 = jnp.where(tt == 0, prev[1:2], pltpu.roll(zb, 1, 0))
        z2 = jnp.where(tt == 0, prev[0:1], jnp.where(tt == 1, prev[1:2], pltpu.roll(zb, 2, 0)))
        conv_ref[b] = gb_ref[b] * (w0 * z2 + w1 * z1 + w2 * zb)
        last = zb[lb - (CONV_WIDTH - 1):lb]
        zprev_s[b] = last
        cst_out[b] = last

    halves = 2 if lb % (2 * SUBLANES) == 0 else 1
    lh = lb // halves
    for part in range(halves):
        rows = slice(part * lh * nb, (part + 1) * lh * nb)
        h_re = jnp.concatenate([sre[c, rows, :] for c in range(tiles)], axis=-1).astype(BF16)
        h_im = jnp.concatenate([sim[c, rows, :] for c in range(tiles)], axis=-1).astype(BF16)
        y = (jnp.dot(h_re, cre_ref[...], preferred_element_type=F32)
             - jnp.dot(h_im, cim_ref[...], preferred_element_type=F32))
        y = y + d_ref[...] * u2[rows]
        g = jax.nn.gelu(y)
        gate = jax.nn.sigmoid(jnp.dot(g.astype(BF16), wglu_ref[...], preferred_element_type=F32) + bglu_ref[...])
        ssm_ref[:, part * lh:(part + 1) * lh, :] = jnp.swapaxes((g * gate).reshape(lh, nb, SSM_DIM), 0, 1)


def _ssm_conv(u, z, gb, h0r, h0i, cprev, lw, layer, lb):
    bsz, seq, _ = u.shape
    nb = SUBLANES
    tok = pl.BlockSpec((nb, lb, SSM_DIM), lambda g, j: (g, j, 0))
    st = pl.BlockSpec((nb, STATE_LANES), lambda g, j: (g, 0))
    cs = pl.BlockSpec((nb, CONV_WIDTH - 1, CONV_DIM), lambda g, j: (g, 0, 0))
    par = lambda *shape: _layer_spec(shape, layer)
    return pl.pallas_call(
        functools.partial(_ssm_body, lb=lb),
        grid=(bsz // nb, seq // lb),
        in_specs=[tok, tok, tok, st, st, cs,
                  par(SSM_DIM, STATE_LANES), par(SSM_DIM, STATE_LANES),
                  par(STATE_LANES, SSM_DIM), par(STATE_LANES, SSM_DIM),
                  par(1, STATE_LANES), par(1, STATE_LANES),
                  par(1, SSM_DIM), par(SSM_DIM, SSM_DIM), par(1, SSM_DIM),
                  par(CONV_WIDTH, CONV_DIM)],
        out_specs=(tok, tok, st, st, cs),
        out_shape=(jax.ShapeDtypeStruct((bsz, seq, SSM_DIM), F32),
                   jax.ShapeDtypeStruct((bsz, seq, CONV_DIM), F32),
                   jax.ShapeDtypeStruct((bsz, STATE_LANES), F32),
                   jax.ShapeDtypeStruct((bsz, STATE_LANES), F32),
                   jax.ShapeDtypeStruct((bsz, CONV_WIDTH - 1, CONV_DIM), F32)),
        scratch_shapes=[pltpu.VMEM((STATE_LANES // LANES, nb * lb, LANES), F32),
                        pltpu.VMEM((STATE_LANES // LANES, nb * lb, LANES), F32),
                        pltpu.VMEM((nb, STATE_LANES), F32), pltpu.VMEM((nb, STATE_LANES), F32),
                        pltpu.VMEM((nb, CONV_WIDTH - 1, CONV_DIM), F32)],
        compiler_params=_params("arbitrary", "arbitrary"),
        name="ssm_conv",
    )(u, z, gb, h0r, h0i, cprev, lw["b_re_blk"], lw["b_im_blk"], lw["c_re_blk"], lw["c_im_blk"],
      lw["a_re"], lw["a_im"], lw["d"], lw["w_glu"], lw["b_glu"], lw["conv_w"])


def _rms_rows(t, w):
    return t * lax.rsqrt(jnp.mean(t * t, axis=-1, keepdims=True) + RMS_EPS) * w


def _mix_ffn_body(x_ref, att_ref, ssm_ref, conv_ref, bn_ref, wout_ref, ln2_ref, wup_ref, wdown_ref, o_ref,
                  *, ff_chunk):
    bn = bn_ref[...]
    mix = jnp.concatenate(
        [_rms_rows(att_ref[...], bn[:, 0:ATT_DIM]),
         _rms_rows(ssm_ref[...], bn[:, ATT_DIM:ATT_DIM + SSM_DIM]),
         _rms_rows(conv_ref[...], bn[:, ATT_DIM + SSM_DIM:])], axis=-1).astype(BF16)
    x1 = x_ref[...] + jnp.dot(mix, wout_ref[...], preferred_element_type=F32)
    h2 = _rms_rows(x1, ln2_ref[...]).astype(BF16)
    acc = x1
    for c in range(D_FF // ff_chunk):
        f = jnp.maximum(jnp.dot(h2, wup_ref[:, c * ff_chunk:(c + 1) * ff_chunk], preferred_element_type=F32), 0.0)
        acc = acc + jnp.dot((f * f).astype(BF16), wdown_ref[c * ff_chunk:(c + 1) * ff_chunk, :],
                            preferred_element_type=F32)
    o_ref[...] = acc


def _mix_ffn(x, att, ssm, conv, lw, layer, tb):
    bsz, seq, _ = x.shape
    rows = bsz * seq
    tok = lambda w: pl.BlockSpec((tb, w), lambda i: (i, 0))
    once = lambda *shape: _layer_spec(shape, layer, pipeline_mode=pl.Buffered(1))
    out = pl.pallas_call(
        functools.partial(_mix_ffn_body, ff_chunk=1024),
        grid=(rows // tb,),
        in_specs=[tok(D_MODEL), tok(ATT_DIM), tok(SSM_DIM), tok(CONV_DIM),
                  once(1, D_MODEL), once(D_MODEL, D_MODEL), once(1, D_MODEL),
                  once(D_MODEL, D_FF), once(D_FF, D_MODEL)],
        out_specs=tok(D_MODEL),
        out_shape=jax.ShapeDtypeStruct((rows, D_MODEL), F32),
        compiler_params=_params("arbitrary"),
        name="mix_ffn",
    )(x.reshape(rows, D_MODEL), att.reshape(rows, ATT_DIM), ssm.reshape(rows, SSM_DIM),
      conv.reshape(rows, CONV_DIM), lw["bn"], lw["w_out"], lw["ln2"], lw["w_up"], lw["w_down"])
    return out.reshape(bsz, seq, D_MODEL)


def _prep_weights(ln1_w, w_in, b_forget, q_norm_w, k_norm_w, conv_w, ssm_lam_re, ssm_lam_im, ssm_log_dt,
                  ssm_b_re, ssm_b_im, ssm_c_re, ssm_c_im, ssm_d, w_glu, b_glu, branch_norm_w, w_out,
                  ln2_w, w_up, w_down):
    depth = w_in.shape[0]
    fg0 = 3 * ATT_DIM
    w_re = jnp.concatenate([w_in[:, :, :fg0], w_in[:, :, fg0 + N_HEADS:], w_in[:, :, fg0:fg0 + N_HEADS],
                            jnp.zeros((depth, D_MODEL, LANES - N_HEADS), F32)], axis=2).astype(BF16)
    head_id = jnp.arange(ATT_DIM) // HEAD_DIM
    groups = depth * SSM_GROUPS
    a_re, a_im, bb_re, bb_im = _s5_prep(
        ssm_lam_re.reshape(groups, SSM_STATE), ssm_lam_im.reshape(groups, SSM_STATE), ssm_log_dt.reshape(groups),
        ssm_b_re.reshape(groups, SSM_STATE, SSM_GROUP), ssm_b_im.reshape(groups, SSM_STATE, SSM_GROUP))
    per_layer = lambda t: t.reshape(depth, SSM_GROUPS, SSM_GROUP, SSM_STATE)
    row = lambda t, w: t.reshape(depth, 1, w)
    return dict(
        ln1=row(ln1_w, D_MODEL), w_in=w_re,
        qn=row(jnp.tile(q_norm_w, (1, N_HEADS)), ATT_DIM), kn=row(jnp.tile(k_norm_w, (1, N_HEADS)), ATT_DIM),
        bf=row(jnp.pad(b_forget, ((0, 0), (0, LANES - N_HEADS))), LANES),
        mbd=(head_id[:, None] == head_id[None, :]).astype(BF16),
        a_re=row(a_re, STATE_LANES), a_im=row(a_im, STATE_LANES),
        b_re_blk=_block_diag(per_layer(bb_re)).astype(BF16), b_im_blk=_block_diag(per_layer(bb_im)).astype(BF16),
        c_re_blk=jnp.swapaxes(_block_diag(ssm_c_re), 1, 2).astype(BF16),
        c_im_blk=jnp.swapaxes(_block_diag(ssm_c_im), 1, 2).astype(BF16),
        d=row(ssm_d, SSM_DIM), w_glu=w_glu.astype(BF16), b_glu=row(b_glu, SSM_DIM),
        conv_w=conv_w, bn=row(branch_norm_w, D_MODEL), w_out=w_out.astype(BF16),
        ln2=row(ln2_w, D_MODEL), w_up=w_up.astype(BF16), w_down=w_down.astype(BF16))


def _run_layer(x, lw, layer, depth, prev, past, tiles):
    bsz, seq, _ = x.shape
    if past is None:
        c0 = jnp.zeros((bsz, 1, LANES), F32)
        h0r = jnp.zeros((bsz, STATE_LANES), F32)
        h0i = h0r
        cprev = jnp.zeros((bsz, CONV_WIDTH - 1, CONV_DIM), F32)
        qx, kx, vb, k_all, v_all, lft, u, z, gb = _inproj(x, c0, lw, tiles["tok"], layer, depth, prev, True)
        att = _attn_prompt(qx, kx, vb, tiles["attn"], tiles["attn_pairs"])
    else:
        cache_kt, cache_vt, clf_t, s_re, s_im, s_conv = past
        plen = cache_kt.shape[-1]
        ct_cache = _cache_cumsum(clf_t.reshape(bsz * N_HEADS, plen)).reshape(bsz, N_HEADS, plen)
        c0 = jnp.pad(ct_cache[:, :, plen - 1], ((0, 0), (0, LANES - N_HEADS))).reshape(bsz, 1, LANES)
        h0r = s_re.reshape(bsz, STATE_LANES)
        h0i = s_im.reshape(bsz, STATE_LANES)
        cprev = s_conv
        q, kb, vb, k_all, v_all, lft, cq, ct, u, z, gb = _inproj(x, c0, lw, tiles["tok"], layer, depth, prev, False)
        att = _attn_sample(q, kb, vb, cq, ct, cache_kt, cache_vt, ct_cache, layer)
    ssm, conv, hr, hi, cst = _ssm_conv(u, z, gb, h0r, h0i, cprev, lw, layer, tiles["scan"])
    y = _mix_ffn(x, att, ssm, conv, lw, layer, tiles["ffn"])
    small = (lft, hr.reshape(bsz, SSM_GROUPS, SSM_STATE), hi.reshape(bsz, SSM_GROUPS, SSM_STATE), cst)
    return y, (k_all, v_all), small


def _tiles(bsz, seq):
    tok = min(seq, 1024)
    return dict(tok=tok, attn=min(seq, 512), attn_pairs=2, scan=min(seq, 128), ffn=min(bsz * seq, 1024))


def _forward(x_prompt, x_sample, cache_k, cache_v, cache_logf, state_ssm_re, state_ssm_im, state_conv, *weights):
    depth = cache_k.shape[0]
    xp, xs = x_prompt, x_sample
    bp, lp = xp.shape[0], xp.shape[1]
    tiles_p = _tiles(bp, lp)
    tiles_s = _tiles(xs.shape[0], xs.shape[1])
    cache_kt = jnp.transpose(cache_k, (0, 1, 3, 4, 2))
    cache_vt = jnp.transpose(cache_v, (0, 1, 3, 4, 2))
    cache_lft = jnp.swapaxes(cache_logf, 2, 3)
    kv_p, kv_s, small_p, small_s = None, None, [], []
    lw = _prep_weights(*weights)
    for l in range(depth):
        xp, kv_p, sp = _run_layer(xp, lw, l, depth, kv_p, None, tiles_p)
        xs, kv_s, ss = _run_layer(xs, lw, l, depth, kv_s,
                                  (cache_kt, cache_vt, cache_lft[l], state_ssm_re[l], state_ssm_im[l], state_conv[l]),
                                  tiles_s)
        small_p.append(sp)
        small_s.append(ss)
    stk = lambda lst, i: jnp.stack([s[i] for s in lst])
    from_t = lambda t: jnp.transpose(t.reshape(depth, bp, N_HEADS, HEAD_DIM, lp), (0, 1, 4, 2, 3))
    return (xp, xs,
            from_t(kv_p[0]), from_t(kv_p[1]), jnp.swapaxes(stk(small_p, 0), 2, 3),
            stk(small_p, 1), stk(small_p, 2), stk(small_p, 3),
            kv_s[0], kv_s[1], jnp.swapaxes(stk(small_s, 0), 2, 3),
            stk(small_s, 1), stk(small_s, 2), stk(small_s, 3))


def kernel(x_prompt, x_sample, cache_k, cache_v, cache_logf, state_ssm_re, state_ssm_im, state_conv, ln1_w, w_in, b_forget, q_norm_w, k_norm_w, conv_w, ssm_lam_re, ssm_lam_im, ssm_log_dt, ssm_b_re, ssm_b_im, ssm_c_re, ssm_c_im, ssm_d, w_glu, b_glu, branch_norm_w, w_out, ln2_w, w_up, w_down):
    return _forward(x_prompt, x_sample, cache_k, cache_v, cache_logf, state_ssm_re, state_ssm_im, state_conv,
                    ln1_w, w_in, b_forget, q_norm_w, k_norm_w, conv_w, ssm_lam_re, ssm_lam_im, ssm_log_dt,
                    ssm_b_re, ssm_b_im, ssm_c_re, ssm_c_im, ssm_d, w_glu, b_glu, branch_norm_w, w_out,
                    ln2_w, w_up, w_down)
```
